```python
import math, functools
import jax, jax.numpy as jnp
from jax import lax
import numpy as np

D_MODEL = 1024
BATCH = 8
SEQ = 4096
DEPTH = 2
DEC_BATCH = 32
DEC_SEQ = 4
PAST_LEN = 16384
PAGE_SIZE = 128

N_GROUPS = 4
GROUP_W = D_MODEL // N_GROUPS
MIX_W = N_GROUPS * GROUP_W
N_IN_BLOCKS = 12
IN_W = N_IN_BLOCKS * GROUP_W
SSM_CH = 16
SSM_GROUPS = GROUP_W // SSM_CH
SSM_STATE = 64
CONV_K = 3
HEAD_DIM = 64
SB_HEADS = GROUP_W // HEAD_DIM
SB_BIAS_INIT = -6.0
MEM_HEADS = 4
MEM_HEAD_DIM = GROUP_W // MEM_HEADS
N_MEM = 256
Q_BLOCK = 128
EPS = 1e-6
DT_MIN = 1e-3
DT_MAX = 1e-1

kernel_name = "hymba_s5_conv_stickbreak_mem_step"


def rmsnorm(x, g):
    xf = x.astype(jnp.float32)
    y = xf * lax.rsqrt(jnp.mean(xf * xf, axis=-1, keepdims=True) + EPS)
    return (y * g.astype(jnp.float32)).astype(x.dtype)


def _ssm_combine(e1, e2):
    a1, b1 = e1
    a2, b2 = e2
    return a1 * a2, a2 * b1 + b2


def ssm_branch(u, h0_re, h0_im, lam_re, lam_im, b_re, b_im, c_re, c_im, log_dt, d, w_glu):
    f32 = jnp.float32
    n, l, _ = u.shape
    uf = u.astype(f32)
    lam = lax.complex(lam_re.astype(f32), lam_im.astype(f32))
    dt = jnp.exp(log_dt.astype(f32))[:, None]
    lam_bar = jnp.exp(lam * dt)
    b = lax.complex(b_re.astype(f32), b_im.astype(f32))
    b_bar = ((lam_bar - 1.0) / lam)[..., None] * b
    c = lax.complex(c_re.astype(f32), c_im.astype(f32))
    ug = uf.reshape(n, l, SSM_GROUPS, SSM_CH).astype(jnp.complex64)
    bu = jnp.einsum("nlgc,gpc->nlgp", ug, b_bar)
    h0 = lax.complex(h0_re.astype(f32), h0_im.astype(f32))
    bu = bu.at[:, 0].add(lam_bar * h0)
    a = jnp.broadcast_to(lam_bar, bu.shape)
    _, h = lax.associative_scan(_ssm_combine, (a, bu), axis=1)
    y = jnp.einsum("nlgp,gcp->nlgc", h, c).real.reshape(n, l, GROUP_W) + d.astype(f32) * uf
    y = jax.nn.gelu(y)
    y = y * jax.nn.sigmoid(y @ w_glu.astype(f32))
    h_last = h[:, -1]
    return y.astype(u.dtype), jnp.real(h_last), jnp.imag(h_last)


def short_conv(b_gate, c_gate, x_in, buf, w):
    v = c_gate * x_in
    vp = jnp.concatenate([buf.astype(v.dtype), v], axis=1)
    l = v.shape[1]
    y = vp[:, 0:l] * w[0]
    for j in range(1, CONV_K):
        y = y + vp[:, j:j + l] * w[j]
    return b_gate * y, vp[:, -(CONV_K - 1):]


def stick_breaking(q, k, v, bias, q_pos, k_pos):
    f32 = jnp.float32
    z = jnp.einsum("nqhd,nkhd->nhqk", q.astype(f32), k.astype(f32)) * (HEAD_DIM ** -0.5)
    z = z + bias.astype(f32)[None, :, None, None]
    mask = k_pos[None, :] < q_pos[:, None]
    log_beta = jax.nn.log_sigmoid(z)
    log_1m = jnp.where(mask, jax.nn.log_sigmoid(-z), 0.0)
    rev = lax.cumsum(log_1m, axis=3, reverse=True)
    excl = jnp.concatenate([rev[..., 1:], jnp.zeros_like(rev[..., :1])], axis=-1)
    wts = jnp.where(mask, jnp.exp(log_beta + excl), 0.0)
    return jnp.einsum("nhqk,nkhd->nqhd", wts, v.astype(f32)).astype(q.dtype)


def sb_prompt(q, k, v, bias):
    n, l, h, dh = q.shape
    nb = l // Q_BLOCK
    qb = q.reshape(n, nb, Q_BLOCK, h, dh).transpose(1, 0, 2, 3, 4)
    k_pos = jnp.arange(l, dtype=jnp.int32)

    def block(args):
        qi, bi = args
        q_pos = bi * Q_BLOCK + jnp.arange(Q_BLOCK, dtype=jnp.int32)
        return stick_breaking(qi, k, v, bias, q_pos, k_pos)

    out = lax.map(block, (qb, jnp.arange(nb, dtype=jnp.int32)))
    return out.transpose(1, 0, 2, 3, 4).reshape(n, l, h, dh)


def sb_sample(q, k, v, bias, k_past, v_past):
    past = k_past.shape[1]
    k_all = jnp.concatenate([k_past.astype(k.dtype), k], axis=1)
    v_all = jnp.concatenate([v_past.astype(v.dtype), v], axis=1)
    q_pos = past + jnp.arange(q.shape[1], dtype=jnp.int32)
    k_pos = jnp.arange(k_all.shape[1], dtype=jnp.int32)
    return stick_breaking(q, k_all, v_all, bias, q_pos, k_pos)


def mem_kv(mem, w):
    n, m, _ = mem.shape
    mk, mv = jnp.split(mem @ w, 2, axis=-1)
    return (mk.reshape(n, m, MEM_HEADS, MEM_HEAD_DIM), mv.reshape(n, m, MEM_HEADS, MEM_HEAD_DIM))


def mem_attend(q, mk, mv):
    f32 = jnp.float32
    s = jnp.einsum("nlhd,nmhd->nhlm", q.astype(f32), mk.astype(f32)) * (MEM_HEAD_DIM ** -0.5)
    p = jax.nn.softmax(s, axis=-1)
    return jnp.einsum("nhlm,nmhd->nlhd", p, mv.astype(f32)).astype(q.dtype)


def mixer_layer(x, lp, h0_re, h0_im, conv_buf, attend_fn, mk, mv):
    n, l, _ = x.shape
    h = rmsnorm(x, lp["norm_g"]) @ lp["w_in"]
    (a_u, a_g, b_b, b_c, b_x, b_g, c_q, c_k, c_v, c_g, m_q, m_g) = jnp.split(h, N_IN_BLOCKS, axis=-1)
    y_a, h_re, h_im = ssm_branch(a_u, h0_re, h0_im, lp["lam_re"], lp["lam_im"], lp["b_re"], lp["b_im"],
                                 lp["c_re"], lp["c_im"], lp["log_dt"], lp["d"], lp["w_glu"])
    y_b, new_buf = short_conv(b_b, b_c, b_x, conv_buf, lp["conv_w"])
    q = c_q.reshape(n, l, SB_HEADS, HEAD_DIM)
    k = c_k.reshape(n, l, SB_HEADS, HEAD_DIM)
    v = c_v.reshape(n, l, SB_HEADS, HEAD_DIM)
    y_c = attend_fn(q, k, v, lp["sb_bias"]).reshape(n, l, GROUP_W)
    y_m = mem_attend(m_q.reshape(n, l, MEM_HEADS, MEM_HEAD_DIM), mk, mv).reshape(n, l, GROUP_W)
    gn = lp["group_norm_g"]
    merged = jnp.concatenate([
        rmsnorm(y_a, gn[0]) * jax.nn.silu(a_g),
        rmsnorm(y_b, gn[1]) * jax.nn.silu(b_g),
        rmsnorm(y_c, gn[2]) * jax.nn.silu(c_g),
        rmsnorm(y_m, gn[3]) * jax.nn.silu(m_g)], axis=-1)
    return x + merged @ lp["w_out"], h_re, h_im, new_buf, k, v


def setup_inputs(seed: int = 0) -> dict:
    key = jax.random.key(seed)
    ks = list(jax.random.split(key, 32))
    f32 = jnp.float32

    def nrm(i, shape, scale):
        return jax.random.normal(ks[i], shape, f32) * scale

    n_pages = PAST_LEN // PAGE_SIZE
    n_used = DEC_BATCH * n_pages
    n_phys = n_used + n_used // 4
    perm = jax.random.permutation(ks[0], n_phys)
    page_table = perm[:n_used].reshape(DEC_BATCH, n_pages).astype(jnp.int32)

    x_prompt = nrm(1, (BATCH, SEQ, D_MODEL), 1.0)
    x_sample = nrm(2, (DEC_BATCH, DEC_SEQ, D_MODEL), 1.0)
    cache_sb_k = nrm(3, (n_phys, DEPTH, PAGE_SIZE, SB_HEADS, HEAD_DIM), 1.0)
    cache_sb_v = nrm(4, (n_phys, DEPTH, PAGE_SIZE, SB_HEADS, HEAD_DIM), 1.0)
    state_ssm_re = nrm(5, (DEC_BATCH, DEPTH, SSM_GROUPS, SSM_STATE), 0.5)
    state_ssm_im = nrm(6, (DEC_BATCH, DEPTH, SSM_GROUPS, SSM_STATE), 0.5)
    state_conv = nrm(7, (DEC_BATCH, DEPTH, CONV_K - 1, GROUP_W), 1.0)
    cache_mem_k = nrm(8, (DEC_BATCH, DEPTH, N_MEM, MEM_HEADS, MEM_HEAD_DIM), 1.0)
    cache_mem_v = nrm(9, (DEC_BATCH, DEPTH, N_MEM, MEM_HEADS, MEM_HEAD_DIM), 1.0)
    mem_prompt = nrm(10, (BATCH, N_MEM, D_MODEL), 1.0)

    norm_g = 1.0 + nrm(11, (DEPTH, D_MODEL), 0.02)
    w_in = nrm(12, (DEPTH, D_MODEL, IN_W), D_MODEL ** -0.5)
    w_out = nrm(13, (DEPTH, MIX_W, D_MODEL), MIX_W ** -0.5)
    group_norm_g = 1.0 + nrm(14, (DEPTH, N_GROUPS, GROUP_W), 0.02)
    ssm_lambda_re = -0.5 + nrm(15, (DEPTH, SSM_GROUPS, SSM_STATE), 0.01)
    ssm_lambda_im = math.pi * jnp.arange(SSM_STATE, dtype=f32) + nrm(16, (DEPTH, SSM_GROUPS, SSM_STATE), 0.01)
    ssm_b_re = nrm(17, (DEPTH, SSM_GROUPS, SSM_STATE, SSM_CH), (2 * SSM_CH) ** -0.5)
    ssm_b_im = nrm(18, (DEPTH, SSM_GROUPS, SSM_STATE, SSM_CH), (2 * SSM_CH) ** -0.5)
    ssm_c_re = nrm(19, (DEPTH, SSM_GROUPS, SSM_CH, SSM_STATE), (2 * SSM_STATE) ** -0.5)
    ssm_c_im = nrm(20, (DEPTH, SSM_GROUPS, SSM_CH, SSM_STATE), (2 * SSM_STATE) ** -0.5)
    ssm_log_dt = jax.random.uniform(ks[21], (DEPTH, SSM_GROUPS), f32, math.log(DT_MIN), math.log(DT_MAX))
    ssm_d = nrm(22, (DEPTH, GROUP_W), 1.0)
    ssm_w_glu = nrm(23, (DEPTH, GROUP_W, GROUP_W), GROUP_W ** -0.5)
    conv_w = nrm(24, (DEPTH, CONV_K, GROUP_W), CONV_K ** -0.5)
    w_mem_kv = nrm(25, (DEPTH, D_MODEL, 2 * GROUP_W), D_MODEL ** -0.5)
    final_norm_g = 1.0 + nrm(26, (D_MODEL,), 0.02)
    sb_bias = SB_BIAS_INIT + nrm(27, (DEPTH, SB_HEADS), 0.1)
    return {
        "x_prompt": x_prompt, "x_sample": x_sample,
        "cache_sb_k": cache_sb_k, "cache_sb_v": cache_sb_v,
        "state_ssm_re": state_ssm_re, "state_ssm_im": state_ssm_im, "state_conv": state_conv,
        "cache_mem_k": cache_mem_k, "cache_mem_v": cache_mem_v,
        "page_table": page_table, "mem_prompt": mem_prompt,
        "norm_g": norm_g, "w_in": w_in, "w_out": w_out, "group_norm_g": group_norm_g,
        "ssm_lambda_re": ssm_lambda_re, "ssm_lambda_im": ssm_lambda_im,
        "ssm_b_re": ssm_b_re, "ssm_b_im": ssm_b_im, "ssm_c_re": ssm_c_re, "ssm_c_im": ssm_c_im,
        "ssm_log_dt": ssm_log_dt, "ssm_d": ssm_d, "ssm_w_glu": ssm_w_glu,
        "conv_w": conv_w, "sb_bias": sb_bias, "w_mem_kv": w_mem_kv, "final_norm_g": final_norm_g,
    }


def reference(x_prompt, x_sample, cache_sb_k, cache_sb_v, state_ssm_re, state_ssm_im, state_conv,
              cache_mem_k, cache_mem_v, page_table, mem_prompt, norm_g, w_in, w_out, group_norm_g,
              ssm_lambda_re, ssm_lambda_im, ssm_b_re, ssm_b_im, ssm_c_re, ssm_c_im, ssm_log_dt, ssm_d,
              ssm_w_glu, conv_w, sb_bias, w_mem_kv, final_norm_g):
    n_p = x_prompt.shape[0]
    n_s = x_sample.shape[0]
    past_len = page_table.shape[1] * cache_sb_k.shape[2]
    xp, xs = x_prompt, x_sample
    zeros_h = jnp.zeros((n_p, SSM_GROUPS, SSM_STATE), jnp.float32)
    zeros_buf = jnp.zeros((n_p, CONV_K - 1, GROUP_W), x_prompt.dtype)
    p_k, p_v, p_re, p_im, p_conv, p_mk, p_mv = [], [], [], [], [], [], []
    s_k, s_v, s_re, s_im, s_conv = [], [], [], [], []
    for i in range(DEPTH):
        lp = {"norm_g": norm_g[i], "w_in": w_in[i], "w_out": w_out[i], "group_norm_g": group_norm_g[i],
              "lam_re": ssm_lambda_re[i], "lam_im": ssm_lambda_im[i], "b_re": ssm_b_re[i],
              "b_im": ssm_b_im[i], "c_re": ssm_c_re[i], "c_im": ssm_c_im[i], "log_dt": ssm_log_dt[i],
              "d": ssm_d[i], "w_glu": ssm_w_glu[i], "conv_w": conv_w[i], "sb_bias": sb_bias[i]}
        mk, mv = mem_kv(mem_prompt, w_mem_kv[i])
        xp, h_re, h_im, buf, k, v = mixer_layer(xp, lp, zeros_h, zeros_h, zeros_buf, sb_prompt, mk, mv)
        p_k.append(k); p_v.append(v); p_re.append(h_re); p_im.append(h_im)
        p_conv.append(buf); p_mk.append(mk); p_mv.append(mv)
        k_past = cache_sb_k[page_table, i].reshape(n_s, past_len, SB_HEADS, HEAD_DIM)
        v_past = cache_sb_v[page_table, i].reshape(n_s, past_len, SB_HEADS, HEAD_DIM)
        attend = functools.partial(sb_sample, k_past=k_past, v_past=v_past)
        xs, h_re, h_im, buf, k, v = mixer_layer(xs, lp, state_ssm_re[:, i], state_ssm_im[:, i],
                                                state_conv[:, i], attend,
                                                cache_mem_k[:, i], cache_mem_v[:, i])
        s_k.append(k); s_v.append(v); s_re.append(h_re); s_im.append(h_im); s_conv.append(buf)
    y_prompt = rmsnorm(xp, final_norm_g)
    y_sample = rmsnorm(xs, final_norm_g)
    return (y_prompt, y_sample,
            jnp.stack(p_k, axis=1), jnp.stack(p_v, axis=1),
            jnp.stack(p_re, axis=1), jnp.stack(p_im, axis=1), jnp.stack(p_conv, axis=1),
            jnp.stack(p_mk, axis=1), jnp.stack(p_mv, axis=1),
            jnp.stack(s_k, axis=1), jnp.stack(s_v, axis=1),
            jnp.stack(s_re, axis=1), jnp.stack(s_im, axis=1), jnp.stack(s_conv, axis=1))
```

```python
import functools
import math

import jax
import jax.numpy as jnp
from jax import lax
from jax.experimental import pallas as pl
from jax.experimental.pallas import tpu as pltpu

F32 = jnp.float32
BF16 = jnp.bfloat16

EPS = 1e-6
GROUP_W = 256
N_IN_BLOCKS = 12
HEAD_DIM = 64
N_HEADS = 4
SSM_GROUPS = 16
SSM_CH = 16
SSM_STATE = 64
SSM_W = SSM_GROUPS * SSM_STATE
CONV_K = 3

(A_U, A_G, B_B, B_C, B_X, B_G, C_Q, C_K, C_V, C_G, M_Q, M_G) = range(N_IN_BLOCKS)

VMEM_LIMIT = 48 * 1024 * 1024


def _cparams(*sem):
    return pltpu.CompilerParams(dimension_semantics=sem, vmem_limit_bytes=VMEM_LIMIT)


def _dot(a, b):
    return jnp.dot(a, b, preferred_element_type=F32)


def _dot_nt(a, b):
    return lax.dot_general(a, b, (((1,), (1,)), ((), ())), preferred_element_type=F32)


def _rms(x, g):
    return x * lax.rsqrt(jnp.mean(x * x, axis=-1, keepdims=True) + EPS) * g


def _silu(x):
    return x * jax.nn.sigmoid(x)


def _inproj_kernel(x_ref, g_ref, w_ref, o_ref):
    xn = _rms(x_ref[...], g_ref[...])
    o_ref[...] = _dot(xn.astype(BF16), w_ref[...])


def _matmul_kernel(x_ref, w_ref, o_ref):
    o_ref[...] = _dot(x_ref[...].astype(BF16), w_ref[...])


def _row_tile(rows, target):
    return min(rows, target)


def in_proj(x2d, g, w_bf16):
    rows, d = x2d.shape
    c = w_bf16.shape[1]
    tm = _row_tile(rows, 256)
    return pl.pallas_call(
        _inproj_kernel,
        grid=(rows // tm,),
        in_specs=[pl.BlockSpec((tm, d), lambda i: (i, 0)),
                  pl.BlockSpec((1, d), lambda i: (0, 0)),
                  pl.BlockSpec((d, c), lambda i: (0, 0))],
        out_specs=pl.BlockSpec((tm, c), lambda i: (i, 0)),
        out_shape=jax.ShapeDtypeStruct((rows, c), F32),
        compiler_params=_cparams("parallel"),
        name="in_proj",
    )(x2d, g.reshape(1, d), w_bf16)


def matmul(x2d, w_bf16):
    rows, d = x2d.shape
    c = w_bf16.shape[1]
    tm = _row_tile(rows, 256)
    return pl.pallas_call(
        _matmul_kernel,
        grid=(rows // tm,),
        in_specs=[pl.BlockSpec((tm, d), lambda i: (i, 0)),
                  pl.BlockSpec((d, c), lambda i: (0, 0))],
        out_specs=pl.BlockSpec((tm, c), lambda i: (i, 0)),
        out_shape=jax.ShapeDtypeStruct((rows, c), F32),
        compiler_params=_cparams("parallel"),
        name="mem_kv",
    )(x2d, w_bf16)


def _ssm_disc_kernel(lre_ref, lim_ref, dt_ref, bre_ref, bim_ref,
                     lbre_ref, lbim_ref, bbre_ref, bbim_ref):
    lre = lre_ref[...]
    lim = lim_ref[...]
    dt = jnp.exp(dt_ref[...])
    mag = jnp.exp(lre * dt)
    lbre = mag * jnp.cos(lim * dt)
    lbim = mag * jnp.sin(lim * dt)
    lbre_ref[...] = lbre
    lbim_ref[...] = lbim
    nre = lbre - 1.0
    nim = lbim
    den = lre * lre + lim * lim
    cre = (nre * lre + nim * lim) / den
    cim = (nim * lre - nre * lim) / den
    bre = bre_ref[...]
    bim = bim_ref[...]
    bbre_ref[...] = cre * bre - cim * bim
    bbim_ref[...] = cre * bim + cim * bre


def ssm_discretise(lam_re, lam_im, log_dt, b_re, b_im):
    col = lambda a: a.reshape(SSM_W, 1)
    dt_col = jnp.broadcast_to(log_dt[:, None], (SSM_GROUPS, SSM_STATE)).reshape(SSM_W, 1)
    outs = pl.pallas_call(
        _ssm_disc_kernel,
        out_shape=[jax.ShapeDtypeStruct((SSM_W, 1), F32)] * 2
        + [jax.ShapeDtypeStruct((SSM_W, SSM_CH), F32)] * 2,
        name="ssm_discretise",
    )(col(lam_re), col(lam_im), dt_col, b_re.reshape(SSM_W, SSM_CH), b_im.reshape(SSM_W, SSM_CH))
    lbre, lbim, bbre, bbim = outs
    shp = (SSM_GROUPS, SSM_STATE, SSM_CH)
    return lbre.reshape(1, SSM_W), lbim.reshape(1, SSM_W), bbre.reshape(shp), bbim.reshape(shp)


def _block_diag_in(b_gpc):
    eye = jnp.eye(SSM_GROUPS, dtype=F32)
    m = b_gpc.transpose(0, 2, 1)[:, :, None, :] * eye[:, None, :, None]
    return m.reshape(SSM_GROUPS * SSM_CH, SSM_W)


def _block_diag_out(c_gcp):
    eye = jnp.eye(SSM_GROUPS, dtype=F32)
    m = c_gcp.transpose(0, 2, 1)[:, :, None, :] * eye[:, None, :, None]
    return m.reshape(SSM_W, SSM_GROUPS * SSM_CH)


def _hi_lo(a):
    hi = a.astype(BF16)
    lo = (a - hi.astype(F32)).astype(BF16)
    return hi, lo


def _dot3(a, b_hi, b_lo):
    a_hi, a_lo = _hi_lo(a)
    return _dot(a_hi, b_hi) + (_dot(a_lo, b_hi) + _dot(a_hi, b_lo))


def _ssm_kernel(n_seq, t_chunk,
                u_ref, h0re_ref, h0im_ref, lre_ref, lim_ref,
                bre_hi_ref, bre_lo_ref, bim_hi_ref, bim_lo_ref,
                cre_ref, cimn_ref, d_ref, wglu_ref,
                y_ref, hre_ref, him_ref,
                bu_re, bu_im, hs_re, hs_im):
    step = pl.program_id(0)

    @pl.when(step == 0)
    def _():
        hre_ref[...] = h0re_ref[...]
        him_ref[...] = h0im_ref[...]

    u = u_ref[...]
    bu_re[...] = _dot3(u, bre_hi_ref[...], bre_lo_ref[...])
    bu_im[...] = _dot3(u, bim_hi_ref[...], bim_lo_ref[...])

    lre = jnp.broadcast_to(lre_ref[...], (n_seq, SSM_W))
    lim = jnp.broadcast_to(lim_ref[...], (n_seq, SSM_W))

    def body(t, carry):
        hr, hi = carry
        rows = pl.ds(pl.multiple_of(t * n_seq, n_seq), n_seq)
        nr = lre * hr - lim * hi + bu_re[rows, :]
        ni = lre * hi + lim * hr + bu_im[rows, :]
        hs_re[rows, :] = nr
        hs_im[rows, :] = ni
        return nr, ni

    hr, hi = lax.fori_loop(0, t_chunk, body, (hre_ref[...], him_ref[...]))
    hre_ref[...] = hr
    him_ref[...] = hi

    y = (_dot(hs_re[...].astype(BF16), cre_ref[...])
         + _dot(hs_im[...].astype(BF16), cimn_ref[...])
         + d_ref[...] * u)
    y = jax.nn.gelu(y)
    y_ref[...] = y * jax.nn.sigmoid(_dot(y.astype(BF16), wglu_ref[...]))


def ssm_scan(u_tm, h0_re, h0_im, sp, n_seq, t_len):
    rows = u_tm.shape[0]
    t_chunk = min(t_len, 64)
    tr = t_chunk * n_seq
    full = lambda shape: pl.BlockSpec(shape, lambda i: (0,) * len(shape))
    return pl.pallas_call(
        functools.partial(_ssm_kernel, n_seq, t_chunk),
        grid=(t_len // t_chunk,),
        in_specs=[pl.BlockSpec((tr, GROUP_W), lambda i: (i, 0)),
                  full((n_seq, SSM_W)), full((n_seq, SSM_W)),
                  full((1, SSM_W)), full((1, SSM_W)),
                  full((GROUP_W, SSM_W)), full((GROUP_W, SSM_W)),
                  full((GROUP_W, SSM_W)), full((GROUP_W, SSM_W)),
                  full((SSM_W, GROUP_W)), full((SSM_W, GROUP_W)),
                  full((1, GROUP_W)), full((GROUP_W, GROUP_W))],
        out_specs=[pl.BlockSpec((tr, GROUP_W), lambda i: (i, 0)),
                   full((n_seq, SSM_W)), full((n_seq, SSM_W))],
        out_shape=[jax.ShapeDtypeStruct((rows, GROUP_W), F32),
                   jax.ShapeDtypeStruct((n_seq, SSM_W), F32),
                   jax.ShapeDtypeStruct((n_seq, SSM_W), F32)],
        scratch_shapes=[pltpu.VMEM((tr, SSM_W), F32)] * 4,
        compiler_params=_cparams("arbitrary"),
        name="ssm_scan",
    )(u_tm, h0_re, h0_im, sp["lbre"], sp["lbim"],
      sp["bre_hi"], sp["bre_lo"], sp["bim_hi"], sp["bim_lo"],
      sp["cre"], sp["cimn"], sp["d"], sp["wglu"])


def _suffix_matrix(n):
    j = lax.broadcasted_iota(jnp.int32, (n, n), 0)
    s = lax.broadcasted_iota(jnp.int32, (n, n), 1)
    return (j > s).astype(BF16)


def _sb_block(z, mask, carry, suffix, v_bf16):
    log1p_term = jnp.log1p(jnp.exp(-jnp.abs(z)))
    log_1m = -(jnp.maximum(z, 0.0) + log1p_term)
    log_beta = jnp.minimum(z, 0.0) - log1p_term
    if mask is not None:
        log_1m = jnp.where(mask, log_1m, 0.0)
    l_bf = log_1m.astype(BF16)
    local = _dot(l_bf, suffix)
    w = jnp.exp(log_beta + (local + carry))
    if mask is not None:
        w = jnp.where(mask, w, 0.0)
    out = _dot(w.astype(BF16), v_bf16)
    new_carry = carry + (local[:, 0:1] + l_bf[:, 0:1].astype(F32))
    return out, new_carry


def _sb_prompt_kernel(tq, bias_ref, q_ref, k_ref, v_ref, o_ref):
    i = pl.program_id(1)
    suffix = _suffix_matrix(tq)
    row = lax.broadcasted_iota(jnp.int32, (tq, tq), 0)
    col = lax.broadcasted_iota(jnp.int32, (tq, tq), 1)
    diag_mask = col < row
    scale = HEAD_DIM ** -0.5
    outs = []
    for h in range(N_HEADS):
        lanes = slice(h * HEAD_DIM, (h + 1) * HEAD_DIM)
        q_h = (q_ref[:, lanes] * scale).astype(BF16)
        bias = bias_ref[h]

        def block(kb, carry, mask, q_h=q_h, bias=bias, lanes=lanes):
            rows = pl.ds(pl.multiple_of(kb * tq, tq), tq)
            k_h = k_ref[rows, lanes].astype(BF16)
            v_h = v_ref[rows, lanes].astype(BF16)
            z = _dot_nt(q_h, k_h) + bias
            return _sb_block(z, mask, carry, suffix, v_h)

        acc, carry = block(i, jnp.zeros((tq, 1), F32), diag_mask)

        def body(j, state, block=block):
            acc, carry = state
            out, carry = block(i - 1 - j, carry, None)
            return acc + out, carry

        acc, _ = lax.fori_loop(0, i, body, (acc, carry))
        outs.append(acc)
    o_ref[...] = jnp.concatenate(outs, axis=-1)


def sb_prompt(h2d, bias, n_seq, t_len):
    tq = min(t_len, 256)
    nb = t_len // tq
    return pl.pallas_call(
        functools.partial(_sb_prompt_kernel, tq),
        grid=(n_seq, nb),
        in_specs=[pl.BlockSpec(memory_space=pltpu.SMEM),
                  pl.BlockSpec((tq, GROUP_W), lambda n, i: (n * nb + i, C_Q)),
                  pl.BlockSpec((t_len, GROUP_W), lambda n, i: (n, C_K)),
                  pl.BlockSpec((t_len, GROUP_W), lambda n, i: (n, C_V))],
        out_specs=pl.BlockSpec((tq, GROUP_W), lambda n, i: (n * nb + i, 0)),
        out_shape=jax.ShapeDtypeStruct((n_seq * t_len, GROUP_W), F32),
        compiler_params=_cparams("parallel", "arbitrary"),
        name="sb_prompt",
    )(bias, h2d, h2d, h2d)


def _head_of_lane(shape, axis):
    return lax.broadcasted_iota(jnp.int32, shape, axis) // HEAD_DIM


def _sb_sample_kernel(t_new, page, pt_ref, bias_ref, qrep_ref, knew_ref, vnew_ref,
                      kp_ref, vp_ref, o_ref, qbd_ref, acc_ref, carry_ref):
    del pt_ref
    p = pl.program_id(1)
    rows_q = N_HEADS * t_new
    suffix = _suffix_matrix(page)
    row_head = lax.broadcasted_iota(jnp.int32, (rows_q, 1), 0) // t_new
    bias_col = jnp.zeros((rows_q, 1), F32)
    for h in range(N_HEADS):
        bias_col = jnp.where(row_head == h, bias_ref[h], bias_col)

    @pl.when(p == 0)
    def _():
        same_head = _head_of_lane((rows_q, GROUP_W), 1) == row_head
        qbd = jnp.where(same_head, qrep_ref[...] * (HEAD_DIM ** -0.5), 0.0).astype(BF16)
        qbd_ref[...] = qbd
        t_of_row = lax.broadcasted_iota(jnp.int32, (rows_q, page), 0) % t_new
        key = lax.broadcasted_iota(jnp.int32, (rows_q, page), 1)
        z = _dot_nt(qbd, knew_ref[...].astype(BF16)) + bias_col
        out, carry = _sb_block(z, key < t_of_row, jnp.zeros((rows_q, 1), F32), suffix,
                               vnew_ref[...].astype(BF16))
        acc_ref[...] = out
        carry_ref[...] = carry

    z = _dot_nt(qbd_ref[...], kp_ref[...].astype(BF16)) + bias_col
    out, carry = _sb_block(z, None, carry_ref[...], suffix, vp_ref[...].astype(BF16))
    acc_ref[...] += out
    carry_ref[...] = carry

    @pl.when(p == pl.num_programs(1) - 1)
    def _():
        o_ref[...] = acc_ref[...]


def sb_sample(q, k_new, v_new, bias, cache_k, cache_v, page_table, layer):
    n_seq, t_new, _ = q.shape
    page = cache_k.shape[2]
    n_pages = page_table.shape[1]
    rows_q = N_HEADS * t_new
    q_rep = jnp.tile(q, (1, N_HEADS, 1))
    pad = ((0, 0), (0, page - t_new), (0, 0))
    k_pad = jnp.pad(k_new, pad)
    v_pad = jnp.pad(v_new, pad)

    def cache_map(n, p, pt):
        return (pt[n, n_pages - 1 - p], layer, 0, 0)

    per_seq = lambda r: pl.BlockSpec((None, r, GROUP_W), lambda n, p, pt: (n, 0, 0))
    acc = pl.pallas_call(
        functools.partial(_sb_sample_kernel, t_new, page),
        grid_spec=pltpu.PrefetchScalarGridSpec(
            num_scalar_prefetch=1,
            grid=(n_seq, n_pages),
            in_specs=[pl.BlockSpec(memory_space=pltpu.SMEM),
                      per_seq(rows_q), per_seq(page), per_seq(page),
                      pl.BlockSpec((None, None, page, GROUP_W), cache_map),
                      pl.BlockSpec((None, None, page, GROUP_W), cache_map)],
            out_specs=per_seq(rows_q),
            scratch_shapes=[pltpu.VMEM((rows_q, GROUP_W), BF16),
                            pltpu.VMEM((rows_q, GROUP_W), F32),
                            pltpu.VMEM((rows_q, 1), F32)]),
        out_shape=jax.ShapeDtypeStruct((n_seq, rows_q, GROUP_W), F32),
        compiler_params=_cparams("parallel", "arbitrary"),
        name="sb_sample",
    )(page_table, bias, q_rep, k_pad, v_pad, cache_k, cache_v)
    acc = acc.reshape(n_seq, N_HEADS, t_new, N_HEADS, HEAD_DIM)
    heads = [acc[:, h, :, h, :] for h in range(N_HEADS)]
    return jnp.stack(heads, axis=2).reshape(n_seq, t_new, GROUP_W)


def _mem_attend_head(q_h, mk_h, mv_h):
    s = _dot_nt(q_h, mk_h)
    e = jnp.exp(s - jnp.max(s, axis=-1, keepdims=True))
    return _dot(e.astype(BF16), mv_h) / jnp.sum(e, axis=-1, keepdims=True)


def _merge_tail(x, y_a, y_b, y_c, y_m, g_a, g_b, g_c, g_m, gn_ref, wout_ref):
    merged = jnp.concatenate([
        _rms(y_a, gn_ref[0:1, :]) * _silu(g_a),
        _rms(y_b, gn_ref[1:2, :]) * _silu(g_b),
        _rms(y_c, gn_ref[2:3, :]) * _silu(g_c),
        _rms(y_m, gn_ref[3:4, :]) * _silu(g_m)], axis=-1)
    return x + _dot(merged.astype(BF16), wout_ref[...])


def _conv_taps(v, vm1, vm2, b_gate, cw_ref):
    return b_gate * (vm2 * cw_ref[0:1, :] + vm1 * cw_ref[1:2, :] + v * cw_ref[2:3, :])


def _merge_prompt_kernel(tm, final,
                         x_ref, ag_ref, bb_ref, bc_ref, bx_ref, bg_ref, cg_ref, mq_ref, mg_ref,
                         hbc_ref, hbx_ref, buf_ref, ya_ref, yc_ref, mk_ref, mv_ref,
                         gn_ref, cw_ref, wout_ref, fg_ref,
                         o_ref, tail_ref):
    i = pl.program_id(1)
    v = bc_ref[...] * bx_ref[...]
    halo = hbc_ref[...] * hbx_ref[...]
    first = i == 0
    prev1 = jnp.where(first, buf_ref[1:2, :], halo[7:8, :])
    prev2 = jnp.where(first, buf_ref[0:1, :], halo[6:7, :])
    row = lax.broadcasted_iota(jnp.int32, (tm, 1), 0)
    vm1 = jnp.where(row == 0, prev1, pltpu.roll(v, 1, 0))
    vm2 = jnp.where(row == 0, prev2, jnp.where(row == 1, prev1, pltpu.roll(v, 2, 0)))
    y_b = _conv_taps(v, vm1, vm2, bb_ref[...], cw_ref)
    tail_ref[...] = v[tm - 8:tm, :]

    scale = HEAD_DIM ** -0.5
    heads = []
    for h in range(N_HEADS):
        lanes = slice(h * HEAD_DIM, (h + 1) * HEAD_DIM)
        q_h = (mq_ref[:, lanes] * scale).astype(BF16)
        heads.append(_mem_attend_head(q_h, mk_ref[:, lanes].astype(BF16),
                                      mv_ref[:, lanes].astype(BF16)))
    y_m = jnp.concatenate(heads, axis=-1)

    x_new = _merge_tail(x_ref[...], ya_ref[...], y_b, yc_ref[...], y_m,
                        ag_ref[...], bg_ref[...], cg_ref[...], mg_ref[...], gn_ref, wout_ref)
    o_ref[...] = _rms(x_new, fg_ref[...]) if final else x_new


def merge_prompt(x2d, h2d, y_a, y_c, conv_buf, mk, mv, gn, conv_w, wout_bf16, final_g,
                 n_seq, t_len, final):
    d = x2d.shape[1]
    n_mem = mk.shape[0] // n_seq
    tm = min(t_len, 256)
    nb = t_len // tm
    hblk = lambda c: pl.BlockSpec((tm, GROUP_W), lambda n, i, c=c: (n * nb + i, c))
    halo = lambda c: pl.BlockSpec(
        (8, GROUP_W), lambda n, i, c=c: (jnp.maximum((n * nb + i) * (tm // 8) - 1, 0), c))
    rows = pl.BlockSpec((tm, GROUP_W), lambda n, i: (n * nb + i, 0))
    full = lambda shape: pl.BlockSpec(shape, lambda n, i: (0,) * len(shape))
    out, tail = pl.pallas_call(
        functools.partial(_merge_prompt_kernel, tm, final),
        grid=(n_seq, nb),
        in_specs=[pl.BlockSpec((tm, d), lambda n, i: (n * nb + i, 0)),
                  hblk(A_G), hblk(B_B), hblk(B_C), hblk(B_X), hblk(B_G), hblk(C_G),
                  hblk(M_Q), hblk(M_G), halo(B_C), halo(B_X),
                  pl.BlockSpec((None, CONV_K - 1, GROUP_W), lambda n, i: (n, 0, 0)),
                  rows, rows,
                  pl.BlockSpec((n_mem, GROUP_W), lambda n, i: (n, 0)),
                  pl.BlockSpec((n_mem, GROUP_W), lambda n, i: (n, 0)),
                  full((4, GROUP_W)), full((CONV_K, GROUP_W)), full(wout_bf16.shape),
                  full((1, d))],
        out_specs=[pl.BlockSpec((tm, d), lambda n, i: (n * nb + i, 0)),
                   pl.BlockSpec((None, 8, GROUP_W), lambda n, i: (n, 0, 0))],
        out_shape=[jax.ShapeDtypeStruct(x2d.shape, F32),
                   jax.ShapeDtypeStruct((n_seq, 8, GROUP_W), F32)],
        compiler_params=_cparams("parallel", "arbitrary"),
        name="merge_prompt",
    )(x2d, h2d, h2d, h2d, h2d, h2d, h2d, h2d, h2d, h2d, h2d, conv_buf, y_a, y_c, mk, mv,
      gn, conv_w, wout_bf16, final_g.reshape(1, d))
    return out, tail[:, 8 - (CONV_K - 1):, :]


def _merge_sample_kernel(t_len, seqs, layer_unused, final,
                         x_ref, ag_ref, bb_ref, bc_ref, bx_ref, bg_ref, cg_ref, mq_ref, mg_ref,
                         buf_ref, ya_ref, yc_ref, mk_ref, mv_ref,
                         gn_ref, cw_ref, wout_ref, fg_ref,
                         o_ref, v_ref):
    del layer_unused
    tm = seqs * t_len
    row = lax.broadcasted_iota(jnp.int32, (tm, 1), 0)
    seq_of_row = row // t_len
    t_of_row = row % t_len

    v = bc_ref[...] * bx_ref[...]
    v_ref[...] = v
    prev1 = jnp.zeros((tm, GROUP_W), F32)
    prev2 = jnp.zeros((tm, GROUP_W), F32)
    for s in range(seqs):
        prev1 = jnp.where(seq_of_row == s, buf_ref[s, 1:2, :], prev1)
        prev2 = jnp.where(seq_of_row == s, buf_ref[s, 0:1, :], prev2)
    vm1 = jnp.where(t_of_row == 0, prev1, pltpu.roll(v, 1, 0))
    vm2 = jnp.where(t_of_row == 0, prev2, jnp.where(t_of_row == 1, prev1, pltpu.roll(v, 2, 0)))
    y_b = _conv_taps(v, vm1, vm2, bb_ref[...], cw_ref)

    scale = HEAD_DIM ** -0.5
    heads = []
    for h in range(N_HEADS):
        lanes = slice(h * HEAD_DIM, (h + 1) * HEAD_DIM)
        q_h = (mq_ref[:, lanes] * scale).astype(BF16)
        y_h = jnp.zeros((tm, HEAD_DIM), F32)
        for s in range(seqs):
            o = _mem_attend_head(q_h, mk_ref[s, :, lanes].astype(BF16),
                                 mv_ref[s, :, lanes].astype(BF16))
            y_h = jnp.where(seq_of_row == s, o, y_h)
        heads.append(y_h)
    y_m = jnp.concatenate(heads, axis=-1)

    x_new = _merge_tail(x_ref[...], ya_ref[...], y_b, yc_ref[...], y_m,
                        ag_ref[...], bg_ref[...], cg_ref[...], mg_ref[...], gn_ref, wout_ref)
    o_ref[...] = _rms(x_new, fg_ref[...]) if final else x_new


def merge_sample(x2d, h2d, y_a, y_c, state_conv, cache_mk, cache_mv, gn, conv_w, wout_bf16,
                 final_g, n_seq, t_len, layer, final):
    d = x2d.shape[1]
    n_mem = cache_mk.shape[2]
    seqs = 16 // t_len
    tm = seqs * t_len
    hblk = lambda c: pl.BlockSpec((tm, GROUP_W), lambda i, c=c: (i, c))
    rows = pl.BlockSpec((tm, GROUP_W), lambda i: (i, 0))
    full = lambda shape: pl.BlockSpec(shape, lambda i: (0,) * len(shape))
    return pl.pallas_call(
        functools.partial(_merge_sample_kernel, t_len, seqs, layer, final),
        grid=(n_seq // seqs,),
        in_specs=[pl.BlockSpec((tm, d), lambda i: (i, 0)),
                  hblk(A_G), hblk(B_B), hblk(B_C), hblk(B_X), hblk(B_G), hblk(C_G),
                  hblk(M_Q), hblk(M_G),
                  pl.BlockSpec((seqs, None, CONV_K - 1, GROUP_W), lambda i: (i, layer, 0, 0)),
                  rows, rows,
                  pl.BlockSpec((seqs, None, n_mem, GROUP_W), lambda i: (i, layer, 0, 0)),
                  pl.BlockSpec((seqs, None, n_mem, GROUP_W), lambda i: (i, layer, 0, 0)),
                  full((4, GROUP_W)), full((CONV_K, GROUP_W)), full(wout_bf16.shape),
                  full((1, d))],
        out_specs=[pl.BlockSpec((tm, d), lambda i: (i, 0)), rows],
        out_shape=[jax.ShapeDtypeStruct(x2d.shape, F32),
                   jax.ShapeDtypeStruct((x2d.shape[0], GROUP_W), F32)],
        compiler_params=_cparams("parallel"),
        name="merge_sample",
    )(x2d, h2d, h2d, h2d, h2d, h2d, h2d, h2d, h2d, state_conv, y_a, y_c, cache_mk, cache_mv,
      gn, conv_w, wout_bf16, final_g.reshape(1, d))


def _time_major(a, n_seq, t_len):
    return a.reshape(n_seq, t_len, -1).transpose(1, 0, 2).reshape(n_seq * t_len, -1)


def _seq_major(a, n_seq, t_len):
    return a.reshape(t_len, n_seq, -1).transpose(1, 0, 2).reshape(n_seq * t_len, -1)


def _col(h2d, c):
    return h2d[:, c * GROUP_W:(c + 1) * GROUP_W]


def kernel(x_prompt, x_sample, cache_sb_k, cache_sb_v, state_ssm_re, state_ssm_im, state_conv,
           cache_mem_k, cache_mem_v, page_table, mem_prompt, norm_g, w_in, w_out, group_norm_g,
           ssm_lambda_re, ssm_lambda_im, ssm_b_re, ssm_b_im, ssm_c_re, ssm_c_im, ssm_log_dt, ssm_d,
           ssm_w_glu, conv_w, sb_bias, w_mem_kv, final_norm_g):
    n_p, l_p, d = x_prompt.shape
    n_s, l_s, _ = x_sample.shape
    depth = w_in.shape[0]
    n_mem = mem_prompt.shape[1]
    n_phys, _, page = cache_sb_k.shape[:3]
    assert l_s >= CONV_K - 1 and 16 % l_s == 0 and n_p % 8 == 0 and n_s % 8 == 0

    xp = x_prompt.reshape(n_p * l_p, d)
    xs = x_sample.reshape(n_s * l_s, d)
    mem2d = mem_prompt.reshape(n_p * n_mem, d)
    cache_k = cache_sb_k.reshape(n_phys, depth, page, GROUP_W)
    cache_v = cache_sb_v.reshape(n_phys, depth, page, GROUP_W)
    cache_mk = cache_mem_k.reshape(n_s, depth, n_mem, GROUP_W)
    cache_mv = cache_mem_v.reshape(n_s, depth, n_mem, GROUP_W)
    zeros_h = jnp.zeros((n_p, SSM_W), F32)
    zeros_buf = jnp.zeros((n_p, CONV_K - 1, GROUP_W), F32)

    p_k, p_v, p_re, p_im, p_conv, p_mk, p_mv = [], [], [], [], [], [], []
    s_k, s_v, s_re, s_im, s_conv = [], [], [], [], []
    for i in range(depth):
        final = i == depth - 1
        w_in_bf = w_in[i].astype(BF16)
        w_out_bf = w_out[i].astype(BF16)
        w_mem_bf = w_mem_kv[i].astype(BF16)

        lbre, lbim, bbre, bbim = ssm_discretise(ssm_lambda_re[i], ssm_lambda_im[i], ssm_log_dt[i],
                                                ssm_b_re[i], ssm_b_im[i])
        bre_hi, bre_lo = _hi_lo(_block_diag_in(bbre))
        bim_hi, bim_lo = _hi_lo(_block_diag_in(bbim))
        sp = {"lbre": lbre, "lbim": lbim,
              "bre_hi": bre_hi, "bre_lo": bre_lo, "bim_hi": bim_hi, "bim_lo": bim_lo,
              "cre": _block_diag_out(ssm_c_re[i]).astype(BF16),
              "cimn": _block_diag_out(-ssm_c_im[i]).astype(BF16),
              "d": ssm_d[i].reshape(1, GROUP_W), "wglu": ssm_w_glu[i].astype(BF16)}

        mkv = matmul(mem2d, w_mem_bf)
        mk, mv = mkv[:, :GROUP_W], mkv[:, GROUP_W:]
        h_p = in_proj(xp, norm_g[i], w_in_bf)
        ya_tm, h_re, h_im = ssm_scan(_time_major(_col(h_p, A_U), n_p, l_p), zeros_h, zeros_h,
                                     sp, n_p, l_p)
        y_a = _seq_major(ya_tm, n_p, l_p)
        y_c = sb_prompt(h_p, sb_bias[i], n_p, l_p)
        xp, conv_p = merge_prompt(xp, h_p, y_a, y_c, zeros_buf, mk, mv, group_norm_g[i], conv_w[i],
                                  w_out_bf, final_norm_g, n_p, l_p, final)
        p_k.append(_col(h_p, C_K).reshape(n_p, l_p, N_HEADS, HEAD_DIM))
        p_v.append(_col(h_p, C_V).reshape(n_p, l_p, N_HEADS, HEAD_DIM))
        p_re.append(h_re.reshape(n_p, SSM_GROUPS, SSM_STATE))
        p_im.append(h_im.reshape(n_p, SSM_GROUPS, SSM_STATE))
        p_conv.append(conv_p)
        p_mk.append(mk.reshape(n_p, n_mem, N_HEADS, HEAD_DIM))
        p_mv.append(mv.reshape(n_p, n_mem, N_HEADS, HEAD_DIM))

        h_s = in_proj(xs, norm_g[i], w_in_bf)
        ya_tm, h_re, h_im = ssm_scan(_time_major(_col(h_s, A_U), n_s, l_s),
                                     state_ssm_re[:, i].reshape(n_s, SSM_W),
                                     state_ssm_im[:, i].reshape(n_s, SSM_W), sp, n_s, l_s)
        y_a = _seq_major(ya_tm, n_s, l_s)
        seq3 = lambda c: _col(h_s, c).reshape(n_s, l_s, GROUP_W)
        y_c = sb_sample(seq3(C_Q), seq3(C_K), seq3(C_V), sb_bias[i], cache_k, cache_v,
                        page_table, i).reshape(n_s * l_s, GROUP_W)
        xs, v_conv = merge_sample(xs, h_s, y_a, y_c, state_conv, cache_mk, cache_mv,
                                  group_norm_g[i], conv_w[i], w_out_bf, final_norm_g,
                                  n_s, l_s, i, final)
        s_k.append(seq3(C_K).reshape(n_s, l_s, N_HEADS, HEAD_DIM))
        s_v.append(seq3(C_V).reshape(n_s, l_s, N_HEADS, HEAD_DIM))
        s_re.append(h_re.reshape(n_s, SSM_GROUPS, SSM_STATE))
        s_im.append(h_im.reshape(n_s, SSM_GROUPS, SSM_STATE))
        s_conv.append(v_conv.reshape(n_s, l_s, GROUP_W)[:, l_s - (CONV_K - 1):, :])

    stack = lambda xs_: jnp.stack(xs_, axis=1)
    return (xp.reshape(n_p, l_p, d), xs.reshape(n_s, l_s, d),
            stack(p_k), stack(p_v), stack(p_re), stack(p_im), stack(p_conv),
            stack(p_mk), stack(p_mv),
            stack(s_k), stack(s_v), stack(s_re), stack(s_im), stack(s_conv))
```

```python
import functools

import jax
import jax.numpy as jnp
from jax import lax
from jax.experimental import pallas as pl
from jax.experimental.pallas import tpu as pltpu

F32 = jnp.float32
BF16 = jnp.bfloat16

EPS = 1e-6
GROUP_W = 256
N_IN_BLOCKS = 12
HEAD_DIM = 64
N_HEADS = 4
ATT_SCALE = HEAD_DIM ** -0.5
SSM_GROUPS = 16
SSM_CH = 16
SSM_STATE = 64
SSM_W = SSM_GROUPS * SSM_STATE
CONV_K = 3

(A_U, A_G, B_B, B_C, B_X, B_G, C_Q, C_K, C_V, C_G, M_Q, M_G) = range(N_IN_BLOCKS)
PROMPT_BLOCKS = (A_U, A_G, B_B, B_C, B_X, B_G, C_G, M_Q, M_G)
PROMPT_COL = {b: j for j, b in enumerate(PROMPT_BLOCKS)}
SAMPLE_COL = {b: b for b in range(N_IN_BLOCKS)}

VMEM_LIMIT = 48 * 1024 * 1024
PAGES_PER_STEP = 8


def _cparams(*sem):
    return pltpu.CompilerParams(dimension_semantics=sem, vmem_limit_bytes=VMEM_LIMIT)


def _dot(a, b):
    return jnp.dot(a, b, preferred_element_type=F32)


def _dot_nt(a, b):
    return lax.dot_general(a, b, (((1,), (1,)), ((), ())), preferred_element_type=F32)


def _rms(x, g):
    return x * lax.rsqrt(jnp.mean(x * x, axis=-1, keepdims=True) + EPS) * g


def _silu(x):
    return x * jax.nn.sigmoid(x)


def _head(h):
    return slice(h * HEAD_DIM, (h + 1) * HEAD_DIM)


def _inproj_kernel(x_ref, g_ref, w_ref, o_ref):
    xn = _rms(x_ref[...], g_ref[...])
    o_ref[...] = _dot(xn.astype(BF16), w_ref[...])


def in_proj(x2d, g, w_bf16):
    rows, d = x2d.shape
    c = w_bf16.shape[1]
    tm = min(rows, 256)
    return pl.pallas_call(
        _inproj_kernel,
        grid=(rows // tm,),
        in_specs=[pl.BlockSpec((tm, d), lambda i: (i, 0)),
                  pl.BlockSpec((1, d), lambda i: (0, 0)),
                  pl.BlockSpec((d, c), lambda i: (0, 0))],
        out_specs=pl.BlockSpec((tm, c), lambda i: (i, 0)),
        out_shape=jax.ShapeDtypeStruct((rows, c), F32),
        compiler_params=_cparams("parallel"),
        name="in_proj",
    )(x2d, g.reshape(1, d), w_bf16)


def _inproj_prompt_kernel(x_ref, g_ref, w_ref, wt_ref, h_ref, krm_ref, qt_ref, vtb_ref,
                          kt_ref, vt_ref):
    n_main = len(PROMPT_BLOCKS) * GROUP_W
    xn = _rms(x_ref[...], g_ref[...]).astype(BF16)
    hm = _dot(xn, w_ref[...])
    h_ref[...] = hm[:, :n_main]
    krm_ref[...] = hm[:, n_main:].astype(BF16)
    t = _dot_nt(wt_ref[...], xn)
    qt_ref[...] = (t[0:GROUP_W] * ATT_SCALE).astype(BF16)
    kt_ref[...] = t[GROUP_W:2 * GROUP_W]
    v_t = t[2 * GROUP_W:3 * GROUP_W]
    vt_ref[...] = v_t
    vtb_ref[...] = v_t.astype(BF16)


def in_proj_prompt(x2d, g, w_in, n_seq, t_len, tm):
    rows, d = x2d.shape
    nb = t_len // tm
    blk = lambda b: w_in[:, b * GROUP_W:(b + 1) * GROUP_W]
    w_main = jnp.concatenate([blk(b) for b in PROMPT_BLOCKS + (C_K,)], axis=1).astype(BF16)
    w_t = jnp.concatenate([blk(C_Q), blk(C_K), blk(C_V)], axis=1).T.astype(BF16)
    n_main = len(PROMPT_BLOCKS) * GROUP_W
    full = lambda shape: pl.BlockSpec(shape, lambda n, i: (0,) * len(shape))
    blocked = pl.BlockSpec((None, None, GROUP_W, tm), lambda n, i: (n, i, 0, 0))
    final_t = pl.BlockSpec((None, GROUP_W, tm), lambda n, i: (n, 0, i))
    return pl.pallas_call(
        _inproj_prompt_kernel,
        grid=(n_seq, nb),
        in_specs=[pl.BlockSpec((tm, d), lambda n, i: (n * nb + i, 0)),
                  full((1, d)), full(w_main.shape), full(w_t.shape)],
        out_specs=[pl.BlockSpec((tm, n_main), lambda n, i: (n * nb + i, 0)),
                   pl.BlockSpec((tm, GROUP_W), lambda n, i: (n * nb + i, 0)),
                   blocked, blocked, final_t, final_t],
        out_shape=[jax.ShapeDtypeStruct((rows, n_main), F32),
                   jax.ShapeDtypeStruct((rows, GROUP_W), BF16),
                   jax.ShapeDtypeStruct((n_seq, nb, GROUP_W, tm), BF16),
                   jax.ShapeDtypeStruct((n_seq, nb, GROUP_W, tm), BF16),
                   jax.ShapeDtypeStruct((n_seq, GROUP_W, t_len), F32),
                   jax.ShapeDtypeStruct((n_seq, GROUP_W, t_len), F32)],
        compiler_params=_cparams("parallel", "parallel"),
        name="in_proj_prompt",
    )(x2d, g.reshape(1, d), w_main, w_t)


def _memkv_kernel(x_ref, wt_ref, o_ref):
    o_ref[...] = _dot_nt(wt_ref[...], x_ref[...].astype(BF16))


def mem_kv_t(mem, w_mem):
    n_seq, n_mem, d = mem.shape
    w_t = w_mem.T.astype(BF16)
    return pl.pallas_call(
        _memkv_kernel,
        grid=(n_seq,),
        in_specs=[pl.BlockSpec((None, n_mem, d), lambda n: (n, 0, 0)),
                  pl.BlockSpec(w_t.shape, lambda n: (0, 0))],
        out_specs=pl.BlockSpec((None, w_t.shape[0], n_mem), lambda n: (n, 0, 0)),
        out_shape=jax.ShapeDtypeStruct((n_seq, w_t.shape[0], n_mem), F32),
        compiler_params=_cparams("parallel"),
        name="mem_kv",
    )(mem, w_t)


def _ssm_disc_kernel(lre_ref, lim_ref, dt_ref, bre_ref, bim_ref,
                     lbre_ref, lbim_ref, bbre_ref, bbim_ref):
    lre = lre_ref[...]
    lim = lim_ref[...]
    dt = jnp.exp(dt_ref[...])
    mag = jnp.exp(lre * dt)
    lbre = mag * jnp.cos(lim * dt)
    lbim = mag * jnp.sin(lim * dt)
    lbre_ref[...] = lbre
    lbim_ref[...] = lbim
    nre = lbre - 1.0
    nim = lbim
    den = lre * lre + lim * lim
    cre = (nre * lre + nim * lim) / den
    cim = (nim * lre - nre * lim) / den
    bre = bre_ref[...]
    bim = bim_ref[...]
    bbre_ref[...] = cre * bre - cim * bim
    bbim_ref[...] = cre * bim + cim * bre


def ssm_discretise(lam_re, lam_im, log_dt, b_re, b_im):
    col = lambda a: a.reshape(SSM_W, 1)
    dt_col = jnp.broadcast_to(log_dt[:, None], (SSM_GROUPS, SSM_STATE)).reshape(SSM_W, 1)
    outs = pl.pallas_call(
        _ssm_disc_kernel,
        out_shape=[jax.ShapeDtypeStruct((SSM_W, 1), F32)] * 2
        + [jax.ShapeDtypeStruct((SSM_W, SSM_CH), F32)] * 2,
        name="ssm_discretise",
    )(col(lam_re), col(lam_im), dt_col, b_re.reshape(SSM_W, SSM_CH), b_im.reshape(SSM_W, SSM_CH))
    lbre, lbim, bbre, bbim = outs
    shp = (SSM_GROUPS, SSM_STATE, SSM_CH)
    return lbre.reshape(1, SSM_W), lbim.reshape(1, SSM_W), bbre.reshape(shp), bbim.reshape(shp)


def _block_diag_in(b_gpc):
    eye = jnp.eye(SSM_GROUPS, dtype=F32)
    m = b_gpc.transpose(0, 2, 1)[:, :, None, :] * eye[:, None, :, None]
    return m.reshape(SSM_GROUPS * SSM_CH, SSM_W)


def _block_diag_out(c_gcp):
    eye = jnp.eye(SSM_GROUPS, dtype=F32)
    m = c_gcp.transpose(0, 2, 1)[:, :, None, :] * eye[:, None, :, None]
    return m.reshape(SSM_W, SSM_GROUPS * SSM_CH)


def _hi_lo(a):
    hi = a.astype(BF16)
    lo = (a - hi.astype(F32)).astype(BF16)
    return hi, lo


def _dot3(a, b_hi, b_lo):
    a_hi, a_lo = _hi_lo(a)
    return _dot(a_hi, b_hi) + (_dot(a_lo, b_hi) + _dot(a_hi, b_lo))


def _ssm_kernel(n_seq, t_chunk,
                u_ref, h0re_ref, h0im_ref, lre_ref, lim_ref,
                bre_hi_ref, bre_lo_ref, bim_hi_ref, bim_lo_ref,
                cre_ref, cimn_ref, d_ref, wglu_ref,
                y_ref, hre_ref, him_ref,
                bu_re, bu_im, hs_re, hs_im):
    step = pl.program_id(0)

    @pl.when(step == 0)
    def _():
        hre_ref[...] = h0re_ref[...]
        him_ref[...] = h0im_ref[...]

    u = u_ref[...]
    bu_re[...] = _dot3(u, bre_hi_ref[...], bre_lo_ref[...])
    bu_im[...] = _dot3(u, bim_hi_ref[...], bim_lo_ref[...])

    lre = jnp.broadcast_to(lre_ref[...], (n_seq, SSM_W))
    lim = jnp.broadcast_to(lim_ref[...], (n_seq, SSM_W))

    def body(t, carry):
        hr, hi = carry
        rows = pl.ds(pl.multiple_of(t * n_seq, n_seq), n_seq)
        nr = lre * hr - lim * hi + bu_re[rows, :]
        ni = lre * hi + lim * hr + bu_im[rows, :]
        hs_re[rows, :] = nr
        hs_im[rows, :] = ni
        return nr, ni

    hr, hi = lax.fori_loop(0, t_chunk, body, (hre_ref[...], him_ref[...]))
    hre_ref[...] = hr
    him_ref[...] = hi

    y = (_dot(hs_re[...].astype(BF16), cre_ref[...])
         + _dot(hs_im[...].astype(BF16), cimn_ref[...])
         + d_ref[...] * u)
    y = jax.nn.gelu(y)
    y_ref[...] = y * jax.nn.sigmoid(_dot(y.astype(BF16), wglu_ref[...]))


def ssm_scan(u_tm, h0_re, h0_im, sp, n_seq, t_len):
    rows = u_tm.shape[0]
    t_chunk = min(t_len, 64)
    tr = t_chunk * n_seq
    full = lambda shape: pl.BlockSpec(shape, lambda i: (0,) * len(shape))
    return pl.pallas_call(
        functools.partial(_ssm_kernel, n_seq, t_chunk),
        grid=(t_len // t_chunk,),
        in_specs=[pl.BlockSpec((tr, GROUP_W), lambda i: (i, 0)),
                  full((n_seq, SSM_W)), full((n_seq, SSM_W)),
                  full((1, SSM_W)), full((1, SSM_W)),
                  full((GROUP_W, SSM_W)), full((GROUP_W, SSM_W)),
                  full((GROUP_W, SSM_W)), full((GROUP_W, SSM_W)),
                  full((SSM_W, GROUP_W)), full((SSM_W, GROUP_W)),
                  full((1, GROUP_W)), full((GROUP_W, GROUP_W))],
        out_specs=[pl.BlockSpec((tr, GROUP_W), lambda i: (i, 0)),
                   full((n_seq, SSM_W)), full((n_seq, SSM_W))],
        out_shape=[jax.ShapeDtypeStruct((rows, GROUP_W), F32),
                   jax.ShapeDtypeStruct((n_seq, SSM_W), F32),
                   jax.ShapeDtypeStruct((n_seq, SSM_W), F32)],
        scratch_shapes=[pltpu.VMEM((tr, SSM_W), F32)] * 4,
        compiler_params=_cparams("arbitrary"),
        name="ssm_scan",
    )(u_tm, h0_re, h0_im, sp["lbre"], sp["lbim"],
      sp["bre_hi"], sp["bre_lo"], sp["bim_hi"], sp["bim_lo"],
      sp["cre"], sp["cimn"], sp["d"], sp["wglu"])


def _softplus(z):
    return jnp.maximum(z, 0.0) + jnp.log(1.0 + jnp.exp(-jnp.abs(z)))


def _sb_prompt_kernel(tq, bias_ref, qt_ref, k_ref, vt_ref, o_ref, acc_ref):
    i = pl.program_id(1)
    s_idx = lax.broadcasted_iota(jnp.int32, (tq, tq), 0)
    t_idx = lax.broadcasted_iota(jnp.int32, (tq, tq), 1)
    neg_suffix = jnp.where(t_idx > s_idx, -1.0, 0.0).astype(BF16)
    diag_mask = s_idx < t_idx
    q_t = [qt_ref[_head(h), :] for h in range(N_HEADS)]

    def block(kb, carries, mask):
        rows = pl.ds(pl.multiple_of(kb * tq, tq), tq)
        outs, new_carries = [], []
        for h in range(N_HEADS):
            z = _dot(k_ref[rows, _head(h)], q_t[h]) + bias_ref[h]
            sp = _softplus(z)
            sp_m = sp if mask is None else jnp.where(mask, sp, 0.0)
            sp_bf = sp_m.astype(BF16)
            local = _dot(neg_suffix, sp_bf)
            w = jnp.exp((z - sp) + (local + carries[h]))
            if mask is not None:
                w = jnp.where(mask, w, 0.0)
            outs.append(_dot(vt_ref[kb, _head(h), :], w.astype(BF16)))
            new_carries.append(carries[h] + (local[0:1, :] - sp_bf[0:1, :].astype(F32)))
        return outs, tuple(new_carries)

    zero = jnp.zeros((1, tq), F32)
    outs, carries = block(i, (zero,) * N_HEADS, diag_mask)
    for h in range(N_HEADS):
        acc_ref[_head(h), :] = outs[h]

    def body(j, carries):
        outs, carries = block(i - 1 - j, carries, None)
        for h in range(N_HEADS):
            acc_ref[_head(h), :] += outs[h]
        return carries

    lax.fori_loop(0, i, body, carries)
    o_ref[...] = acc_ref[...].T


def sb_prompt(q_t, k_rm, v_t, bias, n_seq, t_len, tq):
    nb = t_len // tq
    return pl.pallas_call(
        functools.partial(_sb_prompt_kernel, tq),
        grid=(n_seq, nb),
        in_specs=[pl.BlockSpec(memory_space=pltpu.SMEM),
                  pl.BlockSpec((None, None, GROUP_W, tq), lambda n, i: (n, i, 0, 0)),
                  pl.BlockSpec((t_len, GROUP_W), lambda n, i: (n, 0)),
                  pl.BlockSpec((None, nb, GROUP_W, tq), lambda n, i: (n, 0, 0, 0))],
        out_specs=pl.BlockSpec((tq, GROUP_W), lambda n, i: (n * nb + i, 0)),
        out_shape=jax.ShapeDtypeStruct((n_seq * t_len, GROUP_W), F32),
        scratch_shapes=[pltpu.VMEM((GROUP_W, tq), F32)],
        compiler_params=_cparams("parallel", "arbitrary"),
        name="sb_prompt",
    )(bias, q_t, k_rm, v_t)


def _sb_rows_block(z, mask, carry, neg_suffix):
    sp = _softplus(z)
    sp_m = sp if mask is None else jnp.where(mask, sp, 0.0)
    sp_bf = sp_m.astype(BF16)
    local = _dot(sp_bf, neg_suffix)
    w = jnp.exp((z - sp) + (local + carry))
    if mask is not None:
        w = jnp.where(mask, w, 0.0)
    return w, local[:, 0:1] - sp_bf[:, 0:1].astype(F32)


def _sb_sample_kernel(t_new, page, n_pages, layer,
                      pt_ref, bias_ref, qrep_ref, knew_ref, vnew_ref, ck_hbm, cv_hbm,
                      o_ref, kbuf, vbuf, sem, qbd_ref, acc_ref, carry_ref):
    n = pl.program_id(0)
    j = pl.program_id(1)
    n_steps = pl.num_programs(1)
    step = n * n_steps + j
    slot = step % 2
    rows_q = N_HEADS * t_new
    g_pages = PAGES_PER_STEP

    def page_copies(seq, jj, slot_):
        copies = []
        for g in range(g_pages):
            phys = pt_ref[seq, n_pages - 1 - (jj * g_pages + g)]
            copies.append(pltpu.make_async_copy(ck_hbm.at[phys, layer], kbuf.at[slot_, g],
                                                sem.at[0, slot_]))
            copies.append(pltpu.make_async_copy(cv_hbm.at[phys, layer], vbuf.at[slot_, g],
                                                sem.at[1, slot_]))
        return copies

    @pl.when(step == 0)
    def _():
        for c in page_copies(n, j, slot):
            c.start()

    nxt = step + 1

    @pl.when(nxt < pl.num_programs(0) * n_steps)
    def _():
        for c in page_copies(nxt // n_steps, nxt % n_steps, 1 - slot):
            c.start()

    jk = lax.broadcasted_iota(jnp.int32, (page, page), 0)
    sk = lax.broadcasted_iota(jnp.int32, (page, page), 1)
    neg_suffix = jnp.where(jk > sk, -1.0, 0.0).astype(BF16)
    row_head = lax.broadcasted_iota(jnp.int32, (rows_q, 1), 0) // t_new
    bias_col = jnp.zeros((rows_q, 1), F32)
    for h in range(N_HEADS):
        bias_col = jnp.where(row_head == h, bias_ref[h], bias_col)

    @pl.when(j == 0)
    def _():
        lane_head = lax.broadcasted_iota(jnp.int32, (rows_q, GROUP_W), 1) // HEAD_DIM
        qbd = jnp.where(lane_head == row_head, qrep_ref[...] * ATT_SCALE, 0.0).astype(BF16)
        qbd_ref[...] = qbd
        t_of_row = lax.broadcasted_iota(jnp.int32, (rows_q, page), 0) % t_new
        key = lax.broadcasted_iota(jnp.int32, (rows_q, page), 1)
        z = _dot_nt(qbd, knew_ref[...].astype(BF16)) + bias_col
        w, total = _sb_rows_block(z, key < t_of_row, jnp.zeros((rows_q, 1), F32), neg_suffix)
        acc_ref[...] = _dot(w.astype(BF16), vnew_ref[...].astype(BF16))
        carry_ref[...] = total

    for c in page_copies(n, j, slot):
        c.wait()

    qbd = qbd_ref[...]
    z = jnp.concatenate(
        [_dot(qbd, kbuf[slot, g].astype(BF16)) for g in range(g_pages)], axis=0) + jnp.concatenate(
        [bias_col] * g_pages, axis=0)
    sp = _softplus(z)
    sp_bf = sp.astype(BF16)
    local = _dot(sp_bf, neg_suffix)
    totals = local[:, 0:1] - sp_bf[:, 0:1].astype(F32)
    carry = carry_ref[...]
    carries = []
    for g in range(g_pages):
        carries.append(carry)
        carry = carry + totals[g * rows_q:(g + 1) * rows_q, :]
    carry_ref[...] = carry
    w = jnp.exp((z - sp) + (local + jnp.concatenate(carries, axis=0))).astype(BF16)
    acc = acc_ref[...]
    for g in range(g_pages):
        acc = acc + _dot_nt(w[g * rows_q:(g + 1) * rows_q, :], vbuf[slot, g].astype(BF16))
    acc_ref[...] = acc

    @pl.when(j == n_steps - 1)
    def _():
        o_ref[...] = acc


def sb_sample(q, k_new, v_new, bias, cache_kt, cache_vt, page_table, layer):
    n_seq, t_new, _ = q.shape
    page = cache_kt.shape[3]
    n_pages = page_table.shape[1]
    assert n_pages % PAGES_PER_STEP == 0
    rows_q = N_HEADS * t_new
    q_rep = jnp.tile(q, (1, N_HEADS, 1))
    pad = ((0, 0), (0, page - t_new), (0, 0))
    k_pad = jnp.pad(k_new, pad)
    v_pad = jnp.pad(v_new, pad)
    per_seq = lambda r: pl.BlockSpec((None, r, GROUP_W), lambda n, j, pt: (n, 0, 0))
    acc = pl.pallas_call(
        functools.partial(_sb_sample_kernel, t_new, page, n_pages, layer),
        grid_spec=pltpu.PrefetchScalarGridSpec(
            num_scalar_prefetch=1,
            grid=(n_seq, n_pages // PAGES_PER_STEP),
            in_specs=[pl.BlockSpec(memory_space=pltpu.SMEM),
                      per_seq(rows_q), per_seq(page), per_seq(page),
                      pl.BlockSpec(memory_space=pl.ANY),
                      pl.BlockSpec(memory_space=pl.ANY)],
            out_specs=per_seq(rows_q),
            scratch_shapes=[pltpu.VMEM((2, PAGES_PER_STEP, GROUP_W, page), F32),
                            pltpu.VMEM((2, PAGES_PER_STEP, GROUP_W, page), F32),
                            pltpu.SemaphoreType.DMA((2, 2)),
                            pltpu.VMEM((rows_q, GROUP_W), BF16),
                            pltpu.VMEM((rows_q, GROUP_W), F32),
                            pltpu.VMEM((rows_q, 1), F32)]),
        out_shape=jax.ShapeDtypeStruct((n_seq, rows_q, GROUP_W), F32),
        compiler_params=_cparams("arbitrary", "arbitrary"),
        name="sb_sample",
    )(page_table, bias, q_rep, k_pad, v_pad, cache_kt, cache_vt)
    acc = acc.reshape(n_seq, N_HEADS, t_new, N_HEADS, HEAD_DIM)
    heads = [acc[:, h, :, h, :] for h in range(N_HEADS)]
    return jnp.stack(heads, axis=2).reshape(n_seq, t_new, GROUP_W)


def _mem_attend_head(q_h, mkt_h, mvt_h):
    s = _dot(q_h, mkt_h)
    e = jnp.exp(s - jnp.max(s, axis=-1, keepdims=True))
    return _dot_nt(e.astype(BF16), mvt_h) / jnp.sum(e, axis=-1, keepdims=True)


def _merge_tail(x, y_a, y_b, y_c, y_m, g_a, g_b, g_c, g_m, gn_ref, wout_ref):
    merged = jnp.concatenate([
        _rms(y_a, gn_ref[0:1, :]) * _silu(g_a),
        _rms(y_b, gn_ref[1:2, :]) * _silu(g_b),
        _rms(y_c, gn_ref[2:3, :]) * _silu(g_c),
        _rms(y_m, gn_ref[3:4, :]) * _silu(g_m)], axis=-1)
    return x + _dot(merged.astype(BF16), wout_ref[...])


def _conv_taps(v, vm1, vm2, b_gate, cw_ref):
    return b_gate * (vm2 * cw_ref[0:1, :] + vm1 * cw_ref[1:2, :] + v * cw_ref[2:3, :])


def _merge_prompt_kernel(tm, final,
                         x_ref, ag_ref, bb_ref, bc_ref, bx_ref, bg_ref, cg_ref, mq_ref, mg_ref,
                         hbc_ref, hbx_ref, buf_ref, ya_ref, yc_ref, mkt_ref, mvt_ref,
                         gn_ref, cw_ref, wout_ref, fg_ref,
                         o_ref, tail_ref):
    i = pl.program_id(1)
    v = bc_ref[...] * bx_ref[...]
    halo = hbc_ref[...] * hbx_ref[...]
    first = i == 0
    prev1 = jnp.where(first, buf_ref[1:2, :], halo[7:8, :])
    prev2 = jnp.where(first, buf_ref[0:1, :], halo[6:7, :])
    row = lax.broadcasted_iota(jnp.int32, (tm, 1), 0)
    vm1 = jnp.where(row == 0, prev1, pltpu.roll(v, 1, 0))
    vm2 = jnp.where(row == 0, prev2, jnp.where(row == 1, prev1, pltpu.roll(v, 2, 0)))
    y_b = _conv_taps(v, vm1, vm2, bb_ref[...], cw_ref)
    tail_ref[...] = v[tm - 8:tm, :]

    heads = []
    for h in range(N_HEADS):
        q_h = (mq_ref[:, _head(h)] * ATT_SCALE).astype(BF16)
        heads.append(_mem_attend_head(q_h, mkt_ref[_head(h), :].astype(BF16),
                                      mvt_ref[_head(h), :].astype(BF16)))
    y_m = jnp.concatenate(heads, axis=-1)

    x_new = _merge_tail(x_ref[...], ya_ref[...], y_b, yc_ref[...], y_m,
                        ag_ref[...], bg_ref[...], cg_ref[...], mg_ref[...], gn_ref, wout_ref)
    o_ref[...] = _rms(x_new, fg_ref[...]) if final else x_new


def merge_prompt(x2d, h2d, y_a, y_c, conv_buf, mkv_t, gn, conv_w, wout_bf16, final_g,
                 n_seq, t_len, final):
    d = x2d.shape[1]
    n_mem = mkv_t.shape[2]
    tm = min(t_len, 256)
    nb = t_len // tm
    col = PROMPT_COL
    hblk = lambda c: pl.BlockSpec((tm, GROUP_W), lambda n, i, c=c: (n * nb + i, col[c]))
    halo = lambda c: pl.BlockSpec(
        (8, GROUP_W), lambda n, i, c=c: (jnp.maximum((n * nb + i) * (tm // 8) - 1, 0), col[c]))
    rows = pl.BlockSpec((tm, GROUP_W), lambda n, i: (n * nb + i, 0))
    full = lambda shape: pl.BlockSpec(shape, lambda n, i: (0,) * len(shape))
    out, tail = pl.pallas_call(
        functools.partial(_merge_prompt_kernel, tm, final),
        grid=(n_seq, nb),
        in_specs=[pl.BlockSpec((tm, d), lambda n, i: (n * nb + i, 0)),
                  hblk(A_G), hblk(B_B), hblk(B_C), hblk(B_X), hblk(B_G), hblk(C_G),
                  hblk(M_Q), hblk(M_G), halo(B_C), halo(B_X),
                  pl.BlockSpec((None, CONV_K - 1, GROUP_W), lambda n, i: (n, 0, 0)),
                  rows, rows,
                  pl.BlockSpec((None, GROUP_W, n_mem), lambda n, i: (n, 0, 0)),
                  pl.BlockSpec((None, GROUP_W, n_mem), lambda n, i: (n, 1, 0)),
                  full((4, GROUP_W)), full((CONV_K, GROUP_W)), full(wout_bf16.shape),
                  full((1, d))],
        out_specs=[pl.BlockSpec((tm, d), lambda n, i: (n * nb + i, 0)),
                   pl.BlockSpec((None, 8, GROUP_W), lambda n, i: (n, 0, 0))],
        out_shape=[jax.ShapeDtypeStruct(x2d.shape, F32),
                   jax.ShapeDtypeStruct((n_seq, 8, GROUP_W), F32)],
        compiler_params=_cparams("parallel", "arbitrary"),
        name="merge_prompt",
    )(x2d, h2d, h2d, h2d, h2d, h2d, h2d, h2d, h2d, h2d, h2d, conv_buf, y_a, y_c, mkv_t, mkv_t,
      gn, conv_w, wout_bf16, final_g.reshape(1, d))
    return out, tail[:, 8 - (CONV_K - 1):, :]


def _merge_sample_kernel(t_len, seqs, final,
                         x_ref, ag_ref, bb_ref, bc_ref, bx_ref, bg_ref, cg_ref, mq_ref, mg_ref,
                         buf_ref, ya_ref, yc_ref, mkt_ref, mvt_ref,
                         gn_ref, cw_ref, wout_ref, fg_ref,
                         o_ref, v_ref):
    tm = seqs * t_len
    row = lax.broadcasted_iota(jnp.int32, (tm, 1), 0)
    seq_of_row = row // t_len
    t_of_row = row % t_len

    v = bc_ref[...] * bx_ref[...]
    v_ref[...] = v
    prev1 = jnp.zeros((tm, GROUP_W), F32)
    prev2 = jnp.zeros((tm, GROUP_W), F32)
    for s in range(seqs):
        prev1 = jnp.where(seq_of_row == s, buf_ref[s, 1:2, :], prev1)
        prev2 = jnp.where(seq_of_row == s, buf_ref[s, 0:1, :], prev2)
    vm1 = jnp.where(t_of_row == 0, prev1, pltpu.roll(v, 1, 0))
    vm2 = jnp.where(t_of_row == 0, prev2, jnp.where(t_of_row == 1, prev1, pltpu.roll(v, 2, 0)))
    y_b = _conv_taps(v, vm1, vm2, bb_ref[...], cw_ref)

    heads = []
    for h in range(N_HEADS):
        q_h = (mq_ref[:, _head(h)] * ATT_SCALE).astype(BF16)
        y_h = jnp.zeros((tm, HEAD_DIM), F32)
        for s in range(seqs):
            o = _mem_attend_head(q_h, mkt_ref[s, _head(h), :].astype(BF16),
                                 mvt_ref[s, _head(h), :].astype(BF16))
            y_h = jnp.where(seq_of_row == s, o, y_h)
        heads.append(y_h)
    y_m = jnp.concatenate(heads, axis=-1)

    x_new = _merge_tail(x_ref[...], ya_ref[...], y_b, yc_ref[...], y_m,
                        ag_ref[...], bg_ref[...], cg_ref[...], mg_ref[...], gn_ref, wout_ref)
    o_ref[...] = _rms(x_new, fg_ref[...]) if final else x_new


def merge_sample(x2d, h2d, y_a, y_c, state_conv, cache_mkt, cache_mvt, gn, conv_w, wout_bf16,
                 final_g, n_seq, t_len, layer, final):
    d = x2d.shape[1]
    n_mem = cache_mkt.shape[3]
    seqs = 16 // t_len
    tm = seqs * t_len
    hblk = lambda c: pl.BlockSpec((tm, GROUP_W), lambda i, c=c: (i, c))
    rows = pl.BlockSpec((tm, GROUP_W), lambda i: (i, 0))
    full = lambda shape: pl.BlockSpec(shape, lambda i: (0,) * len(shape))
    mem = pl.BlockSpec((seqs, None, GROUP_W, n_mem), lambda i: (i, layer, 0, 0))
    return pl.pallas_call(
        functools.partial(_merge_sample_kernel, t_len, seqs, final),
        grid=(n_seq // seqs,),
        in_specs=[pl.BlockSpec((tm, d), lambda i: (i, 0)),
                  hblk(A_G), hblk(B_B), hblk(B_C), hblk(B_X), hblk(B_G), hblk(C_G),
                  hblk(M_Q), hblk(M_G),
                  pl.BlockSpec((seqs, None, CONV_K - 1, GROUP_W), lambda i: (i, layer, 0, 0)),
                  rows, rows, mem, mem,
                  full((4, GROUP_W)), full((CONV_K, GROUP_W)), full(wout_bf16.shape),
                  full((1, d))],
        out_specs=[pl.BlockSpec((tm, d), lambda i: (i, 0)), rows],
        out_shape=[jax.ShapeDtypeStruct(x2d.shape, F32),
                   jax.ShapeDtypeStruct((x2d.shape[0], GROUP_W), F32)],
        compiler_params=_cparams("parallel"),
        name="merge_sample",
    )(x2d, h2d, h2d, h2d, h2d, h2d, h2d, h2d, h2d, state_conv, y_a, y_c, cache_mkt, cache_mvt,
      gn, conv_w, wout_bf16, final_g.reshape(1, d))


def _time_major(a, n_seq, t_len):
    return a.reshape(n_seq, t_len, -1).transpose(1, 0, 2).reshape(n_seq * t_len, -1)


def _seq_major(a, n_seq, t_len):
    return a.reshape(t_len, n_seq, -1).transpose(1, 0, 2).reshape(n_seq * t_len, -1)


def _col(h2d, c):
    return h2d[:, c * GROUP_W:(c + 1) * GROUP_W]


def _pos_minor(a):
    lead = a.shape[:-3]
    pos, heads, dim = a.shape[-3:]
    nd = len(lead)
    return a.transpose(*range(nd), nd + 1, nd + 2, nd).reshape(*lead, heads * dim, pos)


def _from_pos_minor(a_t):
    lead = a_t.shape[:-2]
    pos = a_t.shape[-1]
    nd = len(lead)
    a = a_t.reshape(*lead, N_HEADS, HEAD_DIM, pos)
    return a.transpose(*range(nd), nd + 2, nd, nd + 1)


def kernel(x_prompt, x_sample, cache_sb_k, cache_sb_v, state_ssm_re, state_ssm_im, state_conv,
           cache_mem_k, cache_mem_v, page_table, mem_prompt, norm_g, w_in, w_out, group_norm_g,
           ssm_lambda_re, ssm_lambda_im, ssm_b_re, ssm_b_im, ssm_c_re, ssm_c_im, ssm_log_dt, ssm_d,
           ssm_w_glu, conv_w, sb_bias, w_mem_kv, final_norm_g):
    n_p, l_p, d = x_prompt.shape
    n_s, l_s, _ = x_sample.shape
    depth = w_in.shape[0]
    assert l_s >= CONV_K - 1 and 16 % l_s == 0 and n_p % 8 == 0 and n_s % 8 == 0
    tq = min(l_p, 256)

    xp = x_prompt.reshape(n_p * l_p, d)
    xs = x_sample.reshape(n_s * l_s, d)
    cache_kt = _pos_minor(cache_sb_k)
    cache_vt = _pos_minor(cache_sb_v)
    cache_mkt = _pos_minor(cache_mem_k)
    cache_mvt = _pos_minor(cache_mem_v)
    zeros_h = jnp.zeros((n_p, SSM_W), F32)
    zeros_buf = jnp.zeros((n_p, CONV_K - 1, GROUP_W), F32)

    p_kt, p_vt, p_re, p_im, p_conv, p_mkvt = [], [], [], [], [], []
    s_k, s_v, s_re, s_im, s_conv = [], [], [], [], []
    for i in range(depth):
        final = i == depth - 1
        w_in_bf = w_in[i].astype(BF16)
        w_out_bf = w_out[i].astype(BF16)

        lbre, lbim, bbre, bbim = ssm_discretise(ssm_lambda_re[i], ssm_lambda_im[i], ssm_log_dt[i],
                                                ssm_b_re[i], ssm_b_im[i])
        bre_hi, bre_lo = _hi_lo(_block_diag_in(bbre))
        bim_hi, bim_lo = _hi_lo(_block_diag_in(bbim))
        sp = {"lbre": lbre, "lbim": lbim,
              "bre_hi": bre_hi, "bre_lo": bre_lo, "bim_hi": bim_hi, "bim_lo": bim_lo,
              "cre": _block_diag_out(ssm_c_re[i]).astype(BF16),
              "cimn": _block_diag_out(-ssm_c_im[i]).astype(BF16),
              "d": ssm_d[i].reshape(1, GROUP_W), "wglu": ssm_w_glu[i].astype(BF16)}

        mkv_t = mem_kv_t(mem_prompt, w_mem_kv[i])
        h_p, k_rm, q_t, v_tb, k_t, v_t = in_proj_prompt(xp, norm_g[i], w_in[i], n_p, l_p, tq)
        ya_tm, h_re, h_im = ssm_scan(_time_major(_col(h_p, PROMPT_COL[A_U]), n_p, l_p),
                                     zeros_h, zeros_h, sp, n_p, l_p)
        y_a = _seq_major(ya_tm, n_p, l_p)
        y_c = sb_prompt(q_t, k_rm, v_tb, sb_bias[i], n_p, l_p, tq)
        xp, conv_p = merge_prompt(xp, h_p, y_a, y_c, zeros_buf, mkv_t, group_norm_g[i], conv_w[i],
                                  w_out_bf, final_norm_g, n_p, l_p, final)
        p_kt.append(k_t)
        p_vt.append(v_t)
        p_re.append(h_re.reshape(n_p, SSM_GROUPS, SSM_STATE))
        p_im.append(h_im.reshape(n_p, SSM_GROUPS, SSM_STATE))
        p_conv.append(conv_p)
        p_mkvt.append(mkv_t)

        h_s = in_proj(xs, norm_g[i], w_in_bf)
        ya_tm, h_re, h_im = ssm_scan(_time_major(_col(h_s, A_U), n_s, l_s),
                                     state_ssm_re[:, i].reshape(n_s, SSM_W),
                                     state_ssm_im[:, i].reshape(n_s, SSM_W), sp, n_s, l_s)
        y_a = _seq_major(ya_tm, n_s, l_s)
        seq3 = lambda c: _col(h_s, c).reshape(n_s, l_s, GROUP_W)
        y_c = sb_sample(seq3(C_Q), seq3(C_K), seq3(C_V), sb_bias[i], cache_kt, cache_vt,
                        page_table, i).reshape(n_s * l_s, GROUP_W)
        xs, v_conv = merge_sample(xs, h_s, y_a, y_c, state_conv, cache_mkt, cache_mvt,
                                  group_norm_g[i], conv_w[i], w_out_bf, final_norm_g,
                                  n_s, l_s, i, final)
        s_k.append(seq3(C_K).reshape(n_s, l_s, N_HEADS, HEAD_DIM))
        s_v.append(seq3(C_V).reshape(n_s, l_s, N_HEADS, HEAD_DIM))
        s_re.append(h_re.reshape(n_s, SSM_GROUPS, SSM_STATE))
        s_im.append(h_im.reshape(n_s, SSM_GROUPS, SSM_STATE))
        s_conv.append(v_conv.reshape(n_s, l_s, GROUP_W)[:, l_s - (CONV_K - 1):, :])

    stack = lambda xs_: jnp.stack(xs_, axis=1)
    mkv = stack(p_mkvt)
    return (xp.reshape(n_p, l_p, d), xs.reshape(n_s, l_s, d),
            _from_pos_minor(stack(p_kt)), _from_pos_minor(stack(p_vt)),
            stack(p_re), stack(p_im), stack(p_conv),
            _from_pos_minor(mkv[:, :, :GROUP_W]), _from_pos_minor(mkv[:, :, GROUP_W:]),
            stack(s_k), stack(s_v), stack(s_re), stack(s_im), stack(s_conv))
```

```python
import functools

import jax
import jax.numpy as jnp
from jax import lax
from jax.experimental import pallas as pl
from jax.experimental.pallas import tpu as pltpu

F32 = jnp.float32
BF16 = jnp.bfloat16

EPS = 1e-6
GROUP_W = 256
N_IN_BLOCKS = 12
HEAD_DIM = 64
N_HEADS = 4
ATT_SCALE = HEAD_DIM ** -0.5
LOG2E = 1.4426950408889634
SSM_GROUPS = 16
SSM_CH = 16
SSM_STATE = 64
SSM_W = SSM_GROUPS * SSM_STATE
CONV_K = 3
LANES = 128

(A_U, A_G, B_B, B_C, B_X, B_G, C_Q, C_K, C_V, C_G, M_Q, M_G) = range(N_IN_BLOCKS)
PROMPT_BLOCKS = (A_U, A_G, B_B, B_C, B_X, B_G, C_G, M_Q, M_G)
PROMPT_COL = {b: j for j, b in enumerate(PROMPT_BLOCKS)}
SAMPLE_COL = {b: b for b in range(N_IN_BLOCKS)}

VMEM_LIMIT = 48 * 1024 * 1024
PAGES_PER_STEP = 16


def _cparams(*sem):
    return pltpu.CompilerParams(dimension_semantics=sem, vmem_limit_bytes=VMEM_LIMIT)


def _dot(a, b):
    return jnp.dot(a, b, preferred_element_type=F32)


def _dot_nt(a, b):
    return lax.dot_general(a, b, (((1,), (1,)), ((), ())), preferred_element_type=F32)


def _rms(x, g):
    return x * lax.rsqrt(jnp.mean(x * x, axis=-1, keepdims=True) + EPS) * g


def _silu(x):
    return x * jax.nn.sigmoid(x)


def _head(h):
    return slice(h * HEAD_DIM, (h + 1) * HEAD_DIM)


def _inproj_kernel(x_ref, g_ref, w_ref, o_ref):
    xn = _rms(x_ref[...], g_ref[...])
    o_ref[...] = _dot(xn.astype(BF16), w_ref[...])


def in_proj(x2d, g, w_bf16):
    rows, d = x2d.shape
    c = w_bf16.shape[1]
    tm = min(rows, 256)
    return pl.pallas_call(
        _inproj_kernel,
        grid=(rows // tm,),
        in_specs=[pl.BlockSpec((tm, d), lambda i: (i, 0)),
                  pl.BlockSpec((1, d), lambda i: (0, 0)),
                  pl.BlockSpec((d, c), lambda i: (0, 0))],
        out_specs=pl.BlockSpec((tm, c), lambda i: (i, 0)),
        out_shape=jax.ShapeDtypeStruct((rows, c), F32),
        compiler_params=_cparams("parallel"),
        name="in_proj",
    )(x2d, g.reshape(1, d), w_bf16)


def _inproj_prompt_kernel(x_ref, g_ref, w_ref, wt_ref, h_ref, krm_ref, qt_ref, vtb_ref,
                          kt_ref, vt_ref):
    n_main = len(PROMPT_BLOCKS) * GROUP_W
    xn = _rms(x_ref[...], g_ref[...]).astype(BF16)
    hm = _dot(xn, w_ref[...])
    h_ref[...] = hm[:, :n_main]
    krm_ref[...] = hm[:, n_main:].astype(BF16)
    t = _dot_nt(wt_ref[...], xn)
    qt_ref[...] = (t[0:GROUP_W] * (ATT_SCALE * LOG2E)).astype(BF16)
    kt_ref[...] = t[GROUP_W:2 * GROUP_W]
    v_t = t[2 * GROUP_W:3 * GROUP_W]
    vt_ref[...] = v_t
    vtb_ref[...] = v_t.astype(BF16)


def in_proj_prompt(x2d, g, w_in, n_seq, t_len, tm):
    rows, d = x2d.shape
    nb = t_len // tm
    blk = lambda b: w_in[:, b * GROUP_W:(b + 1) * GROUP_W]
    w_main = jnp.concatenate([blk(b) for b in PROMPT_BLOCKS + (C_K,)], axis=1).astype(BF16)
    w_t = jnp.concatenate([blk(C_Q), blk(C_K), blk(C_V)], axis=1).T.astype(BF16)
    n_main = len(PROMPT_BLOCKS) * GROUP_W
    full = lambda shape: pl.BlockSpec(shape, lambda n, i: (0,) * len(shape))
    blocked = pl.BlockSpec((None, None, GROUP_W, tm), lambda n, i: (n, i, 0, 0))
    final_t = pl.BlockSpec((None, GROUP_W, tm), lambda n, i: (n, 0, i))
    return pl.pallas_call(
        _inproj_prompt_kernel,
        grid=(n_seq, nb),
        in_specs=[pl.BlockSpec((tm, d), lambda n, i: (n * nb + i, 0)),
                  full((1, d)), full(w_main.shape), full(w_t.shape)],
        out_specs=[pl.BlockSpec((tm, n_main), lambda n, i: (n * nb + i, 0)),
                   pl.BlockSpec((tm, GROUP_W), lambda n, i: (n * nb + i, 0)),
                   blocked, blocked, final_t, final_t],
        out_shape=[jax.ShapeDtypeStruct((rows, n_main), F32),
                   jax.ShapeDtypeStruct((rows, GROUP_W), BF16),
                   jax.ShapeDtypeStruct((n_seq, nb, GROUP_W, tm), BF16),
                   jax.ShapeDtypeStruct((n_seq, nb, GROUP_W, tm), BF16),
                   jax.ShapeDtypeStruct((n_seq, GROUP_W, t_len), F32),
                   jax.ShapeDtypeStruct((n_seq, GROUP_W, t_len), F32)],
        compiler_params=_cparams("parallel", "parallel"),
        name="in_proj_prompt",
    )(x2d, g.reshape(1, d), w_main, w_t)


def _memkv_kernel(x_ref, wt_ref, o_ref):
    o_ref[...] = _dot_nt(wt_ref[...], x_ref[...].astype(BF16))


def mem_kv_t(mem, w_mem):
    n_seq, n_mem, d = mem.shape
    w_t = w_mem.T.astype(BF16)
    return pl.pallas_call(
        _memkv_kernel,
        grid=(n_seq,),
        in_specs=[pl.BlockSpec((None, n_mem, d), lambda n: (n, 0, 0)),
                  pl.BlockSpec(w_t.shape, lambda n: (0, 0))],
        out_specs=pl.BlockSpec((None, w_t.shape[0], n_mem), lambda n: (n, 0, 0)),
        out_shape=jax.ShapeDtypeStruct((n_seq, w_t.shape[0], n_mem), F32),
        compiler_params=_cparams("parallel"),
        name="mem_kv",
    )(mem, w_t)


def _ssm_disc_kernel(lre_ref, lim_ref, dt_ref, bre_ref, bim_ref,
                     lbre_ref, lbim_ref, bbre_ref, bbim_ref):
    lre = lre_ref[...]
    lim = lim_ref[...]
    dt = jnp.exp(dt_ref[...])
    mag = jnp.exp(lre * dt)
    lbre = mag * jnp.cos(lim * dt)
    lbim = mag * jnp.sin(lim * dt)
    lbre_ref[...] = lbre
    lbim_ref[...] = lbim
    nre = lbre - 1.0
    nim = lbim
    den = lre * lre + lim * lim
    cre = (nre * lre + nim * lim) / den
    cim = (nim * lre - nre * lim) / den
    bre = bre_ref[...]
    bim = bim_ref[...]
    bbre_ref[...] = cre * bre - cim * bim
    bbim_ref[...] = cre * bim + cim * bre


def ssm_discretise(lam_re, lam_im, log_dt, b_re, b_im):
    col = lambda a: a.reshape(SSM_W, 1)
    dt_col = jnp.broadcast_to(log_dt[:, None], (SSM_GROUPS, SSM_STATE)).reshape(SSM_W, 1)
    outs = pl.pallas_call(
        _ssm_disc_kernel,
        out_shape=[jax.ShapeDtypeStruct((SSM_W, 1), F32)] * 2
        + [jax.ShapeDtypeStruct((SSM_W, SSM_CH), F32)] * 2,
        name="ssm_discretise",
    )(col(lam_re), col(lam_im), dt_col, b_re.reshape(SSM_W, SSM_CH), b_im.reshape(SSM_W, SSM_CH))
    lbre, lbim, bbre, bbim = outs
    shp = (SSM_GROUPS, SSM_STATE, SSM_CH)
    return lbre.reshape(1, SSM_W), lbim.reshape(1, SSM_W), bbre.reshape(shp), bbim.reshape(shp)


def _block_diag_in(b_gpc):
    eye = jnp.eye(SSM_GROUPS, dtype=F32)
    m = b_gpc.transpose(0, 2, 1)[:, :, None, :] * eye[:, None, :, None]
    return m.reshape(SSM_GROUPS * SSM_CH, SSM_W)


def _block_diag_out(c_gcp):
    eye = jnp.eye(SSM_GROUPS, dtype=F32)
    m = c_gcp.transpose(0, 2, 1)[:, :, None, :] * eye[:, None, :, None]
    return m.reshape(SSM_W, SSM_GROUPS * SSM_CH)


def _ssm_pitch(t_chunk):
    p = -(-t_chunk // 8)
    return 8 * (p if p % 2 else p + 1)


def _ssm_kernel(n_seq, t_chunk, seq_major,
                u_ref, h0re_ref, h0im_ref, lre_ref, lim_ref,
                bre_ref, bim_ref,
                cre_ref, cimn_ref, d_ref, wglu_ref,
                y_ref, hre_ref, him_ref,
                bu_re, bu_im, hs_re, hs_im):
    step = pl.program_id(0)
    rows = n_seq * t_chunk

    @pl.when(step == 0)
    def _():
        hre_ref[...] = h0re_ref[...]
        him_ref[...] = h0im_ref[...]

    n_slab = SSM_W // LANES
    slab = lambda j: slice(j * LANES, (j + 1) * LANES)
    pitch = _ssm_pitch(t_chunk) if seq_major else None

    def put(ref, j, val):
        if not seq_major:
            ref[j] = val
            return
        for n in range(n_seq):
            ref[j, n * pitch:n * pitch + t_chunk, :] = val[n * t_chunk:(n + 1) * t_chunk, :]

    def get(ref, j):
        if not seq_major:
            return ref[j]
        return jnp.concatenate(
            [ref[j, n * pitch:n * pitch + t_chunk, :] for n in range(n_seq)], axis=0)

    def rows_of(t):
        if seq_major:
            return pl.ds(t, n_seq, stride=pitch)
        return pl.ds(pl.multiple_of(t * n_seq, n_seq), n_seq)

    u = u_ref[...].reshape(rows, GROUP_W)
    u_bf = u.astype(BF16)
    b_re = _dot(u_bf, bre_ref[...])
    b_im = _dot(u_bf, bim_ref[...])
    for j in range(n_slab):
        put(bu_re, j, b_re[:, slab(j)])
        put(bu_im, j, b_im[:, slab(j)])

    lre = [jnp.broadcast_to(lre_ref[:, slab(j)], (n_seq, LANES)) for j in range(n_slab)]
    lim = [jnp.broadcast_to(lim_ref[:, slab(j)], (n_seq, LANES)) for j in range(n_slab)]

    def body(t, carry):
        r = rows_of(t)
        new = []
        for j in range(n_slab):
            hr, hi = carry[j]
            nr = lre[j] * hr - lim[j] * hi + bu_re[j, r, :]
            ni = lre[j] * hi + lim[j] * hr + bu_im[j, r, :]
            hs_re[j, r, :] = nr
            hs_im[j, r, :] = ni
            new.append((nr, ni))
        return tuple(new)

    h0 = tuple((hre_ref[:, slab(j)], him_ref[:, slab(j)]) for j in range(n_slab))
    h_last = lax.fori_loop(0, t_chunk, body, h0)
    for j in range(n_slab):
        hre_ref[:, slab(j)] = h_last[j][0]
        him_ref[:, slab(j)] = h_last[j][1]

    h_re = jnp.concatenate([get(hs_re, j) for j in range(n_slab)], axis=-1)
    h_im = jnp.concatenate([get(hs_im, j) for j in range(n_slab)], axis=-1)
    y = (_dot(h_re.astype(BF16), cre_ref[...])
         + _dot(h_im.astype(BF16), cimn_ref[...])
         + d_ref[...] * u)
    y = jax.nn.gelu(y)
    y = y * jax.nn.sigmoid(_dot(y.astype(BF16), wglu_ref[...]))
    y_ref[...] = y.reshape(y_ref.shape)


def ssm_scan(u, u_col, h0_re, h0_im, sp, n_seq, t_len, seq_major):
    t_chunk = min(t_len, 64)
    tr = t_chunk * n_seq
    s_rows = n_seq * _ssm_pitch(t_chunk) if seq_major else tr
    full = lambda shape: pl.BlockSpec(shape, lambda i: (0,) * len(shape))
    if seq_major:
        u_spec = pl.BlockSpec((n_seq, t_chunk, GROUP_W), lambda i: (0, i, u_col))
        y_spec = pl.BlockSpec((n_seq, t_chunk, GROUP_W), lambda i: (0, i, 0))
        y_shape = jax.ShapeDtypeStruct((n_seq, t_len, GROUP_W), F32)
    else:
        u_spec = pl.BlockSpec((tr, GROUP_W), lambda i: (i, u_col))
        y_spec = pl.BlockSpec((tr, GROUP_W), lambda i: (i, 0))
        y_shape = jax.ShapeDtypeStruct((t_len * n_seq, GROUP_W), F32)
    return pl.pallas_call(
        functools.partial(_ssm_kernel, n_seq, t_chunk, seq_major),
        grid=(t_len // t_chunk,),
        in_specs=[u_spec,
                  full((n_seq, SSM_W)), full((n_seq, SSM_W)),
                  full((1, SSM_W)), full((1, SSM_W)),
                  full((GROUP_W, SSM_W)), full((GROUP_W, SSM_W)),
                  full((SSM_W, GROUP_W)), full((SSM_W, GROUP_W)),
                  full((1, GROUP_W)), full((GROUP_W, GROUP_W))],
        out_specs=[y_spec, full((n_seq, SSM_W)), full((n_seq, SSM_W))],
        out_shape=[y_shape,
                   jax.ShapeDtypeStruct((n_seq, SSM_W), F32),
                   jax.ShapeDtypeStruct((n_seq, SSM_W), F32)],
        scratch_shapes=[pltpu.VMEM((SSM_W // LANES, s_rows, LANES), F32)] * 4,
        compiler_params=_cparams("arbitrary"),
        name="ssm_scan",
    )(u, h0_re, h0_im, sp["lbre"], sp["lbim"],
      sp["bre"], sp["bim"],
      sp["cre"], sp["cimn"], sp["d"], sp["wglu"])


def _softplus(z):
    return jnp.maximum(z, 0.0) + jnp.log(1.0 + jnp.exp(-jnp.abs(z)))


def _softplus2(z2):
    return jnp.maximum(z2, 0.0) + jnp.log(1.0 + jnp.exp2(-jnp.abs(z2))) * LOG2E


def _sb_prompt_kernel(tq, bias_ref, qt_ref, k_ref, vt_ref, o_ref, z_ref, d_ref, acc_ref):
    i = pl.program_id(1)
    s_idx = lax.broadcasted_iota(jnp.int32, (tq, tq), 0)
    t_idx = lax.broadcasted_iota(jnp.int32, (tq, tq), 1)
    neg_suffix = jnp.where(t_idx > s_idx, -1.0, 0.0).astype(BF16)
    diag_mask = s_idx < t_idx
    heads = range(N_HEADS)

    def issue_scores(kb, slot):
        rows = pl.ds(pl.multiple_of(kb * tq, tq), tq)
        for h in heads:
            z_ref[slot, h] = _dot(k_ref[rows, _head(h)], qt_ref[_head(h), :])

    def process(kb, slot, next_kb, carries, mask):
        sp_bfs = []
        for h in heads:
            z = z_ref[slot, h] + bias_ref[h] * LOG2E
            sp = _softplus2(z)
            d_ref[h] = z - sp
            sp_m = sp if mask is None else jnp.where(mask, sp, 0.0)
            sp_bfs.append(sp_m.astype(BF16))
        issue_scores(next_kb, 1 - slot)
        local = [_dot(neg_suffix, sp_bfs[h]) for h in heads]
        ws, new_carries = [], []
        for h in heads:
            w = jnp.exp2(d_ref[h] + (local[h] + carries[h]))
            if mask is not None:
                w = jnp.where(mask, w, 0.0)
            ws.append(w.astype(BF16))
            new_carries.append(carries[h] + (local[h][0:1, :] - sp_bfs[h][0:1, :].astype(F32)))
        outs = [_dot(vt_ref[kb, _head(h), :], ws[h]) for h in heads]
        return outs, tuple(new_carries)

    issue_scores(i, 0)
    zero = jnp.zeros((1, tq), F32)
    outs, carries = process(i, 0, jnp.maximum(i - 1, 0), (zero,) * N_HEADS, diag_mask)
    for h in heads:
        acc_ref[_head(h), :] = outs[h]

    def body(j, carries):
        kb = i - 1 - j
        outs, carries = process(kb, (j + 1) % 2, jnp.maximum(kb - 1, 0), carries, None)
        for h in heads:
            acc_ref[_head(h), :] += outs[h]
        return carries

    lax.fori_loop(0, i, body, carries)
    o_ref[...] = acc_ref[...].T


def sb_prompt(q_t, k_rm, v_t, bias, n_seq, t_len, tq):
    nb = t_len // tq
    return pl.pallas_call(
        functools.partial(_sb_prompt_kernel, tq),
        grid=(n_seq, nb),
        in_specs=[pl.BlockSpec(memory_space=pltpu.SMEM),
                  pl.BlockSpec((None, None, GROUP_W, tq), lambda n, i: (n, i, 0, 0)),
                  pl.BlockSpec((t_len, GROUP_W), lambda n, i: (n, 0)),
                  pl.BlockSpec((None, nb, GROUP_W, tq), lambda n, i: (n, 0, 0, 0))],
        out_specs=pl.BlockSpec((tq, GROUP_W), lambda n, i: (n * nb + i, 0)),
        out_shape=jax.ShapeDtypeStruct((n_seq * t_len, GROUP_W), F32),
        scratch_shapes=[pltpu.VMEM((2, N_HEADS, tq, tq), F32),
                        pltpu.VMEM((N_HEADS, tq, tq), F32),
                        pltpu.VMEM((GROUP_W, tq), F32)],
        compiler_params=_cparams("parallel", "arbitrary"),
        name="sb_prompt",
    )(bias, q_t, k_rm, v_t)


def _sb_rows_block(z, mask, carry, neg_suffix):
    sp = _softplus(z)
    sp_m = sp if mask is None else jnp.where(mask, sp, 0.0)
    sp_bf = sp_m.astype(BF16)
    local = _dot(sp_bf, neg_suffix)
    w = jnp.exp((z - sp) + (local + carry))
    if mask is not None:
        w = jnp.where(mask, w, 0.0)
    return w, local[:, 0:1] - sp_bf[:, 0:1].astype(F32)


def _sb_sample_kernel(t_new, page, n_pages, layer,
                      pt_ref, bias_ref, qrep_ref, knew_ref, vnew_ref, ck_hbm, cv_hbm,
                      o_ref, kbuf, vbuf, sem, qbd_ref, acc_ref, carry_ref):
    n = pl.program_id(0)
    j = pl.program_id(1)
    n_steps = pl.num_programs(1)
    step = n * n_steps + j
    slot = step % 2
    rows_q = N_HEADS * t_new
    g_pages = PAGES_PER_STEP

    def page_copies(seq, jj, slot_):
        copies = []
        for g in range(g_pages):
            phys = pt_ref[seq, n_pages - 1 - (jj * g_pages + g)]
            copies.append(pltpu.make_async_copy(ck_hbm.at[phys, layer], kbuf.at[slot_, g],
                                                sem.at[0, slot_]))
            copies.append(pltpu.make_async_copy(cv_hbm.at[phys, layer], vbuf.at[slot_, g],
                                                sem.at[1, slot_]))
        return copies

    @pl.when(step == 0)
    def _():
        for c in page_copies(n, j, slot):
            c.start()

    nxt = step + 1

    @pl.when(nxt < pl.num_programs(0) * n_steps)
    def _():
        for c in page_copies(nxt // n_steps, nxt % n_steps, 1 - slot):
            c.start()

    jk = lax.broadcasted_iota(jnp.int32, (page, page), 0)
    sk = lax.broadcasted_iota(jnp.int32, (page, page), 1)
    neg_suffix = jnp.where(jk > sk, -1.0, 0.0).astype(BF16)
    row_head = lax.broadcasted_iota(jnp.int32, (rows_q, 1), 0) // t_new
    bias_col = jnp.zeros((rows_q, 1), F32)
    for h in range(N_HEADS):
        bias_col = jnp.where(row_head == h, bias_ref[h], bias_col)

    @pl.when(j == 0)
    def _():
        lane_head = lax.broadcasted_iota(jnp.int32, (rows_q, GROUP_W), 1) // HEAD_DIM
        qbd = jnp.where(lane_head == row_head, qrep_ref[...] * ATT_SCALE, 0.0).astype(BF16)
        qbd_ref[...] = qbd
        t_of_row = lax.broadcasted_iota(jnp.int32, (rows_q, page), 0) % t_new
        key = lax.broadcasted_iota(jnp.int32, (rows_q, page), 1)
        z = _dot_nt(qbd, knew_ref[...].astype(BF16)) + bias_col
        w, total = _sb_rows_block(z, key < t_of_row, jnp.zeros((rows_q, 1), F32), neg_suffix)
        acc_ref[...] = _dot(w.astype(BF16), vnew_ref[...].astype(BF16))
        carry_ref[...] = total

    for c in page_copies(n, j, slot):
        c.wait()

    qbd = qbd_ref[...]
    z = jnp.concatenate(
        [_dot(qbd, kbuf[slot, g].astype(BF16)) for g in range(g_pages)], axis=0) + jnp.concatenate(
        [bias_col] * g_pages, axis=0)
    sp = _softplus(z)
    sp_bf = sp.astype(BF16)
    local = _dot(sp_bf, neg_suffix)
    totals = local[:, 0:1] - sp_bf[:, 0:1].astype(F32)
    carry = carry_ref[...]
    carries = []
    for g in range(g_pages):
        carries.append(carry)
        carry = carry + totals[g * rows_q:(g + 1) * rows_q, :]
    carry_ref[...] = carry
    w = jnp.exp((z - sp) + (local + jnp.concatenate(carries, axis=0))).astype(BF16)
    acc = acc_ref[...]
    for g in range(g_pages):
        acc = acc + _dot_nt(w[g * rows_q:(g + 1) * rows_q, :], vbuf[slot, g].astype(BF16))
    acc_ref[...] = acc

    @pl.when(j == n_steps - 1)
    def _():
        o_ref[...] = acc


def sb_sample(q, k_new, v_new, bias, cache_kt, cache_vt, page_table, layer):
    n_seq, t_new, _ = q.shape
    page = cache_kt.shape[3]
    n_pages = page_table.shape[1]
    assert n_pages % PAGES_PER_STEP == 0
    rows_q = N_HEADS * t_new
    q_rep = jnp.tile(q, (1, N_HEADS, 1))
    pad = ((0, 0), (0, page - t_new), (0, 0))
    k_pad = jnp.pad(k_new, pad)
    v_pad = jnp.pad(v_new, pad)
    per_seq = lambda r: pl.BlockSpec((None, r, GROUP_W), lambda n, j, pt: (n, 0, 0))
    acc = pl.pallas_call(
        functools.partial(_sb_sample_kernel, t_new, page, n_pages, layer),
        grid_spec=pltpu.PrefetchScalarGridSpec(
            num_scalar_prefetch=1,
            grid=(n_seq, n_pages // PAGES_PER_STEP),
            in_specs=[pl.BlockSpec(memory_space=pltpu.SMEM),
                      per_seq(rows_q), per_seq(page), per_seq(page),
                      pl.BlockSpec(memory_space=pl.ANY),
                      pl.BlockSpec(memory_space=pl.ANY)],
            out_specs=per_seq(rows_q),
            scratch_shapes=[pltpu.VMEM((2, PAGES_PER_STEP, GROUP_W, page), F32),
                            pltpu.VMEM((2, PAGES_PER_STEP, GROUP_W, page), F32),
                            pltpu.SemaphoreType.DMA((2, 2)),
                            pltpu.VMEM((rows_q, GROUP_W), BF16),
                            pltpu.VMEM((rows_q, GROUP_W), F32),
                            pltpu.VMEM((rows_q, 1), F32)]),
        out_shape=jax.ShapeDtypeStruct((n_seq, rows_q, GROUP_W), F32),
        compiler_params=_cparams("arbitrary", "arbitrary"),
        name="sb_sample",
    )(page_table, bias, q_rep, k_pad, v_pad, cache_kt, cache_vt)
    acc = acc.reshape(n_seq, N_HEADS, t_new, N_HEADS, HEAD_DIM)
    heads = [acc[:, h, :, h, :] for h in range(N_HEADS)]
    return jnp.stack(heads, axis=2).reshape(n_seq, t_new, GROUP_W)


def _mem_attend_head(q_h, mkt_h, mvt_h):
    s = _dot(q_h, mkt_h)
    e = jnp.exp(s - jnp.max(s, axis=-1, keepdims=True))
    return _dot_nt(e.astype(BF16), mvt_h) / jnp.sum(e, axis=-1, keepdims=True)


def _merge_tail(x, y_a, y_b, y_c, y_m, g_a, g_b, g_c, g_m, gn_ref, wout_ref):
    merged = jnp.concatenate([
        _rms(y_a, gn_ref[0:1, :]) * _silu(g_a),
        _rms(y_b, gn_ref[1:2, :]) * _silu(g_b),
        _rms(y_c, gn_ref[2:3, :]) * _silu(g_c),
        _rms(y_m, gn_ref[3:4, :]) * _silu(g_m)], axis=-1)
    return x + _dot(merged.astype(BF16), wout_ref[...])


def _conv_taps(v, vm1, vm2, b_gate, cw_ref):
    return b_gate * (vm2 * cw_ref[0:1, :] + vm1 * cw_ref[1:2, :] + v * cw_ref[2:3, :])


def _merge_prompt_kernel(tm, final,
                         x_ref, ag_ref, bb_ref, bc_ref, bx_ref, bg_ref, cg_ref, mq_ref, mg_ref,
                         hbc_ref, hbx_ref, buf_ref, ya_ref, yc_ref, mkt_ref, mvt_ref,
                         gn_ref, cw_ref, wout_ref, fg_ref,
                         o_ref, tail_ref):
    i = pl.program_id(1)
    v = bc_ref[...] * bx_ref[...]
    halo = hbc_ref[...] * hbx_ref[...]
    first = i == 0
    prev1 = jnp.where(first, buf_ref[1:2, :], halo[7:8, :])
    prev2 = jnp.where(first, buf_ref[0:1, :], halo[6:7, :])
    row = lax.broadcasted_iota(jnp.int32, (tm, 1), 0)
    vm1 = jnp.where(row == 0, prev1, pltpu.roll(v, 1, 0))
    vm2 = jnp.where(row == 0, prev2, jnp.where(row == 1, prev1, pltpu.roll(v, 2, 0)))
    y_b = _conv_taps(v, vm1, vm2, bb_ref[...], cw_ref)
    tail_ref[...] = v[tm - 8:tm, :]

    heads = []
    for h in range(N_HEADS):
        q_h = (mq_ref[:, _head(h)] * ATT_SCALE).astype(BF16)
        heads.append(_mem_attend_head(q_h, mkt_ref[_head(h), :].astype(BF16),
                                      mvt_ref[_head(h), :].astype(BF16)))
    y_m = jnp.concatenate(heads, axis=-1)

    x_new = _merge_tail(x_ref[...], ya_ref[...], y_b, yc_ref[...], y_m,
                        ag_ref[...], bg_ref[...], cg_ref[...], mg_ref[...], gn_ref, wout_ref)
    o_ref[...] = _rms(x_new, fg_ref[...]) if final else x_new


def merge_prompt(x2d, h2d, y_a, y_c, conv_buf, mkv_t, gn, conv_w, wout_bf16, final_g,
                 n_seq, t_len, final):
    d = x2d.shape[1]
    n_mem = mkv_t.shape[2]
    tm = min(t_len, 256)
    nb = t_len // tm
    col = PROMPT_COL
    hblk = lambda c: pl.BlockSpec((tm, GROUP_W), lambda n, i, c=c: (n * nb + i, col[c]))
    halo = lambda c: pl.BlockSpec(
        (8, GROUP_W), lambda n, i, c=c: (jnp.maximum((n * nb + i) * (tm // 8) - 1, 0), col[c]))
    rows = pl.BlockSpec((tm, GROUP_W), lambda n, i: (n * nb + i, 0))
    full = lambda shape: pl.BlockSpec(shape, lambda n, i: (0,) * len(shape))
    out, tail = pl.pallas_call(
        functools.partial(_merge_prompt_kernel, tm, final),
        grid=(n_seq, nb),
        in_specs=[pl.BlockSpec((tm, d), lambda n, i: (n * nb + i, 0)),
                  hblk(A_G), hblk(B_B), hblk(B_C), hblk(B_X), hblk(B_G), hblk(C_G),
                  hblk(M_Q), hblk(M_G), halo(B_C), halo(B_X),
                  pl.BlockSpec((None, CONV_K - 1, GROUP_W), lambda n, i: (n, 0, 0)),
                  rows, rows,
                  pl.BlockSpec((None, GROUP_W, n_mem), lambda n, i: (n, 0, 0)),
                  pl.BlockSpec((None, GROUP_W, n_mem), lambda n, i: (n, 1, 0)),
                  full((4, GROUP_W)), full((CONV_K, GROUP_W)), full(wout_bf16.shape),
                  full((1, d))],
        out_specs=[pl.BlockSpec((tm, d), lambda n, i: (n * nb + i, 0)),
                   pl.BlockSpec((None, 8, GROUP_W), lambda n, i: (n, 0, 0))],
        out_shape=[jax.ShapeDtypeStruct(x2d.shape, F32),
                   jax.ShapeDtypeStruct((n_seq, 8, GROUP_W), F32)],
        compiler_params=_cparams("parallel", "arbitrary"),
        name="merge_prompt",
    )(x2d, h2d, h2d, h2d, h2d, h2d, h2d, h2d, h2d, h2d, h2d, conv_buf, y_a, y_c, mkv_t, mkv_t,
      gn, conv_w, wout_bf16, final_g.reshape(1, d))
    return out, tail[:, 8 - (CONV_K - 1):, :]


def _merge_sample_kernel(t_len, seqs, final,
                         x_ref, ag_ref, bb_ref, bc_ref, bx_ref, bg_ref, cg_ref, mq_ref, mg_ref,
                         buf_ref, ya_ref, yc_ref, mkt_ref, mvt_ref,
                         gn_ref, cw_ref, wout_ref, fg_ref,
                         o_ref, v_ref):
    tm = seqs * t_len
    row = lax.broadcasted_iota(jnp.int32, (tm, 1), 0)
    seq_of_row = row // t_len
    t_of_row = row % t_len

    v = bc_ref[...] * bx_ref[...]
    v_ref[...] = v
    prev1 = jnp.zeros((tm, GROUP_W), F32)
    prev2 = jnp.zeros((tm, GROUP_W), F32)
    for s in range(seqs):
        prev1 = jnp.where(seq_of_row == s, buf_ref[s, 1:2, :], prev1)
        prev2 = jnp.where(seq_of_row == s, buf_ref[s, 0:1, :], prev2)
    vm1 = jnp.where(t_of_row == 0, prev1, pltpu.roll(v, 1, 0))
    vm2 = jnp.where(t_of_row == 0, prev2, jnp.where(t_of_row == 1, prev1, pltpu.roll(v, 2, 0)))
    y_b = _conv_taps(v, vm1, vm2, bb_ref[...], cw_ref)

    heads = []
    for h in range(N_HEADS):
        q_h = (mq_ref[:, _head(h)] * ATT_SCALE).astype(BF16)
        y_h = jnp.zeros((tm, HEAD_DIM), F32)
        for s in range(seqs):
            o = _mem_attend_head(q_h, mkt_ref[s, _head(h), :].astype(BF16),
                                 mvt_ref[s, _head(h), :].astype(BF16))
            y_h = jnp.where(seq_of_row == s, o, y_h)
        heads.append(y_h)
    y_m = jnp.concatenate(heads, axis=-1)

    x_new = _merge_tail(x_ref[...], ya_ref[...], y_b, yc_ref[...], y_m,
                        ag_ref[...], bg_ref[...], cg_ref[...], mg_ref[...], gn_ref, wout_ref)
    o_ref[...] = _rms(x_new, fg_ref[...]) if final else x_new


def merge_sample(x2d, h2d, y_a, y_c, state_conv, cache_mkt, cache_mvt, gn, conv_w, wout_bf16,
                 final_g, n_seq, t_len, layer, final):
    d = x2d.shape[1]
    n_mem = cache_mkt.shape[3]
    seqs = 16 // t_len
    tm = seqs * t_len
    hblk = lambda c: pl.BlockSpec((tm, GROUP_W), lambda i, c=c: (i, c))
    rows = pl.BlockSpec((tm, GROUP_W), lambda i: (i, 0))
    full = lambda shape: pl.BlockSpec(shape, lambda i: (0,) * len(shape))
    mem = pl.BlockSpec((seqs, None, GROUP_W, n_mem), lambda i: (i, layer, 0, 0))
    return pl.pallas_call(
        functools.partial(_merge_sample_kernel, t_len, seqs, final),
        grid=(n_seq // seqs,),
        in_specs=[pl.BlockSpec((tm, d), lambda i: (i, 0)),
                  hblk(A_G), hblk(B_B), hblk(B_C), hblk(B_X), hblk(B_G), hblk(C_G),
                  hblk(M_Q), hblk(M_G),
                  pl.BlockSpec((seqs, None, CONV_K - 1, GROUP_W), lambda i: (i, layer, 0, 0)),
                  rows, rows, mem, mem,
                  full((4, GROUP_W)), full((CONV_K, GROUP_W)), full(wout_bf16.shape),
                  full((1, d))],
        out_specs=[pl.BlockSpec((tm, d), lambda i: (i, 0)), rows],
        out_shape=[jax.ShapeDtypeStruct(x2d.shape, F32),
                   jax.ShapeDtypeStruct((x2d.shape[0], GROUP_W), F32)],
        compiler_params=_cparams("parallel"),
        name="merge_sample",
    )(x2d, h2d, h2d, h2d, h2d, h2d, h2d, h2d, h2d, state_conv, y_a, y_c, cache_mkt, cache_mvt,
      gn, conv_w, wout_bf16, final_g.reshape(1, d))


def _time_major(a, n_seq, t_len):
    return a.reshape(n_seq, t_len, -1).transpose(1, 0, 2).reshape(n_seq * t_len, -1)


def _seq_major(a, n_seq, t_len):
    return a.reshape(t_len, n_seq, -1).transpose(1, 0, 2).reshape(n_seq * t_len, -1)


def _col(h2d, c):
    return h2d[:, c * GROUP_W:(c + 1) * GROUP_W]


def _pos_minor(a):
    lead = a.shape[:-3]
    pos, heads, dim = a.shape[-3:]
    nd = len(lead)
    return a.transpose(*range(nd), nd + 1, nd + 2, nd).reshape(*lead, heads * dim, pos)


def _from_pos_minor(a_t):
    lead = a_t.shape[:-2]
    pos = a_t.shape[-1]
    nd = len(lead)
    a = a_t.reshape(*lead, N_HEADS, HEAD_DIM, pos)
    return a.transpose(*range(nd), nd + 2, nd, nd + 1)


def kernel(x_prompt, x_sample, cache_sb_k, cache_sb_v, state_ssm_re, state_ssm_im, state_conv,
           cache_mem_k, cache_mem_v, page_table, mem_prompt, norm_g, w_in, w_out, group_norm_g,
           ssm_lambda_re, ssm_lambda_im, ssm_b_re, ssm_b_im, ssm_c_re, ssm_c_im, ssm_log_dt, ssm_d,
           ssm_w_glu, conv_w, sb_bias, w_mem_kv, final_norm_g):
    n_p, l_p, d = x_prompt.shape
    n_s, l_s, _ = x_sample.shape
    depth = w_in.shape[0]
    assert l_s >= CONV_K - 1 and 16 % l_s == 0 and n_p % 8 == 0 and n_s % 8 == 0
    tq = min(l_p, 256)

    xp = x_prompt.reshape(n_p * l_p, d)
    xs = x_sample.reshape(n_s * l_s, d)
    cache_kt = _pos_minor(cache_sb_k)
    cache_vt = _pos_minor(cache_sb_v)
    cache_mkt = _pos_minor(cache_mem_k)
    cache_mvt = _pos_minor(cache_mem_v)
    zeros_h = jnp.zeros((n_p, SSM_W), F32)
    zeros_buf = jnp.zeros((n_p, CONV_K - 1, GROUP_W), F32)

    p_kt, p_vt, p_re, p_im, p_conv, p_mkvt = [], [], [], [], [], []
    s_k, s_v, s_re, s_im, s_conv = [], [], [], [], []
    for i in range(depth):
        final = i == depth - 1
        w_in_bf = w_in[i].astype(BF16)
        w_out_bf = w_out[i].astype(BF16)

        lbre, lbim, bbre, bbim = ssm_discretise(ssm_lambda_re[i], ssm_lambda_im[i], ssm_log_dt[i],
                                                ssm_b_re[i], ssm_b_im[i])
        sp = {"lbre": lbre, "lbim": lbim,
              "bre": _block_diag_in(bbre).astype(BF16), "bim": _block_diag_in(bbim).astype(BF16),
              "cre": _block_diag_out(ssm_c_re[i]).astype(BF16),
              "cimn": _block_diag_out(-ssm_c_im[i]).astype(BF16),
              "d": ssm_d[i].reshape(1, GROUP_W), "wglu": ssm_w_glu[i].astype(BF16)}

        mkv_t = mem_kv_t(mem_prompt, w_mem_kv[i])
        h_p, k_rm, q_t, v_tb, k_t, v_t = in_proj_prompt(xp, norm_g[i], w_in[i], n_p, l_p, tq)
        y_a, h_re, h_im = ssm_scan(h_p.reshape(n_p, l_p, -1), PROMPT_COL[A_U], zeros_h, zeros_h,
                                   sp, n_p, l_p, True)
        y_a = y_a.reshape(n_p * l_p, GROUP_W)
        y_c = sb_prompt(q_t, k_rm, v_tb, sb_bias[i], n_p, l_p, tq)
        xp, conv_p = merge_prompt(xp, h_p, y_a, y_c, zeros_buf, mkv_t, group_norm_g[i], conv_w[i],
                                  w_out_bf, final_norm_g, n_p, l_p, final)
        p_kt.append(k_t)
        p_vt.append(v_t)
        p_re.append(h_re.reshape(n_p, SSM_GROUPS, SSM_STATE))
        p_im.append(h_im.reshape(n_p, SSM_GROUPS, SSM_STATE))
        p_conv.append(conv_p)
        p_mkvt.append(mkv_t)

        h_s = in_proj(xs, norm_g[i], w_in_bf)
        ya_tm, h_re, h_im = ssm_scan(_time_major(_col(h_s, A_U), n_s, l_s), 0,
                                     state_ssm_re[:, i].reshape(n_s, SSM_W),
                                     state_ssm_im[:, i].reshape(n_s, SSM_W), sp, n_s, l_s, False)
        y_a = _seq_major(ya_tm, n_s, l_s)
        seq3 = lambda c: _col(h_s, c).reshape(n_s, l_s, GROUP_W)
        y_c = sb_sample(seq3(C_Q), seq3(C_K), seq3(C_V), sb_bias[i], cache_kt, cache_vt,
                        page_table, i).reshape(n_s * l_s, GROUP_W)
        xs, v_conv = merge_sample(xs, h_s, y_a, y_c, state_conv, cache_mkt, cache_mvt,
                                  group_norm_g[i], conv_w[i], w_out_bf, final_norm_g,
                                  n_s, l_s, i, final)
        s_k.append(seq3(C_K).reshape(n_s, l_s, N_HEADS, HEAD_DIM))
        s_v.append(seq3(C_V).reshape(n_s, l_s, N_HEADS, HEAD_DIM))
        s_re.append(h_re.reshape(n_s, SSM_GROUPS, SSM_STATE))
        s_im.append(h_im.reshape(n_s, SSM_GROUPS, SSM_STATE))
        s_conv.append(v_conv.reshape(n_s, l_s, GROUP_W)[:, l_s - (CONV_K - 1):, :])

    stack = lambda xs_: jnp.stack(xs_, axis=1)
    mkv = stack(p_mkvt)
    return (xp.reshape(n_p, l_p, d), xs.reshape(n_s, l_s, d),
            _from_pos_minor(stack(p_kt)), _from_pos_minor(stack(p_vt)),
            stack(p_re), stack(p_im), stack(p_conv),
            _from_pos_minor(mkv[:, :, :GROUP_W]), _from_pos_minor(mkv[:, :, GROUP_W:]),
            stack(s_k), stack(s_v), stack(s_re), stack(s_im), stack(s_conv))
```

```python
import functools

import jax
import jax.numpy as jnp
from jax import lax
from jax.experimental import pallas as pl
from jax.experimental.pallas import tpu as pltpu

F32 = jnp.float32
BF16 = jnp.bfloat16

EPS = 1e-6
GROUP_W = 256
N_IN_BLOCKS = 12
HEAD_DIM = 64
N_HEADS = 4
ATT_SCALE = HEAD_DIM ** -0.5
LOG2E = 1.4426950408889634
SSM_GROUPS = 16
SSM_CH = 16
SSM_STATE = 64
SSM_W = SSM_GROUPS * SSM_STATE
CONV_K = 3
LANES = 128

(A_U, A_G, B_B, B_C, B_X, B_G, C_Q, C_K, C_V, C_G, M_Q, M_G) = range(N_IN_BLOCKS)
PROMPT_BLOCKS = (A_U, A_G, B_B, B_C, B_X, B_G, C_G, M_Q, M_G)
PROMPT_COL = {b: j for j, b in enumerate(PROMPT_BLOCKS)}
SAMPLE_COL = {b: b for b in range(N_IN_BLOCKS)}

VMEM_LIMIT = 48 * 1024 * 1024
PAGES_PER_STEP = 16


def _cparams(*sem):
    return pltpu.CompilerParams(dimension_semantics=sem, vmem_limit_bytes=VMEM_LIMIT)


def _dot(a, b):
    return jnp.dot(a, b, preferred_element_type=F32)


def _dot_nt(a, b):
    return lax.dot_general(a, b, (((1,), (1,)), ((), ())), preferred_element_type=F32)


def _rms(x, g):
    return x * lax.rsqrt(jnp.mean(x * x, axis=-1, keepdims=True) + EPS) * g


def _silu(x):
    return x * jax.nn.sigmoid(x)


def _head(h):
    return slice(h * HEAD_DIM, (h + 1) * HEAD_DIM)


def _inproj_kernel(x_ref, g_ref, w_ref, o_ref):
    xn = _rms(x_ref[...], g_ref[...])
    o_ref[...] = _dot(xn.astype(BF16), w_ref[...])


def in_proj(x2d, g, w_bf16):
    rows, d = x2d.shape
    c = w_bf16.shape[1]
    tm = min(rows, 256)
    return pl.pallas_call(
        _inproj_kernel,
        grid=(rows // tm,),
        in_specs=[pl.BlockSpec((tm, d), lambda i: (i, 0)),
                  pl.BlockSpec((1, d), lambda i: (0, 0)),
                  pl.BlockSpec((d, c), lambda i: (0, 0))],
        out_specs=pl.BlockSpec((tm, c), lambda i: (i, 0)),
        out_shape=jax.ShapeDtypeStruct((rows, c), F32),
        compiler_params=_cparams("parallel"),
        name="in_proj",
    )(x2d, g.reshape(1, d), w_bf16)


def _inproj_prompt_kernel(x_ref, g_ref, w_ref, wt_ref, h_ref, krm_ref, qt_ref, vtb_ref,
                          kt_ref, vt_ref):
    n_main = len(PROMPT_BLOCKS) * GROUP_W
    xn = _rms(x_ref[...], g_ref[...]).astype(BF16)
    hm = _dot(xn, w_ref[...])
    h_ref[...] = hm[:, :n_main]
    krm_ref[...] = hm[:, n_main:].astype(BF16)
    t = _dot_nt(wt_ref[...], xn)
    qt_ref[...] = (t[0:GROUP_W] * (ATT_SCALE * LOG2E)).astype(BF16)
    kt_ref[...] = t[GROUP_W:2 * GROUP_W]
    v_t = t[2 * GROUP_W:3 * GROUP_W]
    vt_ref[...] = v_t
    vtb_ref[...] = v_t.astype(BF16)


def in_proj_prompt(x2d, g, w_in, n_seq, t_len, tm):
    rows, d = x2d.shape
    nb = t_len // tm
    blk = lambda b: w_in[:, b * GROUP_W:(b + 1) * GROUP_W]
    w_main = jnp.concatenate([blk(b) for b in PROMPT_BLOCKS + (C_K,)], axis=1).astype(BF16)
    w_t = jnp.concatenate([blk(C_Q), blk(C_K), blk(C_V)], axis=1).T.astype(BF16)
    n_main = len(PROMPT_BLOCKS) * GROUP_W
    full = lambda shape: pl.BlockSpec(shape, lambda n, i: (0,) * len(shape))
    blocked = pl.BlockSpec((None, None, GROUP_W, tm), lambda n, i: (n, i, 0, 0))
    final_t = pl.BlockSpec((None, GROUP_W, tm), lambda n, i: (n, 0, i))
    return pl.pallas_call(
        _inproj_prompt_kernel,
        grid=(n_seq, nb),
        in_specs=[pl.BlockSpec((tm, d), lambda n, i: (n * nb + i, 0)),
                  full((1, d)), full(w_main.shape), full(w_t.shape)],
        out_specs=[pl.BlockSpec((tm, n_main), lambda n, i: (n * nb + i, 0)),
                   pl.BlockSpec((tm, GROUP_W), lambda n, i: (n * nb + i, 0)),
                   blocked, blocked, final_t, final_t],
        out_shape=[jax.ShapeDtypeStruct((rows, n_main), F32),
                   jax.ShapeDtypeStruct((rows, GROUP_W), BF16),
                   jax.ShapeDtypeStruct((n_seq, nb, GROUP_W, tm), BF16),
                   jax.ShapeDtypeStruct((n_seq, nb, GROUP_W, tm), BF16),
                   jax.ShapeDtypeStruct((n_seq, GROUP_W, t_len), F32),
                   jax.ShapeDtypeStruct((n_seq, GROUP_W, t_len), F32)],
        compiler_params=_cparams("parallel", "parallel"),
        name="in_proj_prompt",
    )(x2d, g.reshape(1, d), w_main, w_t)


def _memkv_kernel(x_ref, wt_ref, o_ref):
    o_ref[...] = _dot_nt(wt_ref[...], x_ref[...].astype(BF16))


def mem_kv_t(mem, w_mem):
    n_seq, n_mem, d = mem.shape
    w_t = w_mem.T.astype(BF16)
    return pl.pallas_call(
        _memkv_kernel,
        grid=(n_seq,),
        in_specs=[pl.BlockSpec((None, n_mem, d), lambda n: (n, 0, 0)),
                  pl.BlockSpec(w_t.shape, lambda n: (0, 0))],
        out_specs=pl.BlockSpec((None, w_t.shape[0], n_mem), lambda n: (n, 0, 0)),
        out_shape=jax.ShapeDtypeStruct((n_seq, w_t.shape[0], n_mem), F32),
        compiler_params=_cparams("parallel"),
        name="mem_kv",
    )(mem, w_t)


def _ssm_disc_kernel(lre_ref, lim_ref, dt_ref, bre_ref, bim_ref,
                     lbre_ref, lbim_ref, bbre_ref, bbim_ref):
    lre = lre_ref[...]
    lim = lim_ref[...]
    dt = jnp.exp(dt_ref[...])
    mag = jnp.exp(lre * dt)
    lbre = mag * jnp.cos(lim * dt)
    lbim = mag * jnp.sin(lim * dt)
    lbre_ref[...] = lbre
    lbim_ref[...] = lbim
    nre = lbre - 1.0
    nim = lbim
    den = lre * lre + lim * lim
    cre = (nre * lre + nim * lim) / den
    cim = (nim * lre - nre * lim) / den
    bre = bre_ref[...]
    bim = bim_ref[...]
    bbre_ref[...] = cre * bre - cim * bim
    bbim_ref[...] = cre * bim + cim * bre


def ssm_discretise(lam_re, lam_im, log_dt, b_re, b_im):
    col = lambda a: a.reshape(SSM_W, 1)
    dt_col = jnp.broadcast_to(log_dt[:, None], (SSM_GROUPS, SSM_STATE)).reshape(SSM_W, 1)
    outs = pl.pallas_call(
        _ssm_disc_kernel,
        out_shape=[jax.ShapeDtypeStruct((SSM_W, 1), F32)] * 2
        + [jax.ShapeDtypeStruct((SSM_W, SSM_CH), F32)] * 2,
        name="ssm_discretise",
    )(col(lam_re), col(lam_im), dt_col, b_re.reshape(SSM_W, SSM_CH), b_im.reshape(SSM_W, SSM_CH))
    lbre, lbim, bbre, bbim = outs
    shp = (SSM_GROUPS, SSM_STATE, SSM_CH)
    return lbre.reshape(1, SSM_W), lbim.reshape(1, SSM_W), bbre.reshape(shp), bbim.reshape(shp)


def _block_diag_in(b_gpc):
    eye = jnp.eye(SSM_GROUPS, dtype=F32)
    m = b_gpc.transpose(0, 2, 1)[:, :, None, :] * eye[:, None, :, None]
    return m.reshape(SSM_GROUPS * SSM_CH, SSM_W)


def _block_diag_out(c_gcp):
    eye = jnp.eye(SSM_GROUPS, dtype=F32)
    m = c_gcp.transpose(0, 2, 1)[:, :, None, :] * eye[:, None, :, None]
    return m.reshape(SSM_W, SSM_GROUPS * SSM_CH)


def _ssm_pitch(t_chunk):
    p = -(-t_chunk // 8)
    return 8 * (p if p % 2 else p + 1)


def _ssm_kernel(n_seq, t_chunk, seq_major,
                u_ref, h0re_ref, h0im_ref, lre_ref, lim_ref,
                bre_ref, bim_ref,
                cre_ref, cimn_ref, d_ref, wglu_ref,
                y_ref, hre_ref, him_ref,
                bu_re, bu_im, hs_re, hs_im):
    step = pl.program_id(0)
    rows = n_seq * t_chunk

    @pl.when(step == 0)
    def _():
        hre_ref[...] = h0re_ref[...]
        him_ref[...] = h0im_ref[...]

    n_slab = SSM_W // LANES
    slab = lambda j: slice(j * LANES, (j + 1) * LANES)
    pitch = _ssm_pitch(t_chunk) if seq_major else None

    def put(ref, j, val):
        if not seq_major:
            ref[j] = val
            return
        for n in range(n_seq):
            ref[j, n * pitch:n * pitch + t_chunk, :] = val[n * t_chunk:(n + 1) * t_chunk, :]

    def get(ref, j):
        if not seq_major:
            return ref[j]
        return jnp.concatenate(
            [ref[j, n * pitch:n * pitch + t_chunk, :] for n in range(n_seq)], axis=0)

    def rows_of(t):
        if seq_major:
            return pl.ds(t, n_seq, stride=pitch)
        return pl.ds(pl.multiple_of(t * n_seq, n_seq), n_seq)

    u = u_ref[...].reshape(rows, GROUP_W)
    u_bf = u.astype(BF16)
    b_re = _dot(u_bf, bre_ref[...])
    b_im = _dot(u_bf, bim_ref[...])
    for j in range(n_slab):
        put(bu_re, j, b_re[:, slab(j)])
        put(bu_im, j, b_im[:, slab(j)])

    lre = [jnp.broadcast_to(lre_ref[:, slab(j)], (n_seq, LANES)) for j in range(n_slab)]
    lim = [jnp.broadcast_to(lim_ref[:, slab(j)], (n_seq, LANES)) for j in range(n_slab)]

    def body(t, carry):
        r = rows_of(t)
        new = []
        for j in range(n_slab):
            hr, hi = carry[j]
            nr = lre[j] * hr - lim[j] * hi + bu_re[j, r, :]
            ni = lre[j] * hi + lim[j] * hr + bu_im[j, r, :]
            hs_re[j, r, :] = nr
            hs_im[j, r, :] = ni
            new.append((nr, ni))
        return tuple(new)

    h0 = tuple((hre_ref[:, slab(j)], him_ref[:, slab(j)]) for j in range(n_slab))
    h_last = lax.fori_loop(0, t_chunk, body, h0)
    for j in range(n_slab):
        hre_ref[:, slab(j)] = h_last[j][0]
        him_ref[:, slab(j)] = h_last[j][1]

    h_re = jnp.concatenate([get(hs_re, j) for j in range(n_slab)], axis=-1)
    h_im = jnp.concatenate([get(hs_im, j) for j in range(n_slab)], axis=-1)
    y = (_dot(h_re.astype(BF16), cre_ref[...])
         + _dot(h_im.astype(BF16), cimn_ref[...])
         + d_ref[...] * u)
    y = jax.nn.gelu(y)
    y = y * jax.nn.sigmoid(_dot(y.astype(BF16), wglu_ref[...]))
    y_ref[...] = y.reshape(y_ref.shape)


def ssm_scan(u, u_col, h0_re, h0_im, sp, n_seq, t_len, seq_major):
    t_chunk = min(t_len, 64)
    tr = t_chunk * n_seq
    s_rows = n_seq * _ssm_pitch(t_chunk) if seq_major else tr
    full = lambda shape: pl.BlockSpec(shape, lambda i: (0,) * len(shape))
    if seq_major:
        u_spec = pl.BlockSpec((n_seq, t_chunk, GROUP_W), lambda i: (0, i, u_col))
        y_spec = pl.BlockSpec((n_seq, t_chunk, GROUP_W), lambda i: (0, i, 0))
        y_shape = jax.ShapeDtypeStruct((n_seq, t_len, GROUP_W), F32)
    else:
        u_spec = pl.BlockSpec((tr, GROUP_W), lambda i: (i, u_col))
        y_spec = pl.BlockSpec((tr, GROUP_W), lambda i: (i, 0))
        y_shape = jax.ShapeDtypeStruct((t_len * n_seq, GROUP_W), F32)
    return pl.pallas_call(
        functools.partial(_ssm_kernel, n_seq, t_chunk, seq_major),
        grid=(t_len // t_chunk,),
        in_specs=[u_spec,
                  full((n_seq, SSM_W)), full((n_seq, SSM_W)),
                  full((1, SSM_W)), full((1, SSM_W)),
                  full((GROUP_W, SSM_W)), full((GROUP_W, SSM_W)),
                  full((SSM_W, GROUP_W)), full((SSM_W, GROUP_W)),
                  full((1, GROUP_W)), full((GROUP_W, GROUP_W))],
        out_specs=[y_spec, full((n_seq, SSM_W)), full((n_seq, SSM_W))],
        out_shape=[y_shape,
                   jax.ShapeDtypeStruct((n_seq, SSM_W), F32),
                   jax.ShapeDtypeStruct((n_seq, SSM_W), F32)],
        scratch_shapes=[pltpu.VMEM((SSM_W // LANES, s_rows, LANES), F32)] * 4,
        compiler_params=_cparams("arbitrary"),
        name="ssm_scan",
    )(u, h0_re, h0_im, sp["lbre"], sp["lbim"],
      sp["bre"], sp["bim"],
      sp["cre"], sp["cimn"], sp["d"], sp["wglu"])


def _softplus(z):
    return jnp.maximum(z, 0.0) + jnp.log(1.0 + jnp.exp(-jnp.abs(z)))


def _softplus2(z2):
    return jnp.maximum(z2, 0.0) + jnp.log(1.0 + jnp.exp2(-jnp.abs(z2))) * LOG2E


def _sb_prompt_kernel(tq, bias_ref, qt_ref, k_ref, vt_ref, o_ref,
                      z_ref, sp_ref, d_ref, w_ref, acc_ref):
    i = pl.program_id(1)
    s_idx = lax.broadcasted_iota(jnp.int32, (tq, tq), 0)
    t_idx = lax.broadcasted_iota(jnp.int32, (tq, tq), 1)
    neg_suffix = jnp.where(t_idx > s_idx, -1.0, 0.0).astype(BF16)
    diag_mask = s_idx < t_idx
    heads = range(N_HEADS)

    def block_of(t):
        return jnp.maximum(i - t, 0)

    def scores(t):
        rows = pl.ds(pl.multiple_of(block_of(t) * tq, tq), tq)
        for h in heads:
            z_ref[t % 2, h] = _dot(k_ref[rows, _head(h)], qt_ref[_head(h), :])

    def front(t, mask):
        for h in heads:
            z = z_ref[t % 2, h] + bias_ref[h] * LOG2E
            sp = _softplus2(z)
            d_ref[t % 2, h] = z - sp
            sp_m = sp if mask is None else jnp.where(mask, sp, 0.0)
            sp_ref[t % 2, h] = sp_m.astype(BF16)

    def suffix_sums(t):
        return [_dot(neg_suffix, sp_ref[t % 2, h]) for h in heads]

    def weights(t, local, carries, mask):
        new_carries = []
        for h in heads:
            w = jnp.exp2(d_ref[t % 2, h] + (local[h] + carries[h]))
            if mask is not None:
                w = jnp.where(mask, w, 0.0)
            w_ref[h] = w.astype(BF16)
            new_carries.append(
                carries[h] + (local[h][0:1, :] - sp_ref[t % 2, h, 0:1, :].astype(F32)))
        return tuple(new_carries)

    def last(t):
        kb = block_of(t)
        for h in heads:
            acc_ref[_head(h), :] += _dot(vt_ref[kb, _head(h), :], w_ref[h])

    acc_ref[...] = jnp.zeros(acc_ref.shape, F32)
    scores(0)
    front(0, diag_mask)
    scores(1)

    def trip(t, carries, mask):
        local = suffix_sums(t)
        front(t + 1, None)
        scores(t + 2)
        carries = weights(t, local, carries, mask)
        last(t)
        return carries

    zero = jnp.zeros((1, tq), F32)
    carries = trip(0, (zero,) * N_HEADS, diag_mask)
    carries = lax.fori_loop(1, i, lambda t, c: trip(t, c, None), carries)

    @pl.when(i >= 1)
    def _():
        weights(i, suffix_sums(i), carries, None)
        last(i)

    o_ref[...] = acc_ref[...].T


def sb_prompt(q_t, k_rm, v_t, bias, n_seq, t_len, tq):
    nb = t_len // tq
    return pl.pallas_call(
        functools.partial(_sb_prompt_kernel, tq),
        grid=(n_seq, nb),
        in_specs=[pl.BlockSpec(memory_space=pltpu.SMEM),
                  pl.BlockSpec((None, None, GROUP_W, tq), lambda n, i: (n, i, 0, 0)),
                  pl.BlockSpec((t_len, GROUP_W), lambda n, i: (n, 0)),
                  pl.BlockSpec((None, nb, GROUP_W, tq), lambda n, i: (n, 0, 0, 0))],
        out_specs=pl.BlockSpec((tq, GROUP_W), lambda n, i: (n * nb + i, 0)),
        out_shape=jax.ShapeDtypeStruct((n_seq * t_len, GROUP_W), F32),
        scratch_shapes=[pltpu.VMEM((2, N_HEADS, tq, tq), F32),
                        pltpu.VMEM((2, N_HEADS, tq, tq), BF16),
                        pltpu.VMEM((2, N_HEADS, tq, tq), F32),
                        pltpu.VMEM((N_HEADS, tq, tq), BF16),
                        pltpu.VMEM((GROUP_W, tq), F32)],
        compiler_params=_cparams("parallel", "arbitrary"),
        name="sb_prompt",
    )(bias, q_t, k_rm, v_t)


def _sb_rows_block(z, mask, carry, neg_suffix):
    sp = _softplus(z)
    sp_m = sp if mask is None else jnp.where(mask, sp, 0.0)
    sp_bf = sp_m.astype(BF16)
    local = _dot(sp_bf, neg_suffix)
    w = jnp.exp((z - sp) + (local + carry))
    if mask is not None:
        w = jnp.where(mask, w, 0.0)
    return w, local[:, 0:1] - sp_bf[:, 0:1].astype(F32)


def _sb_sample_kernel(t_new, page, n_pages, layer,
                      pt_ref, bias_ref, qrep_ref, knew_ref, vnew_ref, ck_hbm, cv_hbm,
                      o_ref, kbuf, vbuf, sem, qbd_ref, acc_ref, carry_ref):
    n = pl.program_id(0)
    j = pl.program_id(1)
    n_steps = pl.num_programs(1)
    step = n * n_steps + j
    slot = step % 2
    rows_q = N_HEADS * t_new
    g_pages = PAGES_PER_STEP

    def page_copies(seq, jj, slot_):
        copies = []
        for g in range(g_pages):
            phys = pt_ref[seq, n_pages - 1 - (jj * g_pages + g)]
            copies.append(pltpu.make_async_copy(ck_hbm.at[phys, layer], kbuf.at[slot_, g],
                                                sem.at[0, slot_]))
            copies.append(pltpu.make_async_copy(cv_hbm.at[phys, layer], vbuf.at[slot_, g],
                                                sem.at[1, slot_]))
        return copies

    @pl.when(step == 0)
    def _():
        for c in page_copies(n, j, slot):
            c.start()

    nxt = step + 1

    @pl.when(nxt < pl.num_programs(0) * n_steps)
    def _():
        for c in page_copies(nxt // n_steps, nxt % n_steps, 1 - slot):
            c.start()

    jk = lax.broadcasted_iota(jnp.int32, (page, page), 0)
    sk = lax.broadcasted_iota(jnp.int32, (page, page), 1)
    neg_suffix = jnp.where(jk > sk, -1.0, 0.0).astype(BF16)
    row_head = lax.broadcasted_iota(jnp.int32, (rows_q, 1), 0) // t_new
    bias_col = jnp.zeros((rows_q, 1), F32)
    for h in range(N_HEADS):
        bias_col = jnp.where(row_head == h, bias_ref[h], bias_col)

    @pl.when(j == 0)
    def _():
        lane_head = lax.broadcasted_iota(jnp.int32, (rows_q, GROUP_W), 1) // HEAD_DIM
        qbd = jnp.where(lane_head == row_head, qrep_ref[...] * ATT_SCALE, 0.0).astype(BF16)
        qbd_ref[...] = qbd
        t_of_row = lax.broadcasted_iota(jnp.int32, (rows_q, page), 0) % t_new
        key = lax.broadcasted_iota(jnp.int32, (rows_q, page), 1)
        z = _dot_nt(qbd, knew_ref[...].astype(BF16)) + bias_col
        w, total = _sb_rows_block(z, key < t_of_row, jnp.zeros((rows_q, 1), F32), neg_suffix)
        acc_ref[...] = _dot(w.astype(BF16), vnew_ref[...].astype(BF16))
        carry_ref[...] = total

    for c in page_copies(n, j, slot):
        c.wait()

    qbd = qbd_ref[...]
    z = jnp.concatenate(
        [_dot(qbd, kbuf[slot, g].astype(BF16)) for g in range(g_pages)], axis=0) + jnp.concatenate(
        [bias_col] * g_pages, axis=0)
    sp = _softplus(z)
    sp_bf = sp.astype(BF16)
    local = _dot(sp_bf, neg_suffix)
    totals = local[:, 0:1] - sp_bf[:, 0:1].astype(F32)
    carry = carry_ref[...]
    carries = []
    for g in range(g_pages):
        carries.append(carry)
        carry = carry + totals[g * rows_q:(g + 1) * rows_q, :]
    carry_ref[...] = carry
    w = jnp.exp((z - sp) + (local + jnp.concatenate(carries, axis=0))).astype(BF16)
    acc = acc_ref[...]
    for g in range(g_pages):
        acc = acc + _dot_nt(w[g * rows_q:(g + 1) * rows_q, :], vbuf[slot, g].astype(BF16))
    acc_ref[...] = acc

    @pl.when(j == n_steps - 1)
    def _():
        o_ref[...] = acc


def sb_sample(q, k_new, v_new, bias, cache_kt, cache_vt, page_table, layer):
    n_seq, t_new, _ = q.shape
    page = cache_kt.shape[3]
    n_pages = page_table.shape[1]
    assert n_pages % PAGES_PER_STEP == 0
    rows_q = N_HEADS * t_new
    q_rep = jnp.tile(q, (1, N_HEADS, 1))
    pad = ((0, 0), (0, page - t_new), (0, 0))
    k_pad = jnp.pad(k_new, pad)
    v_pad = jnp.pad(v_new, pad)
    per_seq = lambda r: pl.BlockSpec((None, r, GROUP_W), lambda n, j, pt: (n, 0, 0))
    acc = pl.pallas_call(
        functools.partial(_sb_sample_kernel, t_new, page, n_pages, layer),
        grid_spec=pltpu.PrefetchScalarGridSpec(
            num_scalar_prefetch=1,
            grid=(n_seq, n_pages // PAGES_PER_STEP),
            in_specs=[pl.BlockSpec(memory_space=pltpu.SMEM),
                      per_seq(rows_q), per_seq(page), per_seq(page),
                      pl.BlockSpec(memory_space=pl.ANY),
                      pl.BlockSpec(memory_space=pl.ANY)],
            out_specs=per_seq(rows_q),
            scratch_shapes=[pltpu.VMEM((2, PAGES_PER_STEP, GROUP_W, page), F32),
                            pltpu.VMEM((2, PAGES_PER_STEP, GROUP_W, page), F32),
                            pltpu.SemaphoreType.DMA((2, 2)),
                            pltpu.VMEM((rows_q, GROUP_W), BF16),
                            pltpu.VMEM((rows_q, GROUP_W), F32),
                            pltpu.VMEM((rows_q, 1), F32)]),
        out_shape=jax.ShapeDtypeStruct((n_seq, rows_q, GROUP_W), F32),
        compiler_params=_cparams("arbitrary", "arbitrary"),
        name="sb_sample",
    )(page_table, bias, q_rep, k_pad, v_pad, cache_kt, cache_vt)
    acc = acc.reshape(n_seq, N_HEADS, t_new, N_HEADS, HEAD_DIM)
    heads = [acc[:, h, :, h, :] for h in range(N_HEADS)]
    return jnp.stack(heads, axis=2).reshape(n_seq, t_new, GROUP_W)


def _mem_attend_head(q_h, mkt_h, mvt_h):
    s = _dot(q_h, mkt_h)
    e = jnp.exp(s - jnp.max(s, axis=-1, keepdims=True))
    return _dot_nt(e.astype(BF16), mvt_h) / jnp.sum(e, axis=-1, keepdims=True)


def _merge_tail(x, y_a, y_b, y_c, y_m, g_a, g_b, g_c, g_m, gn_ref, wout_ref):
    merged = jnp.concatenate([
        _rms(y_a, gn_ref[0:1, :]) * _silu(g_a),
        _rms(y_b, gn_ref[1:2, :]) * _silu(g_b),
        _rms(y_c, gn_ref[2:3, :]) * _silu(g_c),
        _rms(y_m, gn_ref[3:4, :]) * _silu(g_m)], axis=-1)
    return x + _dot(merged.astype(BF16), wout_ref[...])


def _conv_taps(v, vm1, vm2, b_gate, cw_ref):
    return b_gate * (vm2 * cw_ref[0:1, :] + vm1 * cw_ref[1:2, :] + v * cw_ref[2:3, :])


def _merge_prompt_kernel(tm, final,
                         x_ref, ag_ref, bb_ref, bc_ref, bx_ref, bg_ref, cg_ref, mq_ref, mg_ref,
                         hbc_ref, hbx_ref, buf_ref, ya_ref, yc_ref, mkt_ref, mvt_ref,
                         gn_ref, cw_ref, wout_ref, fg_ref,
                         o_ref, tail_ref):
    i = pl.program_id(1)
    v = bc_ref[...] * bx_ref[...]
    halo = hbc_ref[...] * hbx_ref[...]
    first = i == 0
    prev1 = jnp.where(first, buf_ref[1:2, :], halo[7:8, :])
    prev2 = jnp.where(first, buf_ref[0:1, :], halo[6:7, :])
    row = lax.broadcasted_iota(jnp.int32, (tm, 1), 0)
    vm1 = jnp.where(row == 0, prev1, pltpu.roll(v, 1, 0))
    vm2 = jnp.where(row == 0, prev2, jnp.where(row == 1, prev1, pltpu.roll(v, 2, 0)))
    y_b = _conv_taps(v, vm1, vm2, bb_ref[...], cw_ref)
    tail_ref[...] = v[tm - 8:tm, :]

    heads = []
    for h in range(N_HEADS):
        q_h = (mq_ref[:, _head(h)] * ATT_SCALE).astype(BF16)
        heads.append(_mem_attend_head(q_h, mkt_ref[_head(h), :].astype(BF16),
                                      mvt_ref[_head(h), :].astype(BF16)))
    y_m = jnp.concatenate(heads, axis=-1)

    x_new = _merge_tail(x_ref[...], ya_ref[...], y_b, yc_ref[...], y_m,
                        ag_ref[...], bg_ref[...], cg_ref[...], mg_ref[...], gn_ref, wout_ref)
    o_ref[...] = _rms(x_new, fg_ref[...]) if final else x_new


def merge_prompt(x2d, h2d, y_a, y_c, conv_buf, mkv_t, gn, conv_w, wout_bf16, final_g,
                 n_seq, t_len, final):
    d = x2d.shape[1]
    n_mem = mkv_t.shape[2]
    tm = min(t_len, 256)
    nb = t_len // tm
    col = PROMPT_COL
    hblk = lambda c: pl.BlockSpec((tm, GROUP_W), lambda n, i, c=c: (n * nb + i, col[c]))
    halo = lambda c: pl.BlockSpec(
        (8, GROUP_W), lambda n, i, c=c: (jnp.maximum((n * nb + i) * (tm // 8) - 1, 0), col[c]))
    rows = pl.BlockSpec((tm, GROUP_W), lambda n, i: (n * nb + i, 0))
    full = lambda shape: pl.BlockSpec(shape, lambda n, i: (0,) * len(shape))
    out, tail = pl.pallas_call(
        functools.partial(_merge_prompt_kernel, tm, final),
        grid=(n_seq, nb),
        in_specs=[pl.BlockSpec((tm, d), lambda n, i: (n * nb + i, 0)),
                  hblk(A_G), hblk(B_B), hblk(B_C), hblk(B_X), hblk(B_G), hblk(C_G),
                  hblk(M_Q), hblk(M_G), halo(B_C), halo(B_X),
                  pl.BlockSpec((None, CONV_K - 1, GROUP_W), lambda n, i: (n, 0, 0)),
                  rows, rows,
                  pl.BlockSpec((None, GROUP_W, n_mem), lambda n, i: (n, 0, 0)),
                  pl.BlockSpec((None, GROUP_W, n_mem), lambda n, i: (n, 1, 0)),
                  full((4, GROUP_W)), full((CONV_K, GROUP_W)), full(wout_bf16.shape),
                  full((1, d))],
        out_specs=[pl.BlockSpec((tm, d), lambda n, i: (n * nb + i, 0)),
                   pl.BlockSpec((None, 8, GROUP_W), lambda n, i: (n, 0, 0))],
        out_shape=[jax.ShapeDtypeStruct(x2d.shape, F32),
                   jax.ShapeDtypeStruct((n_seq, 8, GROUP_W), F32)],
        compiler_params=_cparams("parallel", "arbitrary"),
        name="merge_prompt",
    )(x2d, h2d, h2d, h2d, h2d, h2d, h2d, h2d, h2d, h2d, h2d, conv_buf, y_a, y_c, mkv_t, mkv_t,
      gn, conv_w, wout_bf16, final_g.reshape(1, d))
    return out, tail[:, 8 - (CONV_K - 1):, :]


def _merge_sample_kernel(t_len, seqs, final,
                         x_ref, ag_ref, bb_ref, bc_ref, bx_ref, bg_ref, cg_ref, mq_ref, mg_ref,
                         buf_ref, ya_ref, yc_ref, mkt_ref, mvt_ref,
                         gn_ref, cw_ref, wout_ref, fg_ref,
                         o_ref, v_ref):
    tm = seqs * t_len
    row = lax.broadcasted_iota(jnp.int32, (tm, 1), 0)
    seq_of_row = row // t_len
    t_of_row = row % t_len

    v = bc_ref[...] * bx_ref[...]
    v_ref[...] = v
    prev1 = jnp.zeros((tm, GROUP_W), F32)
    prev2 = jnp.zeros((tm, GROUP_W), F32)
    for s in range(seqs):
        prev1 = jnp.where(seq_of_row == s, buf_ref[s, 1:2, :], prev1)
        prev2 = jnp.where(seq_of_row == s, buf_ref[s, 0:1, :], prev2)
    vm1 = jnp.where(t_of_row == 0, prev1, pltpu.roll(v, 1, 0))
    vm2 = jnp.where(t_of_row == 0, prev2, jnp.where(t_of_row == 1, prev1, pltpu.roll(v, 2, 0)))
    y_b = _conv_taps(v, vm1, vm2, bb_ref[...], cw_ref)

    heads = []
    for h in range(N_HEADS):
        q_h = (mq_ref[:, _head(h)] * ATT_SCALE).astype(BF16)
        y_h = jnp.zeros((tm, HEAD_DIM), F32)
        for s in range(seqs):
            o = _mem_attend_head(q_h, mkt_ref[s, _head(h), :].astype(BF16),
                                 mvt_ref[s, _head(h), :].astype(BF16))
            y_h = jnp.where(seq_of_row == s, o, y_h)
        heads.append(y_h)
    y_m = jnp.concatenate(heads, axis=-1)

    x_new = _merge_tail(x_ref[...], ya_ref[...], y_b, yc_ref[...], y_m,
                        ag_ref[...], bg_ref[...], cg_ref[...], mg_ref[...], gn_ref, wout_ref)
    o_ref[...] = _rms(x_new, fg_ref[...]) if final else x_new


def merge_sample(x2d, h2d, y_a, y_c, state_conv, cache_mkt, cache_mvt, gn, conv_w, wout_bf16,
                 final_g, n_seq, t_len, layer, final):
    d = x2d.shape[1]
    n_mem = cache_mkt.shape[3]
    seqs = 16 // t_len
    tm = seqs * t_len
    hblk = lambda c: pl.BlockSpec((tm, GROUP_W), lambda i, c=c: (i, c))
    rows = pl.BlockSpec((tm, GROUP_W), lambda i: (i, 0))
    full = lambda shape: pl.BlockSpec(shape, lambda i: (0,) * len(shape))
    mem = pl.BlockSpec((seqs, None, GROUP_W, n_mem), lambda i: (i, layer, 0, 0))
    return pl.pallas_call(
        functools.partial(_merge_sample_kernel, t_len, seqs, final),
        grid=(n_seq // seqs,),
        in_specs=[pl.BlockSpec((tm, d), lambda i: (i, 0)),
                  hblk(A_G), hblk(B_B), hblk(B_C), hblk(B_X), hblk(B_G), hblk(C_G),
                  hblk(M_Q), hblk(M_G),
                  pl.BlockSpec((seqs, None, CONV_K - 1, GROUP_W), lambda i: (i, layer, 0, 0)),
                  rows, rows, mem, mem,
                  full((4, GROUP_W)), full((CONV_K, GROUP_W)), full(wout_bf16.shape),
                  full((1, d))],
        out_specs=[pl.BlockSpec((tm, d), lambda i: (i, 0)), rows],
        out_shape=[jax.ShapeDtypeStruct(x2d.shape, F32),
                   jax.ShapeDtypeStruct((x2d.shape[0], GROUP_W), F32)],
        compiler_params=_cparams("parallel"),
        name="merge_sample",
    )(x2d, h2d, h2d, h2d, h2d, h2d, h2d, h2d, h2d, state_conv, y_a, y_c, cache_mkt, cache_mvt,
      gn, conv_w, wout_bf16, final_g.reshape(1, d))


def _time_major(a, n_seq, t_len):
    return a.reshape(n_seq, t_len, -1).transpose(1, 0, 2).reshape(n_seq * t_len, -1)


def _seq_major(a, n_seq, t_len):
    return a.reshape(t_len, n_seq, -1).transpose(1, 0, 2).reshape(n_seq * t_len, -1)


def _col(h2d, c):
    return h2d[:, c * GROUP_W:(c + 1) * GROUP_W]


def _pos_minor(a):
    lead = a.shape[:-3]
    pos, heads, dim = a.shape[-3:]
    nd = len(lead)
    return a.transpose(*range(nd), nd + 1, nd + 2, nd).reshape(*lead, heads * dim, pos)


def _from_pos_minor(a_t):
    lead = a_t.shape[:-2]
    pos = a_t.shape[-1]
    nd = len(lead)
    a = a_t.reshape(*lead, N_HEADS, HEAD_DIM, pos)
    return a.transpose(*range(nd), nd + 2, nd, nd + 1)


def kernel(x_prompt, x_sample, cache_sb_k, cache_sb_v, state_ssm_re, state_ssm_im, state_conv,
           cache_mem_k, cache_mem_v, page_table, mem_prompt, norm_g, w_in, w_out, group_norm_g,
           ssm_lambda_re, ssm_lambda_im, ssm_b_re, ssm_b_im, ssm_c_re, ssm_c_im, ssm_log_dt, ssm_d,
           ssm_w_glu, conv_w, sb_bias, w_mem_kv, final_norm_g):
    n_p, l_p, d = x_prompt.shape
    n_s, l_s, _ = x_sample.shape
    depth = w_in.shape[0]
    assert l_s >= CONV_K - 1 and 16 % l_s == 0 and n_p % 8 == 0 and n_s % 8 == 0
    tq = min(l_p, 256)

    xp = x_prompt.reshape(n_p * l_p, d)
    xs = x_sample.reshape(n_s * l_s, d)
    cache_kt = _pos_minor(cache_sb_k)
    cache_vt = _pos_minor(cache_sb_v)
    cache_mkt = _pos_minor(cache_mem_k)
    cache_mvt = _pos_minor(cache_mem_v)
    zeros_h = jnp.zeros((n_p, SSM_W), F32)
    zeros_buf = jnp.zeros((n_p, CONV_K - 1, GROUP_W), F32)

    p_kt, p_vt, p_re, p_im, p_conv, p_mkvt = [], [], [], [], [], []
    s_k, s_v, s_re, s_im, s_conv = [], [], [], [], []
    for i in range(depth):
        final = i == depth - 1
        w_in_bf = w_in[i].astype(BF16)
        w_out_bf = w_out[i].astype(BF16)

        lbre, lbim, bbre, bbim = ssm_discretise(ssm_lambda_re[i], ssm_lambda_im[i], ssm_log_dt[i],
                                                ssm_b_re[i], ssm_b_im[i])
        sp = {"lbre": lbre, "lbim": lbim,
              "bre": _block_diag_in(bbre).astype(BF16), "bim": _block_diag_in(bbim).astype(BF16),
              "cre": _block_diag_out(ssm_c_re[i]).astype(BF16),
              "cimn": _block_diag_out(-ssm_c_im[i]).astype(BF16),
              "d": ssm_d[i].reshape(1, GROUP_W), "wglu": ssm_w_glu[i].astype(BF16)}

        mkv_t = mem_kv_t(mem_prompt, w_mem_kv[i])
        h_p, k_rm, q_t, v_tb, k_t, v_t = in_proj_prompt(xp, norm_g[i], w_in[i], n_p, l_p, tq)
        y_a, h_re, h_im = ssm_scan(h_p.reshape(n_p, l_p, -1), PROMPT_COL[A_U], zeros_h, zeros_h,
                                   sp, n_p, l_p, True)
        y_a = y_a.reshape(n_p * l_p, GROUP_W)
        y_c = sb_prompt(q_t, k_rm, v_tb, sb_bias[i], n_p, l_p, tq)
        xp, conv_p = merge_prompt(xp, h_p, y_a, y_c, zeros_buf, mkv_t, group_norm_g[i], conv_w[i],
                                  w_out_bf, final_norm_g, n_p, l_p, final)
        p_kt.append(k_t)
        p_vt.append(v_t)
        p_re.append(h_re.reshape(n_p, SSM_GROUPS, SSM_STATE))
        p_im.append(h_im.reshape(n_p, SSM_GROUPS, SSM_STATE))
        p_conv.append(conv_p)
        p_mkvt.append(mkv_t)

        h_s = in_proj(xs, norm_g[i], w_in_bf)
        ya_tm, h_re, h_im = ssm_scan(_time_major(_col(h_s, A_U), n_s, l_s), 0,
                                     state_ssm_re[:, i].reshape(n_s, SSM_W),
                                     state_ssm_im[:, i].reshape(n_s, SSM_W), sp, n_s, l_s, False)
        y_a = _seq_major(ya_tm, n_s, l_s)
        seq3 = lambda c: _col(h_s, c).reshape(n_s, l_s, GROUP_W)
        y_c = sb_sample(seq3(C_Q), seq3(C_K), seq3(C_V), sb_bias[i], cache_kt, cache_vt,
                        page_table, i).reshape(n_s * l_s, GROUP_W)
        xs, v_conv = merge_sample(xs, h_s, y_a, y_c, state_conv, cache_mkt, cache_mvt,
                                  group_norm_g[i], conv_w[i], w_out_bf, final_norm_g,
                                  n_s, l_s, i, final)
        s_k.append(seq3(C_K).reshape(n_s, l_s, N_HEADS, HEAD_DIM))
        s_v.append(seq3(C_V).reshape(n_s, l_s, N_HEADS, HEAD_DIM))
        s_re.append(h_re.reshape(n_s, SSM_GROUPS, SSM_STATE))
        s_im.append(h_im.reshape(n_s, SSM_GROUPS, SSM_STATE))
        s_conv.append(v_conv.reshape(n_s, l_s, GROUP_W)[:, l_s - (CONV_K - 1):, :])

    stack = lambda xs_: jnp.stack(xs_, axis=1)
    mkv = stack(p_mkvt)
    return (xp.reshape(n_p, l_p, d), xs.reshape(n_s, l_s, d),
            _from_pos_minor(stack(p_kt)), _from_pos_minor(stack(p_vt)),
            stack(p_re), stack(p_im), stack(p_conv),
            _from_pos_minor(mkv[:, :, :GROUP_W]), _from_pos_minor(mkv[:, :, GROUP_W:]),
            stack(s_k), stack(s_v), stack(s_re), stack(s_im), stack(s_conv))
```

```python
import functools

import jax
import jax.numpy as jnp
from jax import lax
from jax.experimental import pallas as pl
from jax.experimental.pallas import tpu as pltpu

F32 = jnp.float32
BF16 = jnp.bfloat16

EPS = 1e-6
GROUP_W = 256
N_IN_BLOCKS = 12
HEAD_DIM = 64
N_HEADS = 4
ATT_SCALE = HEAD_DIM ** -0.5
LOG2E = 1.4426950408889634
SSM_GROUPS = 16
SSM_CH = 16
SSM_STATE = 64
SSM_W = SSM_GROUPS * SSM_STATE
CONV_K = 3
LANES = 128

(A_U, A_G, B_B, B_C, B_X, B_G, C_Q, C_K, C_V, C_G, M_Q, M_G) = range(N_IN_BLOCKS)
PROMPT_BLOCKS = (A_U, A_G, B_B, B_C, B_X, B_G, C_G, M_Q, M_G)
PROMPT_COL = {b: j for j, b in enumerate(PROMPT_BLOCKS)}
SAMPLE_COL = {b: b for b in range(N_IN_BLOCKS)}

VMEM_LIMIT = 48 * 1024 * 1024
PAGES_PER_STEP = 32


def _cparams(*sem):
    return pltpu.CompilerParams(dimension_semantics=sem, vmem_limit_bytes=VMEM_LIMIT)


def _dot(a, b):
    return jnp.dot(a, b, preferred_element_type=F32)


def _dot_nt(a, b):
    return lax.dot_general(a, b, (((1,), (1,)), ((), ())), preferred_element_type=F32)


def _rms(x, g):
    return x * lax.rsqrt(jnp.mean(x * x, axis=-1, keepdims=True) + EPS) * g


def _silu(x):
    return x * jax.nn.sigmoid(x)


def _head(h):
    return slice(h * HEAD_DIM, (h + 1) * HEAD_DIM)


def _inproj_kernel(x_ref, g_ref, w_ref, o_ref):
    xn = _rms(x_ref[...], g_ref[...])
    o_ref[...] = _dot(xn.astype(BF16), w_ref[...])


def in_proj(x2d, g, w_bf16):
    rows, d = x2d.shape
    c = w_bf16.shape[1]
    tm = min(rows, 256)
    return pl.pallas_call(
        _inproj_kernel,
        grid=(rows // tm,),
        in_specs=[pl.BlockSpec((tm, d), lambda i: (i, 0)),
                  pl.BlockSpec((1, d), lambda i: (0, 0)),
                  pl.BlockSpec((d, c), lambda i: (0, 0))],
        out_specs=pl.BlockSpec((tm, c), lambda i: (i, 0)),
        out_shape=jax.ShapeDtypeStruct((rows, c), F32),
        compiler_params=_cparams("parallel"),
        name="in_proj",
    )(x2d, g.reshape(1, d), w_bf16)


def _inproj_prompt_kernel(x_ref, g_ref, w_ref, wt_ref, h_ref, krm_ref, qt_ref, vtb_ref,
                          kt_ref, vt_ref):
    n_main = len(PROMPT_BLOCKS) * GROUP_W
    xn = _rms(x_ref[...], g_ref[...]).astype(BF16)
    hm = _dot(xn, w_ref[...])
    h_ref[...] = hm[:, :n_main]
    krm_ref[...] = hm[:, n_main:].astype(BF16)
    t = _dot_nt(wt_ref[...], xn)
    qt_ref[...] = (t[0:GROUP_W] * (ATT_SCALE * LOG2E)).astype(BF16)
    kt_ref[...] = t[GROUP_W:2 * GROUP_W]
    v_t = t[2 * GROUP_W:3 * GROUP_W]
    vt_ref[...] = v_t
    vtb_ref[...] = v_t.astype(BF16)


def in_proj_prompt(x2d, g, w_in, n_seq, t_len, tm):
    rows, d = x2d.shape
    nb = t_len // tm
    blk = lambda b: w_in[:, b * GROUP_W:(b + 1) * GROUP_W]
    w_main = jnp.concatenate([blk(b) for b in PROMPT_BLOCKS + (C_K,)], axis=1).astype(BF16)
    w_t = jnp.concatenate([blk(C_Q), blk(C_K), blk(C_V)], axis=1).T.astype(BF16)
    n_main = len(PROMPT_BLOCKS) * GROUP_W
    full = lambda shape: pl.BlockSpec(shape, lambda n, i: (0,) * len(shape))
    blocked = pl.BlockSpec((None, None, GROUP_W, tm), lambda n, i: (n, i, 0, 0))
    final_t = pl.BlockSpec((None, GROUP_W, tm), lambda n, i: (n, 0, i))
    return pl.pallas_call(
        _inproj_prompt_kernel,
        grid=(n_seq, nb),
        in_specs=[pl.BlockSpec((tm, d), lambda n, i: (n * nb + i, 0)),
                  full((1, d)), full(w_main.shape), full(w_t.shape)],
        out_specs=[pl.BlockSpec((tm, n_main), lambda n, i: (n * nb + i, 0)),
                   pl.BlockSpec((tm, GROUP_W), lambda n, i: (n * nb + i, 0)),
                   blocked, blocked, final_t, final_t],
        out_shape=[jax.ShapeDtypeStruct((rows, n_main), F32),
                   jax.ShapeDtypeStruct((rows, GROUP_W), BF16),
                   jax.ShapeDtypeStruct((n_seq, nb, GROUP_W, tm), BF16),
                   jax.ShapeDtypeStruct((n_seq, nb, GROUP_W, tm), BF16),
                   jax.ShapeDtypeStruct((n_seq, GROUP_W, t_len), F32),
                   jax.ShapeDtypeStruct((n_seq, GROUP_W, t_len), F32)],
        compiler_params=_cparams("parallel", "parallel"),
        name="in_proj_prompt",
    )(x2d, g.reshape(1, d), w_main, w_t)


def _memkv_kernel(x_ref, wt_ref, o_ref):
    o_ref[...] = _dot_nt(wt_ref[...], x_ref[...].astype(BF16))


def mem_kv_t(mem, w_mem):
    n_seq, n_mem, d = mem.shape
    w_t = w_mem.T.astype(BF16)
    return pl.pallas_call(
        _memkv_kernel,
        grid=(n_seq,),
        in_specs=[pl.BlockSpec((None, n_mem, d), lambda n: (n, 0, 0)),
                  pl.BlockSpec(w_t.shape, lambda n: (0, 0))],
        out_specs=pl.BlockSpec((None, w_t.shape[0], n_mem), lambda n: (n, 0, 0)),
        out_shape=jax.ShapeDtypeStruct((n_seq, w_t.shape[0], n_mem), F32),
        compiler_params=_cparams("parallel"),
        name="mem_kv",
    )(mem, w_t)


def _ssm_disc_kernel(lre_ref, lim_ref, dt_ref, bre_ref, bim_ref,
                     lbre_ref, lbim_ref, bbre_ref, bbim_ref):
    lre = lre_ref[...]
    lim = lim_ref[...]
    dt = jnp.exp(dt_ref[...])
    mag = jnp.exp(lre * dt)
    lbre = mag * jnp.cos(lim * dt)
    lbim = mag * jnp.sin(lim * dt)
    lbre_ref[...] = lbre
    lbim_ref[...] = lbim
    nre = lbre - 1.0
    nim = lbim
    den = lre * lre + lim * lim
    cre = (nre * lre + nim * lim) / den
    cim = (nim * lre - nre * lim) / den
    bre = bre_ref[...]
    bim = bim_ref[...]
    bbre_ref[...] = cre * bre - cim * bim
    bbim_ref[...] = cre * bim + cim * bre


def ssm_discretise(lam_re, lam_im, log_dt, b_re, b_im):
    col = lambda a: a.reshape(SSM_W, 1)
    dt_col = jnp.broadcast_to(log_dt[:, None], (SSM_GROUPS, SSM_STATE)).reshape(SSM_W, 1)
    outs = pl.pallas_call(
        _ssm_disc_kernel,
        out_shape=[jax.ShapeDtypeStruct((SSM_W, 1), F32)] * 2
        + [jax.ShapeDtypeStruct((SSM_W, SSM_CH), F32)] * 2,
        name="ssm_discretise",
    )(col(lam_re), col(lam_im), dt_col, b_re.reshape(SSM_W, SSM_CH), b_im.reshape(SSM_W, SSM_CH))
    lbre, lbim, bbre, bbim = outs
    shp = (SSM_GROUPS, SSM_STATE, SSM_CH)
    return lbre.reshape(1, SSM_W), lbim.reshape(1, SSM_W), bbre.reshape(shp), bbim.reshape(shp)


def _block_diag_in(b_gpc):
    eye = jnp.eye(SSM_GROUPS, dtype=F32)
    m = b_gpc.transpose(0, 2, 1)[:, :, None, :] * eye[:, None, :, None]
    return m.reshape(SSM_GROUPS * SSM_CH, SSM_W)


def _block_diag_out(c_gcp):
    eye = jnp.eye(SSM_GROUPS, dtype=F32)
    m = c_gcp.transpose(0, 2, 1)[:, :, None, :] * eye[:, None, :, None]
    return m.reshape(SSM_W, SSM_GROUPS * SSM_CH)


def _ssm_pitch(t_chunk):
    p = -(-t_chunk // 8)
    return 8 * (p if p % 2 else p + 1)


def _ssm_kernel(n_seq, t_chunk, seq_major,
                u_ref, h0re_ref, h0im_ref, lre_ref, lim_ref,
                bre_ref, bim_ref,
                cre_ref, cimn_ref, d_ref, wglu_ref,
                y_ref, hre_ref, him_ref,
                bu_re, bu_im, hs_re, hs_im):
    step = pl.program_id(0)
    rows = n_seq * t_chunk

    @pl.when(step == 0)
    def _():
        hre_ref[...] = h0re_ref[...]
        him_ref[...] = h0im_ref[...]

    n_slab = SSM_W // LANES
    slab = lambda j: slice(j * LANES, (j + 1) * LANES)
    pitch = _ssm_pitch(t_chunk) if seq_major else None

    def put(ref, j, val):
        if not seq_major:
            ref[j] = val
            return
        for n in range(n_seq):
            ref[j, n * pitch:n * pitch + t_chunk, :] = val[n * t_chunk:(n + 1) * t_chunk, :]

    def get(ref, j):
        if not seq_major:
            return ref[j]
        return jnp.concatenate(
            [ref[j, n * pitch:n * pitch + t_chunk, :] for n in range(n_seq)], axis=0)

    def rows_of(t):
        if seq_major:
            return pl.ds(t, n_seq, stride=pitch)
        return pl.ds(pl.multiple_of(t * n_seq, n_seq), n_seq)

    u = u_ref[...].reshape(rows, GROUP_W)
    u_bf = u.astype(BF16)
    b_re = _dot(u_bf, bre_ref[...])
    b_im = _dot(u_bf, bim_ref[...])
    for j in range(n_slab):
        put(bu_re, j, b_re[:, slab(j)])
        put(bu_im, j, b_im[:, slab(j)])

    lre = [jnp.broadcast_to(lre_ref[:, slab(j)], (n_seq, LANES)) for j in range(n_slab)]
    lim = [jnp.broadcast_to(lim_ref[:, slab(j)], (n_seq, LANES)) for j in range(n_slab)]

    def body(t, carry):
        r = rows_of(t)
        new = []
        for j in range(n_slab):
            hr, hi = carry[j]
            nr = lre[j] * hr - lim[j] * hi + bu_re[j, r, :]
            ni = lre[j] * hi + lim[j] * hr + bu_im[j, r, :]
            hs_re[j, r, :] = nr
            hs_im[j, r, :] = ni
            new.append((nr, ni))
        return tuple(new)

    h0 = tuple((hre_ref[:, slab(j)], him_ref[:, slab(j)]) for j in range(n_slab))
    h_last = lax.fori_loop(0, t_chunk, body, h0)
    for j in range(n_slab):
        hre_ref[:, slab(j)] = h_last[j][0]
        him_ref[:, slab(j)] = h_last[j][1]

    h_re = jnp.concatenate([get(hs_re, j) for j in range(n_slab)], axis=-1)
    h_im = jnp.concatenate([get(hs_im, j) for j in range(n_slab)], axis=-1)
    y = (_dot(h_re.astype(BF16), cre_ref[...])
         + _dot(h_im.astype(BF16), cimn_ref[...])
         + d_ref[...] * u)
    y = jax.nn.gelu(y)
    y = y * jax.nn.sigmoid(_dot(y.astype(BF16), wglu_ref[...]))
    y_ref[...] = y.reshape(y_ref.shape)


def ssm_scan(u, u_col, h0_re, h0_im, sp, n_seq, t_len, seq_major):
    t_chunk = min(t_len, 128)
    tr = t_chunk * n_seq
    s_rows = n_seq * _ssm_pitch(t_chunk) if seq_major else tr
    full = lambda shape: pl.BlockSpec(shape, lambda i: (0,) * len(shape))
    if seq_major:
        u_spec = pl.BlockSpec((n_seq, t_chunk, GROUP_W), lambda i: (0, i, u_col))
        y_spec = pl.BlockSpec((n_seq, t_chunk, GROUP_W), lambda i: (0, i, 0))
        y_shape = jax.ShapeDtypeStruct((n_seq, t_len, GROUP_W), F32)
    else:
        u_spec = pl.BlockSpec((tr, GROUP_W), lambda i: (i, u_col))
        y_spec = pl.BlockSpec((tr, GROUP_W), lambda i: (i, 0))
        y_shape = jax.ShapeDtypeStruct((t_len * n_seq, GROUP_W), F32)
    return pl.pallas_call(
        functools.partial(_ssm_kernel, n_seq, t_chunk, seq_major),
        grid=(t_len // t_chunk,),
        in_specs=[u_spec,
                  full((n_seq, SSM_W)), full((n_seq, SSM_W)),
                  full((1, SSM_W)), full((1, SSM_W)),
                  full((GROUP_W, SSM_W)), full((GROUP_W, SSM_W)),
                  full((SSM_W, GROUP_W)), full((SSM_W, GROUP_W)),
                  full((1, GROUP_W)), full((GROUP_W, GROUP_W))],
        out_specs=[y_spec, full((n_seq, SSM_W)), full((n_seq, SSM_W))],
        out_shape=[y_shape,
                   jax.ShapeDtypeStruct((n_seq, SSM_W), F32),
                   jax.ShapeDtypeStruct((n_seq, SSM_W), F32)],
        scratch_shapes=[pltpu.VMEM((SSM_W // LANES, s_rows, LANES), F32)] * 4,
        compiler_params=_cparams("arbitrary"),
        name="ssm_scan",
    )(u, h0_re, h0_im, sp["lbre"], sp["lbim"],
      sp["bre"], sp["bim"],
      sp["cre"], sp["cimn"], sp["d"], sp["wglu"])


def _softplus(z):
    return jnp.maximum(z, 0.0) + jnp.log(1.0 + jnp.exp(-jnp.abs(z)))


def _softplus2(z2):
    return jnp.maximum(z2, 0.0) + jnp.log(1.0 + jnp.exp2(-jnp.abs(z2))) * LOG2E


def _sb_prompt_kernel(tq, bias_ref, qt_ref, k_ref, vt_ref, o_ref,
                      z_ref, sp_ref, d_ref, w_ref, acc_ref):
    i = pl.program_id(1)
    s_idx = lax.broadcasted_iota(jnp.int32, (tq, tq), 0)
    t_idx = lax.broadcasted_iota(jnp.int32, (tq, tq), 1)
    neg_suffix = jnp.where(t_idx > s_idx, -1.0, 0.0).astype(BF16)
    diag_mask = s_idx < t_idx
    heads = range(N_HEADS)

    def block_of(t):
        return jnp.maximum(i - t, 0)

    def scores(t):
        rows = pl.ds(pl.multiple_of(block_of(t) * tq, tq), tq)
        for h in heads:
            z_ref[t % 2, h] = _dot(k_ref[rows, _head(h)], qt_ref[_head(h), :])

    def front(t, mask):
        for h in heads:
            z = z_ref[t % 2, h] + bias_ref[h] * LOG2E
            sp = _softplus2(z)
            d_ref[t % 2, h] = z - sp
            sp_m = sp if mask is None else jnp.where(mask, sp, 0.0)
            sp_ref[t % 2, h] = sp_m.astype(BF16)

    def suffix_sums(t):
        return [_dot(neg_suffix, sp_ref[t % 2, h]) for h in heads]

    def weights(t, local, carries, mask):
        new_carries = []
        for h in heads:
            w = jnp.exp2(d_ref[t % 2, h] + (local[h] + carries[h]))
            if mask is not None:
                w = jnp.where(mask, w, 0.0)
            w_ref[h] = w.astype(BF16)
            new_carries.append(
                carries[h] + (local[h][0:1, :] - sp_ref[t % 2, h, 0:1, :].astype(F32)))
        return tuple(new_carries)

    def last(t):
        kb = block_of(t)
        for h in heads:
            acc_ref[_head(h), :] += _dot(vt_ref[kb, _head(h), :], w_ref[h])

    acc_ref[...] = jnp.zeros(acc_ref.shape, F32)
    scores(0)
    front(0, diag_mask)
    scores(1)

    def trip(t, carries, mask):
        local = suffix_sums(t)
        front(t + 1, None)
        scores(t + 2)
        carries = weights(t, local, carries, mask)
        last(t)
        return carries

    zero = jnp.zeros((1, tq), F32)
    carries = trip(0, (zero,) * N_HEADS, diag_mask)
    carries = lax.fori_loop(1, i, lambda t, c: trip(t, c, None), carries)

    @pl.when(i >= 1)
    def _():
        weights(i, suffix_sums(i), carries, None)
        last(i)

    o_ref[...] = acc_ref[...].T


def sb_prompt(q_t, k_rm, v_t, bias, n_seq, t_len, tq):
    nb = t_len // tq
    return pl.pallas_call(
        functools.partial(_sb_prompt_kernel, tq),
        grid=(n_seq, nb),
        in_specs=[pl.BlockSpec(memory_space=pltpu.SMEM),
                  pl.BlockSpec((None, None, GROUP_W, tq), lambda n, i: (n, i, 0, 0)),
                  pl.BlockSpec((t_len, GROUP_W), lambda n, i: (n, 0)),
                  pl.BlockSpec((None, nb, GROUP_W, tq), lambda n, i: (n, 0, 0, 0))],
        out_specs=pl.BlockSpec((tq, GROUP_W), lambda n, i: (n * nb + i, 0)),
        out_shape=jax.ShapeDtypeStruct((n_seq * t_len, GROUP_W), F32),
        scratch_shapes=[pltpu.VMEM((2, N_HEADS, tq, tq), F32),
                        pltpu.VMEM((2, N_HEADS, tq, tq), BF16),
                        pltpu.VMEM((2, N_HEADS, tq, tq), F32),
                        pltpu.VMEM((N_HEADS, tq, tq), BF16),
                        pltpu.VMEM((GROUP_W, tq), F32)],
        compiler_params=_cparams("parallel", "arbitrary"),
        name="sb_prompt",
    )(bias, q_t, k_rm, v_t)


def _sb_rows_block(z, mask, carry, neg_suffix):
    sp = _softplus(z)
    sp_m = sp if mask is None else jnp.where(mask, sp, 0.0)
    sp_bf = sp_m.astype(BF16)
    local = _dot(sp_bf, neg_suffix)
    w = jnp.exp((z - sp) + (local + carry))
    if mask is not None:
        w = jnp.where(mask, w, 0.0)
    return w, local[:, 0:1] - sp_bf[:, 0:1].astype(F32)


def _sb_sample_kernel(t_new, page, n_pages, layer,
                      pt_ref, bias_ref, qrep_ref, knew_ref, vnew_ref, ck_hbm, cv_hbm,
                      o_ref, kbuf, vbuf, sem, qbd_ref, acc_ref, carry_ref):
    n = pl.program_id(0)
    j = pl.program_id(1)
    n_steps = pl.num_programs(1)
    step = n * n_steps + j
    slot = step % 2
    rows_q = N_HEADS * t_new
    g_pages = PAGES_PER_STEP

    def page_copies(seq, jj, slot_):
        copies = []
        for g in range(g_pages):
            phys = pt_ref[seq, n_pages - 1 - (jj * g_pages + g)]
            copies.append(pltpu.make_async_copy(ck_hbm.at[phys, layer], kbuf.at[slot_, g],
                                                sem.at[0, slot_]))
            copies.append(pltpu.make_async_copy(cv_hbm.at[phys, layer], vbuf.at[slot_, g],
                                                sem.at[1, slot_]))
        return copies

    @pl.when(step == 0)
    def _():
        for c in page_copies(n, j, slot):
            c.start()

    nxt = step + 1

    @pl.when(nxt < pl.num_programs(0) * n_steps)
    def _():
        for c in page_copies(nxt // n_steps, nxt % n_steps, 1 - slot):
            c.start()

    jk = lax.broadcasted_iota(jnp.int32, (page, page), 0)
    sk = lax.broadcasted_iota(jnp.int32, (page, page), 1)
    neg_suffix = jnp.where(jk > sk, -1.0, 0.0).astype(BF16)
    row_head = lax.broadcasted_iota(jnp.int32, (rows_q, 1), 0) // t_new
    bias_col = jnp.zeros((rows_q, 1), F32)
    for h in range(N_HEADS):
        bias_col = jnp.where(row_head == h, bias_ref[h], bias_col)

    @pl.when(j == 0)
    def _():
        lane_head = lax.broadcasted_iota(jnp.int32, (rows_q, GROUP_W), 1) // HEAD_DIM
        qbd = jnp.where(lane_head == row_head, qrep_ref[...] * ATT_SCALE, 0.0).astype(BF16)
        qbd_ref[...] = qbd
        t_of_row = lax.broadcasted_iota(jnp.int32, (rows_q, page), 0) % t_new
        key = lax.broadcasted_iota(jnp.int32, (rows_q, page), 1)
        z = _dot_nt(qbd, knew_ref[...].astype(BF16)) + bias_col
        w, total = _sb_rows_block(z, key < t_of_row, jnp.zeros((rows_q, 1), F32), neg_suffix)
        acc_ref[...] = _dot(w.astype(BF16), vnew_ref[...].astype(BF16))
        carry_ref[...] = total

    for c in page_copies(n, j, slot):
        c.wait()

    qbd = qbd_ref[...]
    z = jnp.concatenate(
        [_dot(qbd, kbuf[slot, g].astype(BF16)) for g in range(g_pages)], axis=0) + jnp.concatenate(
        [bias_col] * g_pages, axis=0)
    sp = _softplus(z)
    sp_bf = sp.astype(BF16)
    local = _dot(sp_bf, neg_suffix)
    totals = local[:, 0:1] - sp_bf[:, 0:1].astype(F32)
    carry = carry_ref[...]
    carries = []
    for g in range(g_pages):
        carries.append(carry)
        carry = carry + totals[g * rows_q:(g + 1) * rows_q, :]
    carry_ref[...] = carry
    w = jnp.exp((z - sp) + (local + jnp.concatenate(carries, axis=0))).astype(BF16)
    acc = acc_ref[...]
    for g in range(g_pages):
        acc = acc + _dot_nt(w[g * rows_q:(g + 1) * rows_q, :], vbuf[slot, g].astype(BF16))
    acc_ref[...] = acc

    @pl.when(j == n_steps - 1)
    def _():
        o_ref[...] = acc


def sb_sample(q, k_new, v_new, bias, cache_kt, cache_vt, page_table, layer):
    n_seq, t_new, _ = q.shape
    page = cache_kt.shape[3]
    n_pages = page_table.shape[1]
    assert n_pages % PAGES_PER_STEP == 0
    rows_q = N_HEADS * t_new
    q_rep = jnp.tile(q, (1, N_HEADS, 1))
    pad = ((0, 0), (0, page - t_new), (0, 0))
    k_pad = jnp.pad(k_new, pad)
    v_pad = jnp.pad(v_new, pad)
    per_seq = lambda r: pl.BlockSpec((None, r, GROUP_W), lambda n, j, pt: (n, 0, 0))
    acc = pl.pallas_call(
        functools.partial(_sb_sample_kernel, t_new, page, n_pages, layer),
        grid_spec=pltpu.PrefetchScalarGridSpec(
            num_scalar_prefetch=1,
            grid=(n_seq, n_pages // PAGES_PER_STEP),
            in_specs=[pl.BlockSpec(memory_space=pltpu.SMEM),
                      per_seq(rows_q), per_seq(page), per_seq(page),
                      pl.BlockSpec(memory_space=pl.ANY),
                      pl.BlockSpec(memory_space=pl.ANY)],
            out_specs=per_seq(rows_q),
            scratch_shapes=[pltpu.VMEM((2, PAGES_PER_STEP, GROUP_W, page), F32),
                            pltpu.VMEM((2, PAGES_PER_STEP, GROUP_W, page), F32),
                            pltpu.SemaphoreType.DMA((2, 2)),
                            pltpu.VMEM((rows_q, GROUP_W), BF16),
                            pltpu.VMEM((rows_q, GROUP_W), F32),
                            pltpu.VMEM((rows_q, 1), F32)]),
        out_shape=jax.ShapeDtypeStruct((n_seq, rows_q, GROUP_W), F32),
        compiler_params=_cparams("arbitrary", "arbitrary"),
        name="sb_sample",
    )(page_table, bias, q_rep, k_pad, v_pad, cache_kt, cache_vt)
    acc = acc.reshape(n_seq, N_HEADS, t_new, N_HEADS, HEAD_DIM)
    heads = [acc[:, h, :, h, :] for h in range(N_HEADS)]
    return jnp.stack(heads, axis=2).reshape(n_seq, t_new, GROUP_W)


def _mem_scores(mq_ref, mkt_of_head):
    return [_dot((mq_ref[:, _head(h)] * ATT_SCALE).astype(BF16), mkt_of_head(h).astype(BF16))
            for h in range(N_HEADS)]


def _mem_values(s, mvt_h):
    e = jnp.exp(s - jnp.max(s, axis=-1, keepdims=True))
    return _dot_nt(e.astype(BF16), mvt_h.astype(BF16)) / jnp.sum(e, axis=-1, keepdims=True)


def _out_part(y, gate, group, gn_ref, wout_ref):
    m = _rms(y, gn_ref[group:group + 1, :]) * _silu(gate)
    return _dot(m.astype(BF16), wout_ref[group * GROUP_W:(group + 1) * GROUP_W, :])


def _conv_taps(v, vm1, vm2, b_gate, cw_ref):
    return b_gate * (vm2 * cw_ref[0:1, :] + vm1 * cw_ref[1:2, :] + v * cw_ref[2:3, :])


def _merge_prompt_kernel(tm, final,
                         x_ref, ag_ref, bb_ref, bc_ref, bx_ref, bg_ref, cg_ref, mq_ref, mg_ref,
                         hbc_ref, hbx_ref, buf_ref, ya_ref, yc_ref, mkt_ref, mvt_ref,
                         gn_ref, cw_ref, wout_ref, fg_ref,
                         o_ref, tail_ref):
    i = pl.program_id(1)
    scores = _mem_scores(mq_ref, lambda h: mkt_ref[_head(h), :])
    part = _out_part(ya_ref[...], ag_ref[...], 0, gn_ref, wout_ref)
    part = part + _out_part(yc_ref[...], cg_ref[...], 2, gn_ref, wout_ref)

    v = bc_ref[...] * bx_ref[...]
    halo = hbc_ref[...] * hbx_ref[...]
    first = i == 0
    prev1 = jnp.where(first, buf_ref[1:2, :], halo[7:8, :])
    prev2 = jnp.where(first, buf_ref[0:1, :], halo[6:7, :])
    row = lax.broadcasted_iota(jnp.int32, (tm, 1), 0)
    vm1 = jnp.where(row == 0, prev1, pltpu.roll(v, 1, 0))
    vm2 = jnp.where(row == 0, prev2, jnp.where(row == 1, prev1, pltpu.roll(v, 2, 0)))
    y_b = _conv_taps(v, vm1, vm2, bb_ref[...], cw_ref)
    tail_ref[...] = v[tm - 8:tm, :]
    part = part + _out_part(y_b, bg_ref[...], 1, gn_ref, wout_ref)

    y_m = jnp.concatenate([_mem_values(scores[h], mvt_ref[_head(h), :]) for h in range(N_HEADS)],
                          axis=-1)
    part = part + _out_part(y_m, mg_ref[...], 3, gn_ref, wout_ref)
    x_new = x_ref[...] + part
    if final:
        x_new = _rms(x_new, fg_ref[...])
    o_ref[...] = x_new


def merge_prompt(x2d, h2d, y_a, y_c, conv_buf, mkv_t, gn, conv_w, wout_bf16, final_g,
                 n_seq, t_len, final):
    d = x2d.shape[1]
    n_mem = mkv_t.shape[2]
    tm = min(t_len, 512)
    nb = t_len // tm
    col = PROMPT_COL
    hblk = lambda c: pl.BlockSpec((tm, GROUP_W), lambda n, i, c=c: (n * nb + i, col[c]))
    halo = lambda c: pl.BlockSpec(
        (8, GROUP_W), lambda n, i, c=c: (jnp.maximum((n * nb + i) * (tm // 8) - 1, 0), col[c]))
    rows = pl.BlockSpec((tm, GROUP_W), lambda n, i: (n * nb + i, 0))
    full = lambda shape: pl.BlockSpec(shape, lambda n, i: (0,) * len(shape))
    out, tail = pl.pallas_call(
        functools.partial(_merge_prompt_kernel, tm, final),
        grid=(n_seq, nb),
        in_specs=[pl.BlockSpec((tm, d), lambda n, i: (n * nb + i, 0)),
                  hblk(A_G), hblk(B_B), hblk(B_C), hblk(B_X), hblk(B_G), hblk(C_G),
                  hblk(M_Q), hblk(M_G), halo(B_C), halo(B_X),
                  pl.BlockSpec((None, CONV_K - 1, GROUP_W), lambda n, i: (n, 0, 0)),
                  rows, rows,
                  pl.BlockSpec((None, GROUP_W, n_mem), lambda n, i: (n, 0, 0)),
                  pl.BlockSpec((None, GROUP_W, n_mem), lambda n, i: (n, 1, 0)),
                  full((4, GROUP_W)), full((CONV_K, GROUP_W)), full(wout_bf16.shape),
                  full((1, d))],
        out_specs=[pl.BlockSpec((tm, d), lambda n, i: (n * nb + i, 0)),
                   pl.BlockSpec((None, 8, GROUP_W), lambda n, i: (n, 0, 0))],
        out_shape=[jax.ShapeDtypeStruct(x2d.shape, F32),
                   jax.ShapeDtypeStruct((n_seq, 8, GROUP_W), F32)],
        compiler_params=_cparams("parallel", "arbitrary"),
        name="merge_prompt",
    )(x2d, h2d, h2d, h2d, h2d, h2d, h2d, h2d, h2d, h2d, h2d, conv_buf, y_a, y_c, mkv_t, mkv_t,
      gn, conv_w, wout_bf16, final_g.reshape(1, d))
    return out, tail[:, 8 - (CONV_K - 1):, :]


def _merge_sample_kernel(t_len, seqs, final,
                         x_ref, ag_ref, bb_ref, bc_ref, bx_ref, bg_ref, cg_ref, mq_ref, mg_ref,
                         buf_ref, ya_ref, yc_ref, mkt_ref, mvt_ref,
                         gn_ref, cw_ref, wout_ref, fg_ref,
                         o_ref, v_ref):
    tm = seqs * t_len
    row = lax.broadcasted_iota(jnp.int32, (tm, 1), 0)
    seq_of_row = row // t_len
    t_of_row = row % t_len

    v = bc_ref[...] * bx_ref[...]
    v_ref[...] = v
    prev1 = jnp.zeros((tm, GROUP_W), F32)
    prev2 = jnp.zeros((tm, GROUP_W), F32)
    for s in range(seqs):
        prev1 = jnp.where(seq_of_row == s, buf_ref[s, 1:2, :], prev1)
        prev2 = jnp.where(seq_of_row == s, buf_ref[s, 0:1, :], prev2)
    vm1 = jnp.where(t_of_row == 0, prev1, pltpu.roll(v, 1, 0))
    vm2 = jnp.where(t_of_row == 0, prev2, jnp.where(t_of_row == 1, prev1, pltpu.roll(v, 2, 0)))
    y_b = _conv_taps(v, vm1, vm2, bb_ref[...], cw_ref)

    heads = [jnp.zeros((tm, HEAD_DIM), F32)] * N_HEADS
    for s in range(seqs):
        scores = _mem_scores(mq_ref, lambda h, s=s: mkt_ref[s, _head(h), :])
        for h in range(N_HEADS):
            o = _mem_values(scores[h], mvt_ref[s, _head(h), :])
            heads[h] = jnp.where(seq_of_row == s, o, heads[h])
    y_m = jnp.concatenate(heads, axis=-1)

    part = _out_part(ya_ref[...], ag_ref[...], 0, gn_ref, wout_ref)
    part = part + _out_part(y_b, bg_ref[...], 1, gn_ref, wout_ref)
    part = part + _out_part(yc_ref[...], cg_ref[...], 2, gn_ref, wout_ref)
    part = part + _out_part(y_m, mg_ref[...], 3, gn_ref, wout_ref)
    x_new = x_ref[...] + part
    o_ref[...] = _rms(x_new, fg_ref[...]) if final else x_new


def merge_sample(x2d, h2d, y_a, y_c, state_conv, cache_mkt, cache_mvt, gn, conv_w, wout_bf16,
                 final_g, n_seq, t_len, layer, final):
    d = x2d.shape[1]
    n_mem = cache_mkt.shape[3]
    seqs = 16 // t_len
    tm = seqs * t_len
    hblk = lambda c: pl.BlockSpec((tm, GROUP_W), lambda i, c=c: (i, c))
    rows = pl.BlockSpec((tm, GROUP_W), lambda i: (i, 0))
    full = lambda shape: pl.BlockSpec(shape, lambda i: (0,) * len(shape))
    mem = pl.BlockSpec((seqs, None, GROUP_W, n_mem), lambda i: (i, layer, 0, 0))
    return pl.pallas_call(
        functools.partial(_merge_sample_kernel, t_len, seqs, final),
        grid=(n_seq // seqs,),
        in_specs=[pl.BlockSpec((tm, d), lambda i: (i, 0)),
                  hblk(A_G), hblk(B_B), hblk(B_C), hblk(B_X), hblk(B_G), hblk(C_G),
                  hblk(M_Q), hblk(M_G),
                  pl.BlockSpec((seqs, None, CONV_K - 1, GROUP_W), lambda i: (i, layer, 0, 0)),
                  rows, rows, mem, mem,
                  full((4, GROUP_W)), full((CONV_K, GROUP_W)), full(wout_bf16.shape),
                  full((1, d))],
        out_specs=[pl.BlockSpec((tm, d), lambda i: (i, 0)), rows],
        out_shape=[jax.ShapeDtypeStruct(x2d.shape, F32),
                   jax.ShapeDtypeStruct((x2d.shape[0], GROUP_W), F32)],
        compiler_params=_cparams("parallel"),
        name="merge_sample",
    )(x2d, h2d, h2d, h2d, h2d, h2d, h2d, h2d, h2d, state_conv, y_a, y_c, cache_mkt, cache_mvt,
      gn, conv_w, wout_bf16, final_g.reshape(1, d))


def _time_major(a, n_seq, t_len):
    return a.reshape(n_seq, t_len, -1).transpose(1, 0, 2).reshape(n_seq * t_len, -1)


def _seq_major(a, n_seq, t_len):
    return a.reshape(t_len, n_seq, -1).transpose(1, 0, 2).reshape(n_seq * t_len, -1)


def _col(h2d, c):
    return h2d[:, c * GROUP_W:(c + 1) * GROUP_W]


def _pos_minor(a):
    lead = a.shape[:-3]
    pos, heads, dim = a.shape[-3:]
    nd = len(lead)
    return a.transpose(*range(nd), nd + 1, nd + 2, nd).reshape(*lead, heads * dim, pos)


def _from_pos_minor(a_t):
    lead = a_t.shape[:-2]
    pos = a_t.shape[-1]
    nd = len(lead)
    a = a_t.reshape(*lead, N_HEADS, HEAD_DIM, pos)
    return a.transpose(*range(nd), nd + 2, nd, nd + 1)


def kernel(x_prompt, x_sample, cache_sb_k, cache_sb_v, state_ssm_re, state_ssm_im, state_conv,
           cache_mem_k, cache_mem_v, page_table, mem_prompt, norm_g, w_in, w_out, group_norm_g,
           ssm_lambda_re, ssm_lambda_im, ssm_b_re, ssm_b_im, ssm_c_re, ssm_c_im, ssm_log_dt, ssm_d,
           ssm_w_glu, conv_w, sb_bias, w_mem_kv, final_norm_g):
    n_p, l_p, d = x_prompt.shape
    n_s, l_s, _ = x_sample.shape
    depth = w_in.shape[0]
    assert l_s >= CONV_K - 1 and 16 % l_s == 0 and n_p % 8 == 0 and n_s % 8 == 0
    tq = min(l_p, 256)

    xp = x_prompt.reshape(n_p * l_p, d)
    xs = x_sample.reshape(n_s * l_s, d)
    cache_kt = _pos_minor(cache_sb_k)
    cache_vt = _pos_minor(cache_sb_v)
    cache_mkt = _pos_minor(cache_mem_k)
    cache_mvt = _pos_minor(cache_mem_v)
    zeros_h = jnp.zeros((n_p, SSM_W), F32)
    zeros_buf = jnp.zeros((n_p, CONV_K - 1, GROUP_W), F32)

    p_kt, p_vt, p_re, p_im, p_conv, p_mkvt = [], [], [], [], [], []
    s_k, s_v, s_re, s_im, s_conv = [], [], [], [], []
    for i in range(depth):
        final = i == depth - 1
        w_in_bf = w_in[i].astype(BF16)
        w_out_bf = w_out[i].astype(BF16)

        lbre, lbim, bbre, bbim = ssm_discretise(ssm_lambda_re[i], ssm_lambda_im[i], ssm_log_dt[i],
                                                ssm_b_re[i], ssm_b_im[i])
        sp = {"lbre": lbre, "lbim": lbim,
              "bre": _block_diag_in(bbre).astype(BF16), "bim": _block_diag_in(bbim).astype(BF16),
              "cre": _block_diag_out(ssm_c_re[i]).astype(BF16),
              "cimn": _block_diag_out(-ssm_c_im[i]).astype(BF16),
              "d": ssm_d[i].reshape(1, GROUP_W), "wglu": ssm_w_glu[i].astype(BF16)}

        mkv_t = mem_kv_t(mem_prompt, w_mem_kv[i])
        h_p, k_rm, q_t, v_tb, k_t, v_t = in_proj_prompt(xp, norm_g[i], w_in[i], n_p, l_p, tq)
        y_a, h_re, h_im = ssm_scan(h_p.reshape(n_p, l_p, -1), PROMPT_COL[A_U], zeros_h, zeros_h,
                                   sp, n_p, l_p, True)
        y_a = y_a.reshape(n_p * l_p, GROUP_W)
        y_c = sb_prompt(q_t, k_rm, v_tb, sb_bias[i], n_p, l_p, tq)
        xp, conv_p = merge_prompt(xp, h_p, y_a, y_c, zeros_buf, mkv_t, group_norm_g[i], conv_w[i],
                                  w_out_bf, final_norm_g, n_p, l_p, final)
        p_kt.append(k_t)
        p_vt.append(v_t)
        p_re.append(h_re.reshape(n_p, SSM_GROUPS, SSM_STATE))
        p_im.append(h_im.reshape(n_p, SSM_GROUPS, SSM_STATE))
        p_conv.append(conv_p)
        p_mkvt.append(mkv_t)

        h_s = in_proj(xs, norm_g[i], w_in_bf)
        ya_tm, h_re, h_im = ssm_scan(_time_major(_col(h_s, A_U), n_s, l_s), 0,
                                     state_ssm_re[:, i].reshape(n_s, SSM_W),
                                     state_ssm_im[:, i].reshape(n_s, SSM_W), sp, n_s, l_s, False)
        y_a = _seq_major(ya_tm, n_s, l_s)
        seq3 = lambda c: _col(h_s, c).reshape(n_s, l_s, GROUP_W)
        y_c = sb_sample(seq3(C_Q), seq3(C_K), seq3(C_V), sb_bias[i], cache_kt, cache_vt,
                        page_table, i).reshape(n_s * l_s, GROUP_W)
        xs, v_conv = merge_sample(xs, h_s, y_a, y_c, state_conv, cache_mkt, cache_mvt,
                                  group_norm_g[i], conv_w[i], w_out_bf, final_norm_g,
                                  n_s, l_s, i, final)
        s_k.append(seq3(C_K).reshape(n_s, l_s, N_HEADS, HEAD_DIM))
        s_v.append(seq3(C_V).reshape(n_s, l_s, N_HEADS, HEAD_DIM))
        s_re.append(h_re.reshape(n_s, SSM_GROUPS, SSM_STATE))
        s_im.append(h_im.reshape(n_s, SSM_GROUPS, SSM_STATE))
        s_conv.append(v_conv.reshape(n_s, l_s, GROUP_W)[:, l_s - (CONV_K - 1):, :])

    stack = lambda xs_: jnp.stack(xs_, axis=1)
    mkv = stack(p_mkvt)
    return (xp.reshape(n_p, l_p, d), xs.reshape(n_s, l_s, d),
            _from_pos_minor(stack(p_kt)), _from_pos_minor(stack(p_vt)),
            stack(p_re), stack(p_im), stack(p_conv),
            _from_pos_minor(mkv[:, :, :GROUP_W]), _from_pos_minor(mkv[:, :, GROUP_W:]),
            stack(s_k), stack(s_v), stack(s_re), stack(s_im), stack(s_conv))
```

```python
import functools

import jax
import jax.numpy as jnp
from jax import lax
from jax.experimental import pallas as pl
from jax.experimental.pallas import tpu as pltpu

F32 = jnp.float32
BF16 = jnp.bfloat16

EPS = 1e-6
GROUP_W = 256
N_IN_BLOCKS = 12
HEAD_DIM = 64
N_HEADS = 4
ATT_SCALE = HEAD_DIM ** -0.5
LOG2E = 1.4426950408889634
SSM_GROUPS = 16
SSM_CH = 16
SSM_STATE = 64
SSM_W = SSM_GROUPS * SSM_STATE
CONV_K = 3
LANES = 128

(A_U, A_G, B_B, B_C, B_X, B_G, C_Q, C_K, C_V, C_G, M_Q, M_G) = range(N_IN_BLOCKS)
PROMPT_BLOCKS = (A_U, A_G, B_B, B_C, B_X, B_G, C_G, M_Q, M_G)
PROMPT_COL = {b: j for j, b in enumerate(PROMPT_BLOCKS)}
SAMPLE_COL = {b: b for b in range(N_IN_BLOCKS)}

VMEM_LIMIT = 48 * 1024 * 1024
PAGES_PER_STEP = 32


def _cparams(*sem):
    return pltpu.CompilerParams(dimension_semantics=sem, vmem_limit_bytes=VMEM_LIMIT)


def _dot(a, b):
    return jnp.dot(a, b, preferred_element_type=F32)


def _dot_nt(a, b):
    return lax.dot_general(a, b, (((1,), (1,)), ((), ())), preferred_element_type=F32)


def _rms(x, g):
    return x * lax.rsqrt(jnp.mean(x * x, axis=-1, keepdims=True) + EPS) * g


def _silu(x):
    return x * jax.nn.sigmoid(x)


def _head(h):
    return slice(h * HEAD_DIM, (h + 1) * HEAD_DIM)


def _inproj_kernel(x_ref, g_ref, w_ref, o_ref):
    xn = _rms(x_ref[...], g_ref[...])
    o_ref[...] = _dot(xn.astype(BF16), w_ref[...])


def in_proj(x2d, g, w_bf16):
    rows, d = x2d.shape
    c = w_bf16.shape[1]
    tm = min(rows, 256)
    return pl.pallas_call(
        _inproj_kernel,
        grid=(rows // tm,),
        in_specs=[pl.BlockSpec((tm, d), lambda i: (i, 0)),
                  pl.BlockSpec((1, d), lambda i: (0, 0)),
                  pl.BlockSpec((d, c), lambda i: (0, 0))],
        out_specs=pl.BlockSpec((tm, c), lambda i: (i, 0)),
        out_shape=jax.ShapeDtypeStruct((rows, c), F32),
        compiler_params=_cparams("parallel"),
        name="in_proj",
    )(x2d, g.reshape(1, d), w_bf16)


def _inproj_prompt_kernel(tq, x_ref, g_ref, w_ref, *refs):
    h_ref, krm_ref, qt_ref, vtb_ref, kt_ref, vt_ref = refs[-6:]
    blk = lambda b: slice(b * GROUP_W, (b + 1) * GROUP_W)
    xn = _rms(x_ref[...], g_ref[...]).astype(BF16)
    hm = _dot(xn, w_ref[...])
    h_ref[...] = jnp.concatenate([hm[:, blk(b)] for b in PROMPT_BLOCKS], axis=1)
    k = hm[:, blk(C_K)]
    krm_ref[...] = k.astype(BF16)
    q = hm[:, blk(C_Q)] * (ATT_SCALE * LOG2E)
    v = hm[:, blk(C_V)]
    slots = range(kt_ref.shape[0]) if len(kt_ref.shape) == 3 else (None,)
    for j in range(x_ref.shape[0] // tq):
        rows = slice(j * tq, (j + 1) * tq)
        k_t = k[rows].T
        v_t = v[rows].T
        for slot in slots:
            idx = (slice(None), rows) if slot is None else (slot, slice(None), rows)
            kt_ref[idx] = k_t
            vt_ref[idx] = v_t
        qt_ref[j] = q[rows].T.astype(BF16)
        vtb_ref[j] = v_t.astype(BF16)


def in_proj_prompt(x2d, g, w_bf16, n_seq, t_len, tq, layer, depth, kv_t):
    rows, d = x2d.shape
    per = 2 if t_len % (2 * tq) == 0 else 1
    tm = per * tq
    nb = t_len // tm
    n_main = len(PROMPT_BLOCKS) * GROUP_W
    full = lambda shape: pl.BlockSpec(shape, lambda n, i: (0,) * len(shape))
    blocked = pl.BlockSpec((None, per, GROUP_W, tq), lambda n, i: (n, i, 0, 0))
    prior = () if kv_t is None else tuple(kv_t)
    if prior:
        final_t = pl.BlockSpec((None, None, GROUP_W, tm), lambda n, i: (n, layer, 0, i))
    else:
        final_t = pl.BlockSpec((None, depth, GROUP_W, tm), lambda n, i: (n, 0, 0, i))
    kv_shape = jax.ShapeDtypeStruct((n_seq, depth, GROUP_W, t_len), F32)
    return pl.pallas_call(
        functools.partial(_inproj_prompt_kernel, tq),
        grid=(n_seq, nb),
        in_specs=[pl.BlockSpec((tm, d), lambda n, i: (n * nb + i, 0)),
                  full((1, d)), full(w_bf16.shape)]
        + [pl.BlockSpec(memory_space=pl.ANY)] * len(prior),
        out_specs=[pl.BlockSpec((tm, n_main), lambda n, i: (n * nb + i, 0)),
                   pl.BlockSpec((tm, GROUP_W), lambda n, i: (n * nb + i, 0)),
                   blocked, blocked, final_t, final_t],
        out_shape=[jax.ShapeDtypeStruct((rows, n_main), F32),
                   jax.ShapeDtypeStruct((rows, GROUP_W), BF16),
                   jax.ShapeDtypeStruct((n_seq, t_len // tq, GROUP_W, tq), BF16),
                   jax.ShapeDtypeStruct((n_seq, t_len // tq, GROUP_W, tq), BF16),
                   kv_shape, kv_shape],
        input_output_aliases={3: 4, 4: 5} if prior else {},
        compiler_params=_cparams("parallel", "parallel"),
        name="in_proj_prompt",
    )(x2d, g.reshape(1, d), w_bf16, *prior)


def _memkv_kernel(x_ref, wt_ref, o_ref):
    o_ref[...] = _dot_nt(wt_ref[...], x_ref[...].astype(BF16))


def mem_kv_t(mem, w_mem):
    n_seq, n_mem, d = mem.shape
    w_t = w_mem.T.astype(BF16)
    return pl.pallas_call(
        _memkv_kernel,
        grid=(n_seq,),
        in_specs=[pl.BlockSpec((None, n_mem, d), lambda n: (n, 0, 0)),
                  pl.BlockSpec(w_t.shape, lambda n: (0, 0))],
        out_specs=pl.BlockSpec((None, w_t.shape[0], n_mem), lambda n: (n, 0, 0)),
        out_shape=jax.ShapeDtypeStruct((n_seq, w_t.shape[0], n_mem), F32),
        compiler_params=_cparams("parallel"),
        name="mem_kv",
    )(mem, w_t)


def _ssm_disc_kernel(lre_ref, lim_ref, dt_ref, bre_ref, bim_ref,
                     lbre_ref, lbim_ref, bbre_ref, bbim_ref):
    lre = lre_ref[...]
    lim = lim_ref[...]
    dt = jnp.exp(dt_ref[...])
    mag = jnp.exp(lre * dt)
    lbre = mag * jnp.cos(lim * dt)
    lbim = mag * jnp.sin(lim * dt)
    lbre_ref[...] = lbre
    lbim_ref[...] = lbim
    nre = lbre - 1.0
    nim = lbim
    den = lre * lre + lim * lim
    cre = (nre * lre + nim * lim) / den
    cim = (nim * lre - nre * lim) / den
    bre = bre_ref[...]
    bim = bim_ref[...]
    bbre_ref[...] = cre * bre - cim * bim
    bbim_ref[...] = cre * bim + cim * bre


def ssm_discretise(lam_re, lam_im, log_dt, b_re, b_im):
    col = lambda a: a.reshape(SSM_W, 1)
    dt_col = jnp.broadcast_to(log_dt[:, None], (SSM_GROUPS, SSM_STATE)).reshape(SSM_W, 1)
    outs = pl.pallas_call(
        _ssm_disc_kernel,
        out_shape=[jax.ShapeDtypeStruct((SSM_W, 1), F32)] * 2
        + [jax.ShapeDtypeStruct((SSM_W, SSM_CH), F32)] * 2,
        name="ssm_discretise",
    )(col(lam_re), col(lam_im), dt_col, b_re.reshape(SSM_W, SSM_CH), b_im.reshape(SSM_W, SSM_CH))
    lbre, lbim, bbre, bbim = outs
    shp = (SSM_GROUPS, SSM_STATE, SSM_CH)
    return lbre.reshape(1, SSM_W), lbim.reshape(1, SSM_W), bbre.reshape(shp), bbim.reshape(shp)


def _block_diag_in(b_gpc):
    eye = jnp.eye(SSM_GROUPS, dtype=F32)
    m = b_gpc.transpose(0, 2, 1)[:, :, None, :] * eye[:, None, :, None]
    return m.reshape(SSM_GROUPS * SSM_CH, SSM_W)


def _block_diag_out(c_gcp):
    eye = jnp.eye(SSM_GROUPS, dtype=F32)
    m = c_gcp.transpose(0, 2, 1)[:, :, None, :] * eye[:, None, :, None]
    return m.reshape(SSM_W, SSM_GROUPS * SSM_CH)


def _ssm_pitch(t_chunk):
    p = -(-t_chunk // 8)
    return 8 * (p if p % 2 else p + 1)


def _ssm_kernel(n_seq, t_chunk, seq_major,
                u_ref, h0re_ref, h0im_ref, lre_ref, lim_ref,
                bre_ref, bim_ref,
                cre_ref, cimn_ref, d_ref, wglu_ref,
                y_ref, hre_ref, him_ref,
                bu_re, bu_im, hs_re, hs_im):
    step = pl.program_id(0)
    rows = n_seq * t_chunk

    @pl.when(step == 0)
    def _():
        hre_ref[...] = h0re_ref[...]
        him_ref[...] = h0im_ref[...]

    n_slab = SSM_W // LANES
    slab = lambda j: slice(j * LANES, (j + 1) * LANES)
    pitch = _ssm_pitch(t_chunk) if seq_major else None

    def put(ref, j, val):
        if not seq_major:
            ref[j] = val
            return
        for n in range(n_seq):
            ref[j, n * pitch:n * pitch + t_chunk, :] = val[n * t_chunk:(n + 1) * t_chunk, :]

    def get(ref, j):
        if not seq_major:
            return ref[j]
        return jnp.concatenate(
            [ref[j, n * pitch:n * pitch + t_chunk, :] for n in range(n_seq)], axis=0)

    def rows_of(t):
        if seq_major:
            return pl.ds(t, n_seq, stride=pitch)
        return pl.ds(pl.multiple_of(t * n_seq, n_seq), n_seq)

    u = u_ref[...].reshape(rows, GROUP_W)
    u_bf = u.astype(BF16)
    b_re = _dot(u_bf, bre_ref[...])
    b_im = _dot(u_bf, bim_ref[...])
    for j in range(n_slab):
        put(bu_re, j, b_re[:, slab(j)])
        put(bu_im, j, b_im[:, slab(j)])

    lre = [jnp.broadcast_to(lre_ref[:, slab(j)], (n_seq, LANES)) for j in range(n_slab)]
    lim = [jnp.broadcast_to(lim_ref[:, slab(j)], (n_seq, LANES)) for j in range(n_slab)]

    def body(t, carry):
        r = rows_of(t)
        new = []
        for j in range(n_slab):
            hr, hi = carry[j]
            nr = lre[j] * hr - lim[j] * hi + bu_re[j, r, :]
            ni = lre[j] * hi + lim[j] * hr + bu_im[j, r, :]
            hs_re[j, r, :] = nr
            hs_im[j, r, :] = ni
            new.append((nr, ni))
        return tuple(new)

    h0 = tuple((hre_ref[:, slab(j)], him_ref[:, slab(j)]) for j in range(n_slab))
    h_last = lax.fori_loop(0, t_chunk, body, h0)
    for j in range(n_slab):
        hre_ref[:, slab(j)] = h_last[j][0]
        him_ref[:, slab(j)] = h_last[j][1]

    h_re = jnp.concatenate([get(hs_re, j) for j in range(n_slab)], axis=-1)
    h_im = jnp.concatenate([get(hs_im, j) for j in range(n_slab)], axis=-1)
    y = (_dot(h_re.astype(BF16), cre_ref[...])
         + _dot(h_im.astype(BF16), cimn_ref[...])
         + d_ref[...] * u)
    y = jax.nn.gelu(y)
    y = y * jax.nn.sigmoid(_dot(y.astype(BF16), wglu_ref[...]))
    y_ref[...] = y.reshape(y_ref.shape)


def ssm_scan(u, u_col, h0_re, h0_im, sp, n_seq, t_len, seq_major):
    t_chunk = min(t_len, 128)
    tr = t_chunk * n_seq
    s_rows = n_seq * _ssm_pitch(t_chunk) if seq_major else tr
    full = lambda shape: pl.BlockSpec(shape, lambda i: (0,) * len(shape))
    if seq_major:
        u_spec = pl.BlockSpec((n_seq, t_chunk, GROUP_W), lambda i: (0, i, u_col))
        y_spec = pl.BlockSpec((n_seq, t_chunk, GROUP_W), lambda i: (0, i, 0))
        y_shape = jax.ShapeDtypeStruct((n_seq, t_len, GROUP_W), F32)
    else:
        u_spec = pl.BlockSpec((tr, GROUP_W), lambda i: (i, u_col))
        y_spec = pl.BlockSpec((tr, GROUP_W), lambda i: (i, 0))
        y_shape = jax.ShapeDtypeStruct((t_len * n_seq, GROUP_W), F32)
    return pl.pallas_call(
        functools.partial(_ssm_kernel, n_seq, t_chunk, seq_major),
        grid=(t_len // t_chunk,),
        in_specs=[u_spec,
                  full((n_seq, SSM_W)), full((n_seq, SSM_W)),
                  full((1, SSM_W)), full((1, SSM_W)),
                  full((GROUP_W, SSM_W)), full((GROUP_W, SSM_W)),
                  full((SSM_W, GROUP_W)), full((SSM_W, GROUP_W)),
                  full((1, GROUP_W)), full((GROUP_W, GROUP_W))],
        out_specs=[y_spec, full((n_seq, SSM_W)), full((n_seq, SSM_W))],
        out_shape=[y_shape,
                   jax.ShapeDtypeStruct((n_seq, SSM_W), F32),
                   jax.ShapeDtypeStruct((n_seq, SSM_W), F32)],
        scratch_shapes=[pltpu.VMEM((SSM_W // LANES, s_rows, LANES), F32)] * 4,
        compiler_params=_cparams("arbitrary"),
        name="ssm_scan",
    )(u, h0_re, h0_im, sp["lbre"], sp["lbim"],
      sp["bre"], sp["bim"],
      sp["cre"], sp["cimn"], sp["d"], sp["wglu"])


def _softplus(z):
    return jnp.maximum(z, 0.0) + jnp.log(1.0 + jnp.exp(-jnp.abs(z)))


def _softplus2(z2):
    return jnp.maximum(z2, 0.0) + jnp.log(1.0 + jnp.exp2(-jnp.abs(z2))) * LOG2E


def _sb_prompt_kernel(tq, bias_ref, qt_ref, k_ref, vt_ref, o_ref,
                      z_ref, sp_ref, d_ref, w_ref, acc_ref):
    i = pl.program_id(1)
    s_idx = lax.broadcasted_iota(jnp.int32, (tq, tq), 0)
    t_idx = lax.broadcasted_iota(jnp.int32, (tq, tq), 1)
    neg_suffix = jnp.where(t_idx > s_idx, -1.0, 0.0).astype(BF16)
    diag_mask = s_idx < t_idx
    heads = range(N_HEADS)

    def block_of(t):
        return jnp.maximum(i - t, 0)

    def scores(t):
        rows = pl.ds(pl.multiple_of(block_of(t) * tq, tq), tq)
        for h in heads:
            z_ref[t % 2, h] = _dot(k_ref[rows, _head(h)], qt_ref[_head(h), :])

    def front(t, mask):
        for h in heads:
            z = z_ref[t % 2, h] + bias_ref[h] * LOG2E
            sp = _softplus2(z)
            d_ref[t % 2, h] = z - sp
            sp_m = sp if mask is None else jnp.where(mask, sp, 0.0)
            sp_ref[t % 2, h] = sp_m.astype(BF16)

    def suffix_sums(t):
        return [_dot(neg_suffix, sp_ref[t % 2, h]) for h in heads]

    def weights(t, local, carries, mask):
        new_carries = []
        for h in heads:
            w = jnp.exp2(d_ref[t % 2, h] + (local[h] + carries[h]))
            if mask is not None:
                w = jnp.where(mask, w, 0.0)
            w_ref[h] = w.astype(BF16)
            new_carries.append(
                carries[h] + (local[h][0:1, :] - sp_ref[t % 2, h, 0:1, :].astype(F32)))
        return tuple(new_carries)

    def last(t):
        kb = block_of(t)
        for h in heads:
            acc_ref[_head(h), :] += _dot(vt_ref[kb, _head(h), :], w_ref[h])

    acc_ref[...] = jnp.zeros(acc_ref.shape, F32)
    scores(0)
    scores(1)
    front(0, diag_mask)

    def trip(t, carries, mask):
        local = suffix_sums(t)
        front(t + 1, None)
        scores(t + 2)
        carries = weights(t, local, carries, mask)
        last(t)
        return carries

    zero = jnp.zeros((1, tq), F32)
    carries = trip(0, (zero,) * N_HEADS, diag_mask)
    carries = lax.fori_loop(1, i, lambda t, c: trip(t, c, None), carries)

    @pl.when(i >= 1)
    def _():
        weights(i, suffix_sums(i), carries, None)
        last(i)

    o_ref[...] = acc_ref[...].T


def sb_prompt(q_t, k_rm, v_t, bias, n_seq, t_len, tq):
    nb = t_len // tq
    return pl.pallas_call(
        functools.partial(_sb_prompt_kernel, tq),
        grid=(n_seq, nb),
        in_specs=[pl.BlockSpec(memory_space=pltpu.SMEM),
                  pl.BlockSpec((None, None, GROUP_W, tq), lambda n, i: (n, i, 0, 0)),
                  pl.BlockSpec((t_len, GROUP_W), lambda n, i: (n, 0)),
                  pl.BlockSpec((None, nb, GROUP_W, tq), lambda n, i: (n, 0, 0, 0))],
        out_specs=pl.BlockSpec((tq, GROUP_W), lambda n, i: (n * nb + i, 0)),
        out_shape=jax.ShapeDtypeStruct((n_seq * t_len, GROUP_W), F32),
        scratch_shapes=[pltpu.VMEM((2, N_HEADS, tq, tq), F32),
                        pltpu.VMEM((2, N_HEADS, tq, tq), BF16),
                        pltpu.VMEM((2, N_HEADS, tq, tq), F32),
                        pltpu.VMEM((N_HEADS, tq, tq), BF16),
                        pltpu.VMEM((GROUP_W, tq), F32)],
        compiler_params=_cparams("parallel", "arbitrary"),
        name="sb_prompt",
    )(bias, q_t, k_rm, v_t)


def _sb_rows_block(z, mask, carry, neg_suffix):
    sp = _softplus(z)
    sp_m = sp if mask is None else jnp.where(mask, sp, 0.0)
    sp_bf = sp_m.astype(BF16)
    local = _dot(sp_bf, neg_suffix)
    w = jnp.exp((z - sp) + (local + carry))
    if mask is not None:
        w = jnp.where(mask, w, 0.0)
    return w, local[:, 0:1] - sp_bf[:, 0:1].astype(F32)


def _sb_sample_kernel(t_new, page, n_pages, layer,
                      pt_ref, bias_ref, qrep_ref, knew_ref, vnew_ref, ck_hbm, cv_hbm,
                      o_ref, kbuf, vbuf, sem, qbd_ref, acc_ref, carry_ref):
    n = pl.program_id(0)
    j = pl.program_id(1)
    n_steps = pl.num_programs(1)
    step = n * n_steps + j
    slot = step % 2
    rows_q = N_HEADS * t_new
    g_pages = PAGES_PER_STEP

    def page_copies(seq, jj, slot_):
        copies = []
        for g in range(g_pages):
            phys = pt_ref[seq, n_pages - 1 - (jj * g_pages + g)]
            copies.append(pltpu.make_async_copy(ck_hbm.at[phys, layer], kbuf.at[slot_, g],
                                                sem.at[0, slot_]))
            copies.append(pltpu.make_async_copy(cv_hbm.at[phys, layer], vbuf.at[slot_, g],
                                                sem.at[1, slot_]))
        return copies

    @pl.when(step == 0)
    def _():
        for c in page_copies(n, j, slot):
            c.start()

    nxt = step + 1

    @pl.when(nxt < pl.num_programs(0) * n_steps)
    def _():
        for c in page_copies(nxt // n_steps, nxt % n_steps, 1 - slot):
            c.start()

    jk = lax.broadcasted_iota(jnp.int32, (page, page), 0)
    sk = lax.broadcasted_iota(jnp.int32, (page, page), 1)
    neg_suffix = jnp.where(jk > sk, -1.0, 0.0).astype(BF16)
    row_head = lax.broadcasted_iota(jnp.int32, (rows_q, 1), 0) // t_new
    bias_col = jnp.zeros((rows_q, 1), F32)
    for h in range(N_HEADS):
        bias_col = jnp.where(row_head == h, bias_ref[h], bias_col)

    @pl.when(j == 0)
    def _():
        lane_head = lax.broadcasted_iota(jnp.int32, (rows_q, GROUP_W), 1) // HEAD_DIM
        qbd = jnp.where(lane_head == row_head, qrep_ref[...] * ATT_SCALE, 0.0).astype(BF16)
        qbd_ref[...] = qbd
        t_of_row = lax.broadcasted_iota(jnp.int32, (rows_q, page), 0) % t_new
        key = lax.broadcasted_iota(jnp.int32, (rows_q, page), 1)
        z = _dot_nt(qbd, knew_ref[...].astype(BF16)) + bias_col
        w, total = _sb_rows_block(z, key < t_of_row, jnp.zeros((rows_q, 1), F32), neg_suffix)
        acc_ref[...] = _dot(w.astype(BF16), vnew_ref[...].astype(BF16))
        carry_ref[...] = total

    for c in page_copies(n, j, slot):
        c.wait()

    qbd = qbd_ref[...]
    z = jnp.concatenate(
        [_dot(qbd, kbuf[slot, g].astype(BF16)) for g in range(g_pages)], axis=0) + jnp.concatenate(
        [bias_col] * g_pages, axis=0)
    sp = _softplus(z)
    sp_bf = sp.astype(BF16)
    local = _dot(sp_bf, neg_suffix)
    totals = local[:, 0:1] - sp_bf[:, 0:1].astype(F32)
    carry = carry_ref[...]
    carries = []
    for g in range(g_pages):
        carries.append(carry)
        carry = carry + totals[g * rows_q:(g + 1) * rows_q, :]
    carry_ref[...] = carry
    w = jnp.exp((z - sp) + (local + jnp.concatenate(carries, axis=0))).astype(BF16)
    acc = acc_ref[...]
    for g in range(g_pages):
        acc = acc + _dot_nt(w[g * rows_q:(g + 1) * rows_q, :], vbuf[slot, g].astype(BF16))
    acc_ref[...] = acc

    @pl.when(j == n_steps - 1)
    def _():
        o_ref[...] = acc


def sb_sample(q, k_new, v_new, bias, cache_kt, cache_vt, page_table, layer):
    n_seq, t_new, _ = q.shape
    page = cache_kt.shape[3]
    n_pages = page_table.shape[1]
    assert n_pages % PAGES_PER_STEP == 0
    rows_q = N_HEADS * t_new
    q_rep = jnp.tile(q, (1, N_HEADS, 1))
    pad = ((0, 0), (0, page - t_new), (0, 0))
    k_pad = jnp.pad(k_new, pad)
    v_pad = jnp.pad(v_new, pad)
    per_seq = lambda r: pl.BlockSpec((None, r, GROUP_W), lambda n, j, pt: (n, 0, 0))
    acc = pl.pallas_call(
        functools.partial(_sb_sample_kernel, t_new, page, n_pages, layer),
        grid_spec=pltpu.PrefetchScalarGridSpec(
            num_scalar_prefetch=1,
            grid=(n_seq, n_pages // PAGES_PER_STEP),
            in_specs=[pl.BlockSpec(memory_space=pltpu.SMEM),
                      per_seq(rows_q), per_seq(page), per_seq(page),
                      pl.BlockSpec(memory_space=pl.ANY),
                      pl.BlockSpec(memory_space=pl.ANY)],
            out_specs=per_seq(rows_q),
            scratch_shapes=[pltpu.VMEM((2, PAGES_PER_STEP, GROUP_W, page), F32),
                            pltpu.VMEM((2, PAGES_PER_STEP, GROUP_W, page), F32),
                            pltpu.SemaphoreType.DMA((2, 2)),
                            pltpu.VMEM((rows_q, GROUP_W), BF16),
                            pltpu.VMEM((rows_q, GROUP_W), F32),
                            pltpu.VMEM((rows_q, 1), F32)]),
        out_shape=jax.ShapeDtypeStruct((n_seq, rows_q, GROUP_W), F32),
        compiler_params=_cparams("arbitrary", "arbitrary"),
        name="sb_sample",
    )(page_table, bias, q_rep, k_pad, v_pad, cache_kt, cache_vt)
    acc = acc.reshape(n_seq, N_HEADS, t_new, N_HEADS, HEAD_DIM)
    heads = [acc[:, h, :, h, :] for h in range(N_HEADS)]
    return jnp.stack(heads, axis=2).reshape(n_seq, t_new, GROUP_W)


def _mem_scores(mq_ref, mkt_of_head):
    return [_dot((mq_ref[:, _head(h)] * ATT_SCALE).astype(BF16), mkt_of_head(h).astype(BF16))
            for h in range(N_HEADS)]


def _mem_values(s, mvt_h):
    e = jnp.exp(s - jnp.max(s, axis=-1, keepdims=True))
    return _dot_nt(e.astype(BF16), mvt_h.astype(BF16)) / jnp.sum(e, axis=-1, keepdims=True)


def _out_part(y, gate, group, gn_ref, wout_ref):
    m = _rms(y, gn_ref[group:group + 1, :]) * _silu(gate)
    return _dot(m.astype(BF16), wout_ref[group * GROUP_W:(group + 1) * GROUP_W, :])


def _conv_taps(v, vm1, vm2, b_gate, cw_ref):
    return b_gate * (vm2 * cw_ref[0:1, :] + vm1 * cw_ref[1:2, :] + v * cw_ref[2:3, :])


def _merge_prompt_kernel(tm, final,
                         x_ref, ag_ref, bb_ref, bc_ref, bx_ref, bg_ref, cg_ref, mq_ref, mg_ref,
                         hbc_ref, hbx_ref, buf_ref, ya_ref, yc_ref, mkt_ref, mvt_ref,
                         gn_ref, cw_ref, wout_ref, fg_ref,
                         o_ref, tail_ref):
    i = pl.program_id(1)
    scores = _mem_scores(mq_ref, lambda h: mkt_ref[_head(h), :])
    part = _out_part(ya_ref[...], ag_ref[...], 0, gn_ref, wout_ref)
    part = part + _out_part(yc_ref[...], cg_ref[...], 2, gn_ref, wout_ref)

    v = bc_ref[...] * bx_ref[...]
    halo = hbc_ref[...] * hbx_ref[...]
    first = i == 0
    prev1 = jnp.where(first, buf_ref[1:2, :], halo[7:8, :])
    prev2 = jnp.where(first, buf_ref[0:1, :], halo[6:7, :])
    row = lax.broadcasted_iota(jnp.int32, (tm, 1), 0)
    vm1 = jnp.where(row == 0, prev1, pltpu.roll(v, 1, 0))
    vm2 = jnp.where(row == 0, prev2, jnp.where(row == 1, prev1, pltpu.roll(v, 2, 0)))
    y_b = _conv_taps(v, vm1, vm2, bb_ref[...], cw_ref)
    tail_ref[...] = v[tm - 8:tm, :]
    part = part + _out_part(y_b, bg_ref[...], 1, gn_ref, wout_ref)

    y_m = jnp.concatenate([_mem_values(scores[h], mvt_ref[_head(h), :]) for h in range(N_HEADS)],
                          axis=-1)
    part = part + _out_part(y_m, mg_ref[...], 3, gn_ref, wout_ref)
    x_new = x_ref[...] + part
    if final:
        x_new = _rms(x_new, fg_ref[...])
    o_ref[...] = x_new


def merge_prompt(x2d, h2d, y_a, y_c, conv_buf, mkv_t, gn, conv_w, wout_bf16, final_g,
                 n_seq, t_len, final):
    d = x2d.shape[1]
    n_mem = mkv_t.shape[2]
    tm = min(t_len, 512)
    nb = t_len // tm
    col = PROMPT_COL
    hblk = lambda c: pl.BlockSpec((tm, GROUP_W), lambda n, i, c=c: (n * nb + i, col[c]))
    halo = lambda c: pl.BlockSpec(
        (8, GROUP_W), lambda n, i, c=c: (jnp.maximum((n * nb + i) * (tm // 8) - 1, 0), col[c]))
    rows = pl.BlockSpec((tm, GROUP_W), lambda n, i: (n * nb + i, 0))
    full = lambda shape: pl.BlockSpec(shape, lambda n, i: (0,) * len(shape))
    out, tail = pl.pallas_call(
        functools.partial(_merge_prompt_kernel, tm, final),
        grid=(n_seq, nb),
        in_specs=[pl.BlockSpec((tm, d), lambda n, i: (n * nb + i, 0)),
                  hblk(A_G), hblk(B_B), hblk(B_C), hblk(B_X), hblk(B_G), hblk(C_G),
                  hblk(M_Q), hblk(M_G), halo(B_C), halo(B_X),
                  pl.BlockSpec((None, CONV_K - 1, GROUP_W), lambda n, i: (n, 0, 0)),
                  rows, rows,
                  pl.BlockSpec((None, GROUP_W, n_mem), lambda n, i: (n, 0, 0)),
                  pl.BlockSpec((None, GROUP_W, n_mem), lambda n, i: (n, 1, 0)),
                  full((4, GROUP_W)), full((CONV_K, GROUP_W)), full(wout_bf16.shape),
                  full((1, d))],
        out_specs=[pl.BlockSpec((tm, d), lambda n, i: (n * nb + i, 0)),
                   pl.BlockSpec((None, 8, GROUP_W), lambda n, i: (n, 0, 0))],
        out_shape=[jax.ShapeDtypeStruct(x2d.shape, F32),
                   jax.ShapeDtypeStruct((n_seq, 8, GROUP_W), F32)],
        compiler_params=_cparams("parallel", "arbitrary"),
        name="merge_prompt",
    )(x2d, h2d, h2d, h2d, h2d, h2d, h2d, h2d, h2d, h2d, h2d, conv_buf, y_a, y_c, mkv_t, mkv_t,
      gn, conv_w, wout_bf16, final_g.reshape(1, d))
    return out, tail[:, 8 - (CONV_K - 1):, :]


def _merge_sample_kernel(t_len, seqs, final,
                         x_ref, ag_ref, bb_ref, bc_ref, bx_ref, bg_ref, cg_ref, mq_ref, mg_ref,
                         buf_ref, ya_ref, yc_ref, mkt_ref, mvt_ref,
                         gn_ref, cw_ref, wout_ref, fg_ref,
                         o_ref, v_ref):
    tm = seqs * t_len
    row = lax.broadcasted_iota(jnp.int32, (tm, 1), 0)
    seq_of_row = row // t_len
    t_of_row = row % t_len

    v = bc_ref[...] * bx_ref[...]
    v_ref[...] = v
    prev1 = jnp.zeros((tm, GROUP_W), F32)
    prev2 = jnp.zeros((tm, GROUP_W), F32)
    for s in range(seqs):
        prev1 = jnp.where(seq_of_row == s, buf_ref[s, 1:2, :], prev1)
        prev2 = jnp.where(seq_of_row == s, buf_ref[s, 0:1, :], prev2)
    vm1 = jnp.where(t_of_row == 0, prev1, pltpu.roll(v, 1, 0))
    vm2 = jnp.where(t_of_row == 0, prev2, jnp.where(t_of_row == 1, prev1, pltpu.roll(v, 2, 0)))
    y_b = _conv_taps(v, vm1, vm2, bb_ref[...], cw_ref)

    heads = [jnp.zeros((tm, HEAD_DIM), F32)] * N_HEADS
    for s in range(seqs):
        scores = _mem_scores(mq_ref, lambda h, s=s: mkt_ref[s, _head(h), :])
        for h in range(N_HEADS):
            o = _mem_values(scores[h], mvt_ref[s, _head(h), :])
            heads[h] = jnp.where(seq_of_row == s, o, heads[h])
    y_m = jnp.concatenate(heads, axis=-1)

    part = _out_part(ya_ref[...], ag_ref[...], 0, gn_ref, wout_ref)
    part = part + _out_part(y_b, bg_ref[...], 1, gn_ref, wout_ref)
    part = part + _out_part(yc_ref[...], cg_ref[...], 2, gn_ref, wout_ref)
    part = part + _out_part(y_m, mg_ref[...], 3, gn_ref, wout_ref)
    x_new = x_ref[...] + part
    o_ref[...] = _rms(x_new, fg_ref[...]) if final else x_new


def merge_sample(x2d, h2d, y_a, y_c, state_conv, cache_mkt, cache_mvt, gn, conv_w, wout_bf16,
                 final_g, n_seq, t_len, layer, final):
    d = x2d.shape[1]
    n_mem = cache_mkt.shape[3]
    seqs = 16 // t_len
    tm = seqs * t_len
    hblk = lambda c: pl.BlockSpec((tm, GROUP_W), lambda i, c=c: (i, c))
    rows = pl.BlockSpec((tm, GROUP_W), lambda i: (i, 0))
    full = lambda shape: pl.BlockSpec(shape, lambda i: (0,) * len(shape))
    mem = pl.BlockSpec((seqs, None, GROUP_W, n_mem), lambda i: (i, layer, 0, 0))
    return pl.pallas_call(
        functools.partial(_merge_sample_kernel, t_len, seqs, final),
        grid=(n_seq // seqs,),
        in_specs=[pl.BlockSpec((tm, d), lambda i: (i, 0)),
                  hblk(A_G), hblk(B_B), hblk(B_C), hblk(B_X), hblk(B_G), hblk(C_G),
                  hblk(M_Q), hblk(M_G),
                  pl.BlockSpec((seqs, None, CONV_K - 1, GROUP_W), lambda i: (i, layer, 0, 0)),
                  rows, rows, mem, mem,
                  full((4, GROUP_W)), full((CONV_K, GROUP_W)), full(wout_bf16.shape),
                  full((1, d))],
        out_specs=[pl.BlockSpec((tm, d), lambda i: (i, 0)), rows],
        out_shape=[jax.ShapeDtypeStruct(x2d.shape, F32),
                   jax.ShapeDtypeStruct((x2d.shape[0], GROUP_W), F32)],
        compiler_params=_cparams("parallel"),
        name="merge_sample",
    )(x2d, h2d, h2d, h2d, h2d, h2d, h2d, h2d, h2d, state_conv, y_a, y_c, cache_mkt, cache_mvt,
      gn, conv_w, wout_bf16, final_g.reshape(1, d))


def _time_major(a, n_seq, t_len):
    return a.reshape(n_seq, t_len, -1).transpose(1, 0, 2).reshape(n_seq * t_len, -1)


def _seq_major(a, n_seq, t_len):
    return a.reshape(t_len, n_seq, -1).transpose(1, 0, 2).reshape(n_seq * t_len, -1)


def _col(h2d, c):
    return h2d[:, c * GROUP_W:(c + 1) * GROUP_W]


def _pos_minor(a):
    lead = a.shape[:-3]
    pos, heads, dim = a.shape[-3:]
    nd = len(lead)
    return a.transpose(*range(nd), nd + 1, nd + 2, nd).reshape(*lead, heads * dim, pos)


def _from_pos_minor(a_t):
    lead = a_t.shape[:-2]
    pos = a_t.shape[-1]
    nd = len(lead)
    a = a_t.reshape(*lead, N_HEADS, HEAD_DIM, pos)
    return a.transpose(*range(nd), nd + 2, nd, nd + 1)


def kernel(x_prompt, x_sample, cache_sb_k, cache_sb_v, state_ssm_re, state_ssm_im, state_conv,
           cache_mem_k, cache_mem_v, page_table, mem_prompt, norm_g, w_in, w_out, group_norm_g,
           ssm_lambda_re, ssm_lambda_im, ssm_b_re, ssm_b_im, ssm_c_re, ssm_c_im, ssm_log_dt, ssm_d,
           ssm_w_glu, conv_w, sb_bias, w_mem_kv, final_norm_g):
    n_p, l_p, d = x_prompt.shape
    n_s, l_s, _ = x_sample.shape
    depth = w_in.shape[0]
    assert l_s >= CONV_K - 1 and 16 % l_s == 0 and n_p % 8 == 0 and n_s % 8 == 0
    tq = min(l_p, 256)

    xp = x_prompt.reshape(n_p * l_p, d)
    xs = x_sample.reshape(n_s * l_s, d)
    cache_kt = _pos_minor(cache_sb_k)
    cache_vt = _pos_minor(cache_sb_v)
    cache_mkt = _pos_minor(cache_mem_k)
    cache_mvt = _pos_minor(cache_mem_v)
    zeros_h = jnp.zeros((n_p, SSM_W), F32)
    zeros_buf = jnp.zeros((n_p, CONV_K - 1, GROUP_W), F32)

    kv_t = None
    p_re, p_im, p_conv, p_mkvt = [], [], [], []
    s_k, s_v, s_re, s_im, s_conv = [], [], [], [], []
    for i in range(depth):
        final = i == depth - 1
        w_in_bf = w_in[i].astype(BF16)
        w_out_bf = w_out[i].astype(BF16)

        lbre, lbim, bbre, bbim = ssm_discretise(ssm_lambda_re[i], ssm_lambda_im[i], ssm_log_dt[i],
                                                ssm_b_re[i], ssm_b_im[i])
        sp = {"lbre": lbre, "lbim": lbim,
              "bre": _block_diag_in(bbre).astype(BF16), "bim": _block_diag_in(bbim).astype(BF16),
              "cre": _block_diag_out(ssm_c_re[i]).astype(BF16),
              "cimn": _block_diag_out(-ssm_c_im[i]).astype(BF16),
              "d": ssm_d[i].reshape(1, GROUP_W), "wglu": ssm_w_glu[i].astype(BF16)}

        mkv_t = mem_kv_t(mem_prompt, w_mem_kv[i])
        h_p, k_rm, q_t, v_tb, *kv_t = in_proj_prompt(xp, norm_g[i], w_in_bf, n_p, l_p, tq, i,
                                                     depth, kv_t)
        y_a, h_re, h_im = ssm_scan(h_p.reshape(n_p, l_p, -1), PROMPT_COL[A_U], zeros_h, zeros_h,
                                   sp, n_p, l_p, True)
        y_a = y_a.reshape(n_p * l_p, GROUP_W)
        y_c = sb_prompt(q_t, k_rm, v_tb, sb_bias[i], n_p, l_p, tq)
        xp, conv_p = merge_prompt(xp, h_p, y_a, y_c, zeros_buf, mkv_t, group_norm_g[i], conv_w[i],
                                  w_out_bf, final_norm_g, n_p, l_p, final)
        p_re.append(h_re.reshape(n_p, SSM_GROUPS, SSM_STATE))
        p_im.append(h_im.reshape(n_p, SSM_GROUPS, SSM_STATE))
        p_conv.append(conv_p)
        p_mkvt.append(mkv_t)

        h_s = in_proj(xs, norm_g[i], w_in_bf)
        ya_tm, h_re, h_im = ssm_scan(_time_major(_col(h_s, A_U), n_s, l_s), 0,
                                     state_ssm_re[:, i].reshape(n_s, SSM_W),
                                     state_ssm_im[:, i].reshape(n_s, SSM_W), sp, n_s, l_s, False)
        y_a = _seq_major(ya_tm, n_s, l_s)
        seq3 = lambda c: _col(h_s, c).reshape(n_s, l_s, GROUP_W)
        y_c = sb_sample(seq3(C_Q), seq3(C_K), seq3(C_V), sb_bias[i], cache_kt, cache_vt,
                        page_table, i).reshape(n_s * l_s, GROUP_W)
        xs, v_conv = merge_sample(xs, h_s, y_a, y_c, state_conv, cache_mkt, cache_mvt,
                                  group_norm_g[i], conv_w[i], w_out_bf, final_norm_g,
                                  n_s, l_s, i, final)
        s_k.append(seq3(C_K).reshape(n_s, l_s, N_HEADS, HEAD_DIM))
        s_v.append(seq3(C_V).reshape(n_s, l_s, N_HEADS, HEAD_DIM))
        s_re.append(h_re.reshape(n_s, SSM_GROUPS, SSM_STATE))
        s_im.append(h_im.reshape(n_s, SSM_GROUPS, SSM_STATE))
        s_conv.append(v_conv.reshape(n_s, l_s, GROUP_W)[:, l_s - (CONV_K - 1):, :])

    stack = lambda xs_: jnp.stack(xs_, axis=1)
    mkv = stack(p_mkvt)
    return (xp.reshape(n_p, l_p, d), xs.reshape(n_s, l_s, d),
            _from_pos_minor(kv_t[0]), _from_pos_minor(kv_t[1]),
            stack(p_re), stack(p_im), stack(p_conv),
            _from_pos_minor(mkv[:, :, :GROUP_W]), _from_pos_minor(mkv[:, :, GROUP_W:]),
            stack(s_k), stack(s_v), stack(s_re), stack(s_im), stack(s_conv))
```

```python
import functools

import jax
import jax.numpy as jnp
from jax import lax
from jax.experimental import pallas as pl
from jax.experimental.pallas import tpu as pltpu

F32 = jnp.float32
BF16 = jnp.bfloat16

EPS = 1e-6
GROUP_W = 256
N_IN_BLOCKS = 12
HEAD_DIM = 64
N_HEADS = 4
ATT_SCALE = HEAD_DIM ** -0.5
LOG2E = 1.4426950408889634
SSM_GROUPS = 16
SSM_CH = 16
SSM_STATE = 64
SSM_W = SSM_GROUPS * SSM_STATE
CONV_K = 3
LANES = 128

(A_U, A_G, B_B, B_C, B_X, B_G, C_Q, C_K, C_V, C_G, M_Q, M_G) = range(N_IN_BLOCKS)
PROMPT_BLOCKS = (A_U, A_G, B_B, B_C, B_X, B_G, C_G, M_Q, M_G)
PROMPT_COL = {b: j for j, b in enumerate(PROMPT_BLOCKS)}
SAMPLE_COL = {b: b for b in range(N_IN_BLOCKS)}

VMEM_LIMIT = 48 * 1024 * 1024
PAGES_PER_STEP = 32


def _cparams(*sem):
    return pltpu.CompilerParams(dimension_semantics=sem, vmem_limit_bytes=VMEM_LIMIT)


def _dot(a, b):
    return jnp.dot(a, b, preferred_element_type=F32)


def _dot_nt(a, b):
    return lax.dot_general(a, b, (((1,), (1,)), ((), ())), preferred_element_type=F32)


def _rms(x, g):
    return x * lax.rsqrt(jnp.mean(x * x, axis=-1, keepdims=True) + EPS) * g


def _silu(x):
    return x * jax.nn.sigmoid(x)


def _head(h):
    return slice(h * HEAD_DIM, (h + 1) * HEAD_DIM)


def _inproj_kernel(x_ref, g_ref, w_ref, o_ref):
    xn = _rms(x_ref[...], g_ref[...])
    o_ref[...] = _dot(xn.astype(BF16), w_ref[...])


def in_proj(x2d, g, w_bf16):
    rows, d = x2d.shape
    c = w_bf16.shape[1]
    tm = min(rows, 256)
    return pl.pallas_call(
        _inproj_kernel,
        grid=(rows // tm,),
        in_specs=[pl.BlockSpec((tm, d), lambda i: (i, 0)),
                  pl.BlockSpec((1, d), lambda i: (0, 0)),
                  pl.BlockSpec((d, c), lambda i: (0, 0))],
        out_specs=pl.BlockSpec((tm, c), lambda i: (i, 0)),
        out_shape=jax.ShapeDtypeStruct((rows, c), F32),
        compiler_params=_cparams("parallel"),
        name="in_proj",
    )(x2d, g.reshape(1, d), w_bf16)


def _inproj_prompt_kernel(tq, x_ref, g_ref, w_ref, *refs):
    h_ref, krm_ref, qt_ref, vtb_ref, kt_ref, vt_ref = refs[-6:]
    blk = lambda b: slice(b * GROUP_W, (b + 1) * GROUP_W)
    xn = _rms(x_ref[...], g_ref[...]).astype(BF16)
    hm = _dot(xn, w_ref[...])
    h_ref[...] = jnp.concatenate([hm[:, blk(b)] for b in PROMPT_BLOCKS], axis=1)
    k = hm[:, blk(C_K)]
    krm_ref[...] = k.astype(BF16)
    q = hm[:, blk(C_Q)] * (ATT_SCALE * LOG2E)
    v = hm[:, blk(C_V)]
    slots = range(kt_ref.shape[0]) if len(kt_ref.shape) == 3 else (None,)
    for j in range(x_ref.shape[0] // tq):
        rows = slice(j * tq, (j + 1) * tq)
        k_t = k[rows].T
        v_t = v[rows].T
        for slot in slots:
            idx = (slice(None), rows) if slot is None else (slot, slice(None), rows)
            kt_ref[idx] = k_t
            vt_ref[idx] = v_t
        qt_ref[j] = q[rows].T.astype(BF16)
        vtb_ref[j] = v_t.astype(BF16)


def in_proj_prompt(x2d, g, w_bf16, n_seq, t_len, tq, layer, depth, kv_t):
    rows, d = x2d.shape
    per = 2 if t_len % (2 * tq) == 0 else 1
    tm = per * tq
    nb = t_len // tm
    n_main = len(PROMPT_BLOCKS) * GROUP_W
    full = lambda shape: pl.BlockSpec(shape, lambda n, i: (0,) * len(shape))
    blocked = pl.BlockSpec((None, per, GROUP_W, tq), lambda n, i: (n, i, 0, 0))
    prior = () if kv_t is None else tuple(kv_t)
    if prior:
        final_t = pl.BlockSpec((None, None, GROUP_W, tm), lambda n, i: (n, layer, 0, i))
    else:
        final_t = pl.BlockSpec((None, depth, GROUP_W, tm), lambda n, i: (n, 0, 0, i))
    kv_shape = jax.ShapeDtypeStruct((n_seq, depth, GROUP_W, t_len), F32)
    return pl.pallas_call(
        functools.partial(_inproj_prompt_kernel, tq),
        grid=(n_seq, nb),
        in_specs=[pl.BlockSpec((tm, d), lambda n, i: (n * nb + i, 0)),
                  full((1, d)), full(w_bf16.shape)]
        + [pl.BlockSpec(memory_space=pl.ANY)] * len(prior),
        out_specs=[pl.BlockSpec((tm, n_main), lambda n, i: (n * nb + i, 0)),
                   pl.BlockSpec((tm, GROUP_W), lambda n, i: (n * nb + i, 0)),
                   blocked, blocked, final_t, final_t],
        out_shape=[jax.ShapeDtypeStruct((rows, n_main), F32),
                   jax.ShapeDtypeStruct((rows, GROUP_W), BF16),
                   jax.ShapeDtypeStruct((n_seq, t_len // tq, GROUP_W, tq), BF16),
                   jax.ShapeDtypeStruct((n_seq, t_len // tq, GROUP_W, tq), BF16),
                   kv_shape, kv_shape],
        input_output_aliases={3: 4, 4: 5} if prior else {},
        compiler_params=_cparams("parallel", "parallel"),
        name="in_proj_prompt",
    )(x2d, g.reshape(1, d), w_bf16, *prior)


def _memkv_kernel(x_ref, wt_ref, o_ref):
    o_ref[...] = _dot_nt(wt_ref[...], x_ref[...].astype(BF16))


def mem_kv_t(mem, w_mem):
    n_seq, n_mem, d = mem.shape
    w_t = w_mem.T.astype(BF16)
    return pl.pallas_call(
        _memkv_kernel,
        grid=(n_seq,),
        in_specs=[pl.BlockSpec((None, n_mem, d), lambda n: (n, 0, 0)),
                  pl.BlockSpec(w_t.shape, lambda n: (0, 0))],
        out_specs=pl.BlockSpec((None, w_t.shape[0], n_mem), lambda n: (n, 0, 0)),
        out_shape=jax.ShapeDtypeStruct((n_seq, w_t.shape[0], n_mem), F32),
        compiler_params=_cparams("parallel"),
        name="mem_kv",
    )(mem, w_t)


def _ssm_disc_kernel(lre_ref, lim_ref, dt_ref, bre_ref, bim_ref,
                     lbre_ref, lbim_ref, bbre_ref, bbim_ref):
    lre = lre_ref[...]
    lim = lim_ref[...]
    dt = jnp.exp(dt_ref[...])
    mag = jnp.exp(lre * dt)
    lbre = mag * jnp.cos(lim * dt)
    lbim = mag * jnp.sin(lim * dt)
    lbre_ref[...] = lbre
    lbim_ref[...] = lbim
    nre = lbre - 1.0
    nim = lbim
    den = lre * lre + lim * lim
    cre = (nre * lre + nim * lim) / den
    cim = (nim * lre - nre * lim) / den
    bre = bre_ref[...]
    bim = bim_ref[...]
    bbre_ref[...] = cre * bre - cim * bim
    bbim_ref[...] = cre * bim + cim * bre


def ssm_discretise(lam_re, lam_im, log_dt, b_re, b_im):
    col = lambda a: a.reshape(SSM_W, 1)
    dt_col = jnp.broadcast_to(log_dt[:, None], (SSM_GROUPS, SSM_STATE)).reshape(SSM_W, 1)
    outs = pl.pallas_call(
        _ssm_disc_kernel,
        out_shape=[jax.ShapeDtypeStruct((SSM_W, 1), F32)] * 2
        + [jax.ShapeDtypeStruct((SSM_W, SSM_CH), F32)] * 2,
        name="ssm_discretise",
    )(col(lam_re), col(lam_im), dt_col, b_re.reshape(SSM_W, SSM_CH), b_im.reshape(SSM_W, SSM_CH))
    lbre, lbim, bbre, bbim = outs
    shp = (SSM_GROUPS, SSM_STATE, SSM_CH)
    return lbre.reshape(1, SSM_W), lbim.reshape(1, SSM_W), bbre.reshape(shp), bbim.reshape(shp)


def _block_diag_in(b_gpc):
    eye = jnp.eye(SSM_GROUPS, dtype=F32)
    m = b_gpc.transpose(0, 2, 1)[:, :, None, :] * eye[:, None, :, None]
    return m.reshape(SSM_GROUPS * SSM_CH, SSM_W)


def _block_diag_out(c_gcp):
    eye = jnp.eye(SSM_GROUPS, dtype=F32)
    m = c_gcp.transpose(0, 2, 1)[:, :, None, :] * eye[:, None, :, None]
    return m.reshape(SSM_W, SSM_GROUPS * SSM_CH)


def _ssm_pitch(t_chunk):
    p = -(-t_chunk // 8)
    return 8 * (p if p % 2 else p + 1)


def _ssm_kernel(n_seq, t_chunk, seq_major,
                u_ref, h0re_ref, h0im_ref, lre_ref, lim_ref,
                bre_ref, bim_ref,
                cre_ref, cimn_ref, d_ref, wglu_ref,
                y_ref, hre_ref, him_ref,
                bu_re, bu_im, hs_re, hs_im):
    step = pl.program_id(0)
    rows = n_seq * t_chunk

    @pl.when(step == 0)
    def _():
        hre_ref[...] = h0re_ref[...]
        him_ref[...] = h0im_ref[...]

    n_slab = SSM_W // LANES
    slab = lambda j: slice(j * LANES, (j + 1) * LANES)
    pitch = _ssm_pitch(t_chunk) if seq_major else None

    def put(ref, j, val):
        if not seq_major:
            ref[j] = val
            return
        for n in range(n_seq):
            ref[j, n * pitch:n * pitch + t_chunk, :] = val[n * t_chunk:(n + 1) * t_chunk, :]

    def get(ref, j):
        if not seq_major:
            return ref[j]
        return jnp.concatenate(
            [ref[j, n * pitch:n * pitch + t_chunk, :] for n in range(n_seq)], axis=0)

    def rows_of(t):
        if seq_major:
            return pl.ds(t, n_seq, stride=pitch)
        return pl.ds(pl.multiple_of(t * n_seq, n_seq), n_seq)

    halves = [slice(0, rows // 2), slice(rows // 2, rows)] if rows % 16 == 0 else [slice(0, rows)]
    u = u_ref[...].reshape(rows, GROUP_W)
    u_bf = u.astype(BF16)
    b_re = jnp.concatenate([_dot(u_bf[r], bre_ref[...]) for r in halves], axis=0)
    b_im = jnp.concatenate([_dot(u_bf[r], bim_ref[...]) for r in halves], axis=0)
    for j in range(n_slab):
        put(bu_re, j, b_re[:, slab(j)])
        put(bu_im, j, b_im[:, slab(j)])

    lre = [jnp.broadcast_to(lre_ref[:, slab(j)], (n_seq, LANES)) for j in range(n_slab)]
    lim = [jnp.broadcast_to(lim_ref[:, slab(j)], (n_seq, LANES)) for j in range(n_slab)]

    def body(t, carry):
        r = rows_of(t)
        new = []
        for j in range(n_slab):
            hr, hi = carry[j]
            nr = lre[j] * hr - lim[j] * hi + bu_re[j, r, :]
            ni = lre[j] * hi + lim[j] * hr + bu_im[j, r, :]
            hs_re[j, r, :] = nr
            hs_im[j, r, :] = ni
            new.append((nr, ni))
        return tuple(new)

    h0 = tuple((hre_ref[:, slab(j)], him_ref[:, slab(j)]) for j in range(n_slab))
    h_last = lax.fori_loop(0, t_chunk, body, h0, unroll=4)
    for j in range(n_slab):
        hre_ref[:, slab(j)] = h_last[j][0]
        him_ref[:, slab(j)] = h_last[j][1]

    h_re = jnp.concatenate([get(hs_re, j) for j in range(n_slab)], axis=-1).astype(BF16)
    h_im = jnp.concatenate([get(hs_im, j) for j in range(n_slab)], axis=-1).astype(BF16)
    ch = jnp.concatenate([_dot(h_re[r], cre_ref[...]) + _dot(h_im[r], cimn_ref[...])
                          for r in halves], axis=0)
    y = ch + d_ref[...] * u
    y = jax.nn.gelu(y)
    y = y * jax.nn.sigmoid(_dot(y.astype(BF16), wglu_ref[...]))
    y_ref[...] = y.reshape(y_ref.shape)


def ssm_scan(u, u_col, h0_re, h0_im, sp, n_seq, t_len, seq_major):
    t_chunk = min(t_len, 128)
    tr = t_chunk * n_seq
    s_rows = n_seq * _ssm_pitch(t_chunk) if seq_major else tr
    full = lambda shape: pl.BlockSpec(shape, lambda i: (0,) * len(shape))
    if seq_major:
        u_spec = pl.BlockSpec((n_seq, t_chunk, GROUP_W), lambda i: (0, i, u_col))
        y_spec = pl.BlockSpec((n_seq, t_chunk, GROUP_W), lambda i: (0, i, 0))
        y_shape = jax.ShapeDtypeStruct((n_seq, t_len, GROUP_W), F32)
    else:
        u_spec = pl.BlockSpec((tr, GROUP_W), lambda i: (i, u_col))
        y_spec = pl.BlockSpec((tr, GROUP_W), lambda i: (i, 0))
        y_shape = jax.ShapeDtypeStruct((t_len * n_seq, GROUP_W), F32)
    return pl.pallas_call(
        functools.partial(_ssm_kernel, n_seq, t_chunk, seq_major),
        grid=(t_len // t_chunk,),
        in_specs=[u_spec,
                  full((n_seq, SSM_W)), full((n_seq, SSM_W)),
                  full((1, SSM_W)), full((1, SSM_W)),
                  full((GROUP_W, SSM_W)), full((GROUP_W, SSM_W)),
                  full((SSM_W, GROUP_W)), full((SSM_W, GROUP_W)),
                  full((1, GROUP_W)), full((GROUP_W, GROUP_W))],
        out_specs=[y_spec, full((n_seq, SSM_W)), full((n_seq, SSM_W))],
        out_shape=[y_shape,
                   jax.ShapeDtypeStruct((n_seq, SSM_W), F32),
                   jax.ShapeDtypeStruct((n_seq, SSM_W), F32)],
        scratch_shapes=[pltpu.VMEM((SSM_W // LANES, s_rows, LANES), F32)] * 4,
        compiler_params=_cparams("arbitrary"),
        name="ssm_scan",
    )(u, h0_re, h0_im, sp["lbre"], sp["lbim"],
      sp["bre"], sp["bim"],
      sp["cre"], sp["cimn"], sp["d"], sp["wglu"])


def _softplus(z):
    return jnp.maximum(z, 0.0) + jnp.log(1.0 + jnp.exp(-jnp.abs(z)))


def _softplus2(z2):
    return jnp.maximum(z2, 0.0) + jnp.log(1.0 + jnp.exp2(-jnp.abs(z2))) * LOG2E


def _sb_prompt_kernel(tq, bias_ref, qt_ref, k_ref, vt_ref, o_ref,
                      z_ref, sp_ref, d_ref, w_ref, acc_ref):
    i = pl.program_id(1)
    s_idx = lax.broadcasted_iota(jnp.int32, (tq, tq), 0)
    t_idx = lax.broadcasted_iota(jnp.int32, (tq, tq), 1)
    neg_suffix = jnp.where(t_idx > s_idx, -1.0, 0.0).astype(BF16)
    diag_mask = s_idx < t_idx
    heads = range(N_HEADS)

    def block_of(t):
        return jnp.maximum(i - t, 0)

    def scores(t):
        rows = pl.ds(pl.multiple_of(block_of(t) * tq, tq), tq)
        for h in heads:
            z_ref[t % 2, h] = _dot(k_ref[rows, _head(h)], qt_ref[_head(h), :])

    def front(t, mask):
        for h in heads:
            z = z_ref[t % 2, h] + bias_ref[h] * LOG2E
            sp = _softplus2(z)
            d_ref[t % 2, h] = z - sp
            sp_m = sp if mask is None else jnp.where(mask, sp, 0.0)
            sp_ref[t % 2, h] = sp_m.astype(BF16)

    def suffix_sums(t):
        return [_dot(neg_suffix, sp_ref[t % 2, h]) for h in heads]

    def weights(t, local, carries, mask):
        new_carries = []
        for h in heads:
            w = jnp.exp2(d_ref[t % 2, h] + (local[h] + carries[h]))
            if mask is not None:
                w = jnp.where(mask, w, 0.0)
            w_ref[h] = w.astype(BF16)
            new_carries.append(
                carries[h] + (local[h][0:1, :] - sp_ref[t % 2, h, 0:1, :].astype(F32)))
        return tuple(new_carries)

    def last(t):
        kb = block_of(t)
        for h in heads:
            acc_ref[_head(h), :] += _dot(vt_ref[kb, _head(h), :], w_ref[h])

    acc_ref[...] = jnp.zeros(acc_ref.shape, F32)
    scores(0)
    scores(1)
    front(0, diag_mask)

    def trip(t, carries, mask):
        local = suffix_sums(t)
        front(t + 1, None)
        scores(t + 2)
        carries = weights(t, local, carries, mask)
        last(t)
        return carries

    zero = jnp.zeros((1, tq), F32)
    carries = trip(0, (zero,) * N_HEADS, diag_mask)
    carries = lax.fori_loop(1, i, lambda t, c: trip(t, c, None), carries)

    @pl.when(i >= 1)
    def _():
        weights(i, suffix_sums(i), carries, None)
        last(i)

    o_ref[...] = acc_ref[...].T


def sb_prompt(q_t, k_rm, v_t, bias, n_seq, t_len, tq):
    nb = t_len // tq
    return pl.pallas_call(
        functools.partial(_sb_prompt_kernel, tq),
        grid=(n_seq, nb),
        in_specs=[pl.BlockSpec(memory_space=pltpu.SMEM),
                  pl.BlockSpec((None, None, GROUP_W, tq), lambda n, i: (n, i, 0, 0)),
                  pl.BlockSpec((t_len, GROUP_W), lambda n, i: (n, 0)),
                  pl.BlockSpec((None, nb, GROUP_W, tq), lambda n, i: (n, 0, 0, 0))],
        out_specs=pl.BlockSpec((tq, GROUP_W), lambda n, i: (n * nb + i, 0)),
        out_shape=jax.ShapeDtypeStruct((n_seq * t_len, GROUP_W), F32),
        scratch_shapes=[pltpu.VMEM((2, N_HEADS, tq, tq), F32),
                        pltpu.VMEM((2, N_HEADS, tq, tq), BF16),
                        pltpu.VMEM((2, N_HEADS, tq, tq), F32),
                        pltpu.VMEM((N_HEADS, tq, tq), BF16),
                        pltpu.VMEM((GROUP_W, tq), F32)],
        compiler_params=_cparams("parallel", "arbitrary"),
        name="sb_prompt",
    )(bias, q_t, k_rm, v_t)


def _sb_rows_block(z, mask, carry, neg_suffix):
    sp = _softplus(z)
    sp_m = sp if mask is None else jnp.where(mask, sp, 0.0)
    sp_bf = sp_m.astype(BF16)
    local = _dot(sp_bf, neg_suffix)
    w = jnp.exp((z - sp) + (local + carry))
    if mask is not None:
        w = jnp.where(mask, w, 0.0)
    return w, local[:, 0:1] - sp_bf[:, 0:1].astype(F32)


def _sb_sample_kernel(t_new, page, n_pages, layer,
                      pt_ref, bias_ref, qrep_ref, knew_ref, vnew_ref, ck_hbm, cv_hbm,
                      o_ref, kbuf, vbuf, sem, qbd_ref, acc_ref, carry_ref):
    n = pl.program_id(0)
    j = pl.program_id(1)
    n_steps = pl.num_programs(1)
    step = n * n_steps + j
    slot = step % 2
    rows_q = N_HEADS * t_new
    g_pages = PAGES_PER_STEP

    def page_copies(seq, jj, slot_):
        copies = []
        for g in range(g_pages):
            phys = pt_ref[seq, n_pages - 1 - (jj * g_pages + g)]
            copies.append(pltpu.make_async_copy(ck_hbm.at[phys, layer], kbuf.at[slot_, g],
                                                sem.at[0, slot_]))
            copies.append(pltpu.make_async_copy(cv_hbm.at[phys, layer], vbuf.at[slot_, g],
                                                sem.at[1, slot_]))
        return copies

    @pl.when(step == 0)
    def _():
        for c in page_copies(n, j, slot):
            c.start()

    nxt = step + 1

    @pl.when(nxt < pl.num_programs(0) * n_steps)
    def _():
        for c in page_copies(nxt // n_steps, nxt % n_steps, 1 - slot):
            c.start()

    jk = lax.broadcasted_iota(jnp.int32, (page, page), 0)
    sk = lax.broadcasted_iota(jnp.int32, (page, page), 1)
    neg_suffix = jnp.where(jk > sk, -1.0, 0.0).astype(BF16)
    row_head = lax.broadcasted_iota(jnp.int32, (rows_q, 1), 0) // t_new
    bias_col = jnp.zeros((rows_q, 1), F32)
    for h in range(N_HEADS):
        bias_col = jnp.where(row_head == h, bias_ref[h], bias_col)

    @pl.when(j == 0)
    def _():
        lane_head = lax.broadcasted_iota(jnp.int32, (rows_q, GROUP_W), 1) // HEAD_DIM
        qbd = jnp.where(lane_head == row_head, qrep_ref[...] * ATT_SCALE, 0.0).astype(BF16)
        qbd_ref[...] = qbd
        t_of_row = lax.broadcasted_iota(jnp.int32, (rows_q, page), 0) % t_new
        key = lax.broadcasted_iota(jnp.int32, (rows_q, page), 1)
        z = _dot_nt(qbd, knew_ref[...].astype(BF16)) + bias_col
        w, total = _sb_rows_block(z, key < t_of_row, jnp.zeros((rows_q, 1), F32), neg_suffix)
        acc_ref[...] = _dot(w.astype(BF16), vnew_ref[...].astype(BF16))
        carry_ref[...] = total

    for c in page_copies(n, j, slot):
        c.wait()

    qbd = qbd_ref[...]
    z = jnp.concatenate(
        [_dot(qbd, kbuf[slot, g].astype(BF16)) for g in range(g_pages)], axis=0) + jnp.concatenate(
        [bias_col] * g_pages, axis=0)
    sp = _softplus(z)
    sp_bf = sp.astype(BF16)
    local = _dot(sp_bf, neg_suffix)
    totals = local[:, 0:1] - sp_bf[:, 0:1].astype(F32)
    carry = carry_ref[...]
    carries = []
    for g in range(g_pages):
        carries.append(carry)
        carry = carry + totals[g * rows_q:(g + 1) * rows_q, :]
    carry_ref[...] = carry
    w = jnp.exp((z - sp) + (local + jnp.concatenate(carries, axis=0))).astype(BF16)
    acc = acc_ref[...]
    for g in range(g_pages):
        acc = acc + _dot_nt(w[g * rows_q:(g + 1) * rows_q, :], vbuf[slot, g].astype(BF16))
    acc_ref[...] = acc

    @pl.when(j == n_steps - 1)
    def _():
        o_ref[...] = acc


def sb_sample(q, k_new, v_new, bias, cache_kt, cache_vt, page_table, layer):
    n_seq, t_new, _ = q.shape
    page = cache_kt.shape[3]
    n_pages = page_table.shape[1]
    assert n_pages % PAGES_PER_STEP == 0
    rows_q = N_HEADS * t_new
    q_rep = jnp.tile(q, (1, N_HEADS, 1))
    pad = ((0, 0), (0, page - t_new), (0, 0))
    k_pad = jnp.pad(k_new, pad)
    v_pad = jnp.pad(v_new, pad)
    per_seq = lambda r: pl.BlockSpec((None, r, GROUP_W), lambda n, j, pt: (n, 0, 0))
    acc = pl.pallas_call(
        functools.partial(_sb_sample_kernel, t_new, page, n_pages, layer),
        grid_spec=pltpu.PrefetchScalarGridSpec(
            num_scalar_prefetch=1,
            grid=(n_seq, n_pages // PAGES_PER_STEP),
            in_specs=[pl.BlockSpec(memory_space=pltpu.SMEM),
                      per_seq(rows_q), per_seq(page), per_seq(page),
                      pl.BlockSpec(memory_space=pl.ANY),
                      pl.BlockSpec(memory_space=pl.ANY)],
            out_specs=per_seq(rows_q),
            scratch_shapes=[pltpu.VMEM((2, PAGES_PER_STEP, GROUP_W, page), F32),
                            pltpu.VMEM((2, PAGES_PER_STEP, GROUP_W, page), F32),
                            pltpu.SemaphoreType.DMA((2, 2)),
                            pltpu.VMEM((rows_q, GROUP_W), BF16),
                            pltpu.VMEM((rows_q, GROUP_W), F32),
                            pltpu.VMEM((rows_q, 1), F32)]),
        out_shape=jax.ShapeDtypeStruct((n_seq, rows_q, GROUP_W), F32),
        compiler_params=_cparams("arbitrary", "arbitrary"),
        name="sb_sample",
    )(page_table, bias, q_rep, k_pad, v_pad, cache_kt, cache_vt)
    acc = acc.reshape(n_seq, N_HEADS, t_new, N_HEADS, HEAD_DIM)
    heads = [acc[:, h, :, h, :] for h in range(N_HEADS)]
    return jnp.stack(heads, axis=2).reshape(n_seq, t_new, GROUP_W)


def _mem_scores(mq_ref, mkt_of_head):
    return [_dot((mq_ref[:, _head(h)] * ATT_SCALE).astype(BF16), mkt_of_head(h).astype(BF16))
            for h in range(N_HEADS)]


def _mem_values(s, mvt_h):
    e = jnp.exp(s - jnp.max(s, axis=-1, keepdims=True))
    return _dot_nt(e.astype(BF16), mvt_h.astype(BF16)) / jnp.sum(e, axis=-1, keepdims=True)


def _out_part(y, gate, group, gn_ref, wout_ref):
    m = _rms(y, gn_ref[group:group + 1, :]) * _silu(gate)
    return _dot(m.astype(BF16), wout_ref[group * GROUP_W:(group + 1) * GROUP_W, :])


def _conv_taps(v, vm1, vm2, b_gate, cw_ref):
    return b_gate * (vm2 * cw_ref[0:1, :] + vm1 * cw_ref[1:2, :] + v * cw_ref[2:3, :])


def _merge_prompt_kernel(tm, final,
                         x_ref, ag_ref, bb_ref, bc_ref, bx_ref, bg_ref, cg_ref, mq_ref, mg_ref,
                         hbc_ref, hbx_ref, buf_ref, ya_ref, yc_ref, mkt_ref, mvt_ref,
                         gn_ref, cw_ref, wout_ref, fg_ref,
                         o_ref, tail_ref):
    i = pl.program_id(1)
    scores = _mem_scores(mq_ref, lambda h: mkt_ref[_head(h), :])
    y_m = jnp.concatenate([_mem_values(scores[h], mvt_ref[_head(h), :]) for h in range(N_HEADS)],
                          axis=-1)
    part = _out_part(ya_ref[...], ag_ref[...], 0, gn_ref, wout_ref)
    part = part + _out_part(yc_ref[...], cg_ref[...], 2, gn_ref, wout_ref)

    v = bc_ref[...] * bx_ref[...]
    halo = hbc_ref[...] * hbx_ref[...]
    first = i == 0
    prev1 = jnp.where(first, buf_ref[1:2, :], halo[7:8, :])
    prev2 = jnp.where(first, buf_ref[0:1, :], halo[6:7, :])
    row = lax.broadcasted_iota(jnp.int32, (tm, 1), 0)
    vm1 = jnp.where(row == 0, prev1, pltpu.roll(v, 1, 0))
    vm2 = jnp.where(row == 0, prev2, jnp.where(row == 1, prev1, pltpu.roll(v, 2, 0)))
    y_b = _conv_taps(v, vm1, vm2, bb_ref[...], cw_ref)
    tail_ref[...] = v[tm - 8:tm, :]
    part = part + _out_part(y_b, bg_ref[...], 1, gn_ref, wout_ref)
    part = part + _out_part(y_m, mg_ref[...], 3, gn_ref, wout_ref)
    x_new = x_ref[...] + part
    if final:
        x_new = _rms(x_new, fg_ref[...])
    o_ref[...] = x_new


def merge_prompt(x2d, h2d, y_a, y_c, conv_buf, mkv_t, gn, conv_w, wout_bf16, final_g,
                 n_seq, t_len, final):
    d = x2d.shape[1]
    n_mem = mkv_t.shape[2]
    tm = min(t_len, 512)
    nb = t_len // tm
    col = PROMPT_COL
    hblk = lambda c: pl.BlockSpec((tm, GROUP_W), lambda n, i, c=c: (n * nb + i, col[c]))
    halo = lambda c: pl.BlockSpec(
        (8, GROUP_W), lambda n, i, c=c: (jnp.maximum((n * nb + i) * (tm // 8) - 1, 0), col[c]))
    rows = pl.BlockSpec((tm, GROUP_W), lambda n, i: (n * nb + i, 0))
    full = lambda shape: pl.BlockSpec(shape, lambda n, i: (0,) * len(shape))
    out, tail = pl.pallas_call(
        functools.partial(_merge_prompt_kernel, tm, final),
        grid=(n_seq, nb),
        in_specs=[pl.BlockSpec((tm, d), lambda n, i: (n * nb + i, 0)),
                  hblk(A_G), hblk(B_B), hblk(B_C), hblk(B_X), hblk(B_G), hblk(C_G),
                  hblk(M_Q), hblk(M_G), halo(B_C), halo(B_X),
                  pl.BlockSpec((None, CONV_K - 1, GROUP_W), lambda n, i: (n, 0, 0)),
                  rows, rows,
                  pl.BlockSpec((None, GROUP_W, n_mem), lambda n, i: (n, 0, 0)),
                  pl.BlockSpec((None, GROUP_W, n_mem), lambda n, i: (n, 1, 0)),
                  full((4, GROUP_W)), full((CONV_K, GROUP_W)), full(wout_bf16.shape),
                  full((1, d))],
        out_specs=[pl.BlockSpec((tm, d), lambda n, i: (n * nb + i, 0)),
                   pl.BlockSpec((None, 8, GROUP_W), lambda n, i: (n, 0, 0))],
        out_shape=[jax.ShapeDtypeStruct(x2d.shape, F32),
                   jax.ShapeDtypeStruct((n_seq, 8, GROUP_W), F32)],
        compiler_params=_cparams("parallel", "arbitrary"),
        name="merge_prompt",
    )(x2d, h2d, h2d, h2d, h2d, h2d, h2d, h2d, h2d, h2d, h2d, conv_buf, y_a, y_c, mkv_t, mkv_t,
      gn, conv_w, wout_bf16, final_g.reshape(1, d))
    return out, tail[:, 8 - (CONV_K - 1):, :]


def _merge_sample_kernel(t_len, seqs, final,
                         x_ref, ag_ref, bb_ref, bc_ref, bx_ref, bg_ref, cg_ref, mq_ref, mg_ref,
                         buf_ref, ya_ref, yc_ref, mkt_ref, mvt_ref,
                         gn_ref, cw_ref, wout_ref, fg_ref,
                         o_ref, v_ref):
    tm = seqs * t_len
    row = lax.broadcasted_iota(jnp.int32, (tm, 1), 0)
    seq_of_row = row // t_len
    t_of_row = row % t_len

    v = bc_ref[...] * bx_ref[...]
    v_ref[...] = v
    prev1 = jnp.zeros((tm, GROUP_W), F32)
    prev2 = jnp.zeros((tm, GROUP_W), F32)
    for s in range(seqs):
        prev1 = jnp.where(seq_of_row == s, buf_ref[s, 1:2, :], prev1)
        prev2 = jnp.where(seq_of_row == s, buf_ref[s, 0:1, :], prev2)
    vm1 = jnp.where(t_of_row == 0, prev1, pltpu.roll(v, 1, 0))
    vm2 = jnp.where(t_of_row == 0, prev2, jnp.where(t_of_row == 1, prev1, pltpu.roll(v, 2, 0)))
    y_b = _conv_taps(v, vm1, vm2, bb_ref[...], cw_ref)

    heads = [jnp.zeros((tm, HEAD_DIM), F32)] * N_HEADS
    for s in range(seqs):
        scores = _mem_scores(mq_ref, lambda h, s=s: mkt_ref[s, _head(h), :])
        for h in range(N_HEADS):
            o = _mem_values(scores[h], mvt_ref[s, _head(h), :])
            heads[h] = jnp.where(seq_of_row == s, o, heads[h])
    y_m = jnp.concatenate(heads, axis=-1)

    part = _out_part(ya_ref[...], ag_ref[...], 0, gn_ref, wout_ref)
    part = part + _out_part(y_b, bg_ref[...], 1, gn_ref, wout_ref)
    part = part + _out_part(yc_ref[...], cg_ref[...], 2, gn_ref, wout_ref)
    part = part + _out_part(y_m, mg_ref[...], 3, gn_ref, wout_ref)
    x_new = x_ref[...] + part
    o_ref[...] = _rms(x_new, fg_ref[...]) if final else x_new


def merge_sample(x2d, h2d, y_a, y_c, state_conv, cache_mkt, cache_mvt, gn, conv_w, wout_bf16,
                 final_g, n_seq, t_len, layer, final):
    d = x2d.shape[1]
    n_mem = cache_mkt.shape[3]
    seqs = 16 // t_len
    tm = seqs * t_len
    hblk = lambda c: pl.BlockSpec((tm, GROUP_W), lambda i, c=c: (i, c))
    rows = pl.BlockSpec((tm, GROUP_W), lambda i: (i, 0))
    full = lambda shape: pl.BlockSpec(shape, lambda i: (0,) * len(shape))
    mem = pl.BlockSpec((seqs, None, GROUP_W, n_mem), lambda i: (i, layer, 0, 0))
    return pl.pallas_call(
        functools.partial(_merge_sample_kernel, t_len, seqs, final),
        grid=(n_seq // seqs,),
        in_specs=[pl.BlockSpec((tm, d), lambda i: (i, 0)),
                  hblk(A_G), hblk(B_B), hblk(B_C), hblk(B_X), hblk(B_G), hblk(C_G),
                  hblk(M_Q), hblk(M_G),
                  pl.BlockSpec((seqs, None, CONV_K - 1, GROUP_W), lambda i: (i, layer, 0, 0)),
                  rows, rows, mem, mem,
                  full((4, GROUP_W)), full((CONV_K, GROUP_W)), full(wout_bf16.shape),
                  full((1, d))],
        out_specs=[pl.BlockSpec((tm, d), lambda i: (i, 0)), rows],
        out_shape=[jax.ShapeDtypeStruct(x2d.shape, F32),
                   jax.ShapeDtypeStruct((x2d.shape[0], GROUP_W), F32)],
        compiler_params=_cparams("parallel"),
        name="merge_sample",
    )(x2d, h2d, h2d, h2d, h2d, h2d, h2d, h2d, h2d, state_conv, y_a, y_c, cache_mkt, cache_mvt,
      gn, conv_w, wout_bf16, final_g.reshape(1, d))


def _time_major(a, n_seq, t_len):
    return a.reshape(n_seq, t_len, -1).transpose(1, 0, 2).reshape(n_seq * t_len, -1)


def _seq_major(a, n_seq, t_len):
    return a.reshape(t_len, n_seq, -1).transpose(1, 0, 2).reshape(n_seq * t_len, -1)


def _col(h2d, c):
    return h2d[:, c * GROUP_W:(c + 1) * GROUP_W]


def _pos_minor(a):
    lead = a.shape[:-3]
    pos, heads, dim = a.shape[-3:]
    nd = len(lead)
    return a.transpose(*range(nd), nd + 1, nd + 2, nd).reshape(*lead, heads * dim, pos)


def _from_pos_minor(a_t):
    lead = a_t.shape[:-2]
    pos = a_t.shape[-1]
    nd = len(lead)
    a = a_t.reshape(*lead, N_HEADS, HEAD_DIM, pos)
    return a.transpose(*range(nd), nd + 2, nd, nd + 1)


def kernel(x_prompt, x_sample, cache_sb_k, cache_sb_v, state_ssm_re, state_ssm_im, state_conv,
           cache_mem_k, cache_mem_v, page_table, mem_prompt, norm_g, w_in, w_out, group_norm_g,
           ssm_lambda_re, ssm_lambda_im, ssm_b_re, ssm_b_im, ssm_c_re, ssm_c_im, ssm_log_dt, ssm_d,
           ssm_w_glu, conv_w, sb_bias, w_mem_kv, final_norm_g):
    n_p, l_p, d = x_prompt.shape
    n_s, l_s, _ = x_sample.shape
    depth = w_in.shape[0]
    assert l_s >= CONV_K - 1 and 16 % l_s == 0 and n_p % 8 == 0 and n_s % 8 == 0
    tq = min(l_p, 256)

    xp = x_prompt.reshape(n_p * l_p, d)
    xs = x_sample.reshape(n_s * l_s, d)
    cache_kt = _pos_minor(cache_sb_k)
    cache_vt = _pos_minor(cache_sb_v)
    cache_mkt = _pos_minor(cache_mem_k)
    cache_mvt = _pos_minor(cache_mem_v)
    zeros_h = jnp.zeros((n_p, SSM_W), F32)
    zeros_buf = jnp.zeros((n_p, CONV_K - 1, GROUP_W), F32)

    kv_t = None
    p_re, p_im, p_conv, p_mkvt = [], [], [], []
    s_k, s_v, s_re, s_im, s_conv = [], [], [], [], []
    for i in range(depth):
        final = i == depth - 1
        w_in_bf = w_in[i].astype(BF16)
        w_out_bf = w_out[i].astype(BF16)

        lbre, lbim, bbre, bbim = ssm_discretise(ssm_lambda_re[i], ssm_lambda_im[i], ssm_log_dt[i],
                                                ssm_b_re[i], ssm_b_im[i])
        sp = {"lbre": lbre, "lbim": lbim,
              "bre": _block_diag_in(bbre).astype(BF16), "bim": _block_diag_in(bbim).astype(BF16),
              "cre": _block_diag_out(ssm_c_re[i]).astype(BF16),
              "cimn": _block_diag_out(-ssm_c_im[i]).astype(BF16),
              "d": ssm_d[i].reshape(1, GROUP_W), "wglu": ssm_w_glu[i].astype(BF16)}

        mkv_t = mem_kv_t(mem_prompt, w_mem_kv[i])
        h_p, k_rm, q_t, v_tb, *kv_t = in_proj_prompt(xp, norm_g[i], w_in_bf, n_p, l_p, tq, i,
                                                     depth, kv_t)
        y_a, h_re, h_im = ssm_scan(h_p.reshape(n_p, l_p, -1), PROMPT_COL[A_U], zeros_h, zeros_h,
                                   sp, n_p, l_p, True)
        y_a = y_a.reshape(n_p * l_p, GROUP_W)
        y_c = sb_prompt(q_t, k_rm, v_tb, sb_bias[i], n_p, l_p, tq)
        xp, conv_p = merge_prompt(xp, h_p, y_a, y_c, zeros_buf, mkv_t, group_norm_g[i], conv_w[i],
                                  w_out_bf, final_norm_g, n_p, l_p, final)
        p_re.append(h_re.reshape(n_p, SSM_GROUPS, SSM_STATE))
        p_im.append(h_im.reshape(n_p, SSM_GROUPS, SSM_STATE))
        p_conv.append(conv_p)
        p_mkvt.append(mkv_t)

        h_s = in_proj(xs, norm_g[i], w_in_bf)
        ya_tm, h_re, h_im = ssm_scan(_time_major(_col(h_s, A_U), n_s, l_s), 0,
                                     state_ssm_re[:, i].reshape(n_s, SSM_W),
                                     state_ssm_im[:, i].reshape(n_s, SSM_W), sp, n_s, l_s, False)
        y_a = _seq_major(ya_tm, n_s, l_s)
        seq3 = lambda c: _col(h_s, c).reshape(n_s, l_s, GROUP_W)
        y_c = sb_sample(seq3(C_Q), seq3(C_K), seq3(C_V), sb_bias[i], cache_kt, cache_vt,
                        page_table, i).reshape(n_s * l_s, GROUP_W)
        xs, v_conv = merge_sample(xs, h_s, y_a, y_c, state_conv, cache_mkt, cache_mvt,
                                  group_norm_g[i], conv_w[i], w_out_bf, final_norm_g,
                                  n_s, l_s, i, final)
        s_k.append(seq3(C_K).reshape(n_s, l_s, N_HEADS, HEAD_DIM))
        s_v.append(seq3(C_V).reshape(n_s, l_s, N_HEADS, HEAD_DIM))
        s_re.append(h_re.reshape(n_s, SSM_GROUPS, SSM_STATE))
        s_im.append(h_im.reshape(n_s, SSM_GROUPS, SSM_STATE))
        s_conv.append(v_conv.reshape(n_s, l_s, GROUP_W)[:, l_s - (CONV_K - 1):, :])

    stack = lambda xs_: jnp.stack(xs_, axis=1)
    mkv = stack(p_mkvt)
    return (xp.reshape(n_p, l_p, d), xs.reshape(n_s, l_s, d),
            _from_pos_minor(kv_t[0]), _from_pos_minor(kv_t[1]),
            stack(p_re), stack(p_im), stack(p_conv),
            _from_pos_minor(mkv[:, :, :GROUP_W]), _from_pos_minor(mkv[:, :, GROUP_W:]),
            stack(s_k), stack(s_v), stack(s_re), stack(s_im), stack(s_conv))
```

```python
import functools

import jax
import jax.numpy as jnp
from jax import lax
from jax.experimental import pallas as pl
from jax.experimental.pallas import tpu as pltpu

F32 = jnp.float32
BF16 = jnp.bfloat16

EPS = 1e-6
GROUP_W = 256
N_IN_BLOCKS = 12
HEAD_DIM = 64
N_HEADS = 4
ATT_SCALE = HEAD_DIM ** -0.5
LOG2E = 1.4426950408889634
SSM_GROUPS = 16
SSM_CH = 16
SSM_STATE = 64
SSM_W = SSM_GROUPS * SSM_STATE
CONV_K = 3
LANES = 128

(A_U, A_G, B_B, B_C, B_X, B_G, C_Q, C_K, C_V, C_G, M_Q, M_G) = range(N_IN_BLOCKS)
PROMPT_BLOCKS = (A_U, A_G, B_B, B_C, B_X, B_G, C_G, M_Q, M_G)
PROMPT_COL = {b: j for j, b in enumerate(PROMPT_BLOCKS)}
SAMPLE_COL = {b: b for b in range(N_IN_BLOCKS)}

VMEM_LIMIT = 48 * 1024 * 1024


def _cparams(*sem):
    return pltpu.CompilerParams(dimension_semantics=sem, vmem_limit_bytes=VMEM_LIMIT)


def _dot(a, b):
    return jnp.dot(a, b, preferred_element_type=F32)


def _dot_nt(a, b):
    return lax.dot_general(a, b, (((1,), (1,)), ((), ())), preferred_element_type=F32)


def _rms(x, g):
    return x * lax.rsqrt(jnp.mean(x * x, axis=-1, keepdims=True) + EPS) * g


def _silu(x):
    return x * jax.nn.sigmoid(x)


def _head(h):
    return slice(h * HEAD_DIM, (h + 1) * HEAD_DIM)


def _inproj_kernel(x_ref, g_ref, w_ref, o_ref):
    xn = _rms(x_ref[...], g_ref[...])
    o_ref[...] = _dot(xn.astype(BF16), w_ref[...])


def in_proj(x2d, g, w_bf16):
    rows, d = x2d.shape
    c = w_bf16.shape[1]
    tm = min(rows, 256)
    return pl.pallas_call(
        _inproj_kernel,
        grid=(rows // tm,),
        in_specs=[pl.BlockSpec((tm, d), lambda i: (i, 0)),
                  pl.BlockSpec((1, d), lambda i: (0, 0)),
                  pl.BlockSpec((d, c), lambda i: (0, 0))],
        out_specs=pl.BlockSpec((tm, c), lambda i: (i, 0)),
        out_shape=jax.ShapeDtypeStruct((rows, c), F32),
        compiler_params=_cparams("parallel"),
        name="in_proj",
    )(x2d, g.reshape(1, d), w_bf16)


def _inproj_prompt_kernel(tq, x_ref, g_ref, w_ref, *refs):
    h_ref, krm_ref, qt_ref, vtb_ref, kt_ref, vt_ref = refs[-6:]
    blk = lambda b: slice(b * GROUP_W, (b + 1) * GROUP_W)
    xn = _rms(x_ref[...], g_ref[...]).astype(BF16)
    hm = _dot(xn, w_ref[...])
    h_ref[...] = jnp.concatenate([hm[:, blk(b)] for b in PROMPT_BLOCKS], axis=1)
    k = hm[:, blk(C_K)]
    krm_ref[...] = k.astype(BF16)
    q = hm[:, blk(C_Q)] * (ATT_SCALE * LOG2E)
    v = hm[:, blk(C_V)]
    slots = range(kt_ref.shape[0]) if len(kt_ref.shape) == 3 else (None,)
    for j in range(x_ref.shape[0] // tq):
        rows = slice(j * tq, (j + 1) * tq)
        k_t = k[rows].T
        v_t = v[rows].T
        for slot in slots:
            idx = (slice(None), rows) if slot is None else (slot, slice(None), rows)
            kt_ref[idx] = k_t
            vt_ref[idx] = v_t
        qt_ref[j] = q[rows].T.astype(BF16)
        vtb_ref[j] = v_t.astype(BF16)


def in_proj_prompt(x2d, g, w_bf16, n_seq, t_len, tq, layer, depth, kv_t):
    rows, d = x2d.shape
    per = 2 if t_len % (2 * tq) == 0 else 1
    tm = per * tq
    nb = t_len // tm
    n_main = len(PROMPT_BLOCKS) * GROUP_W
    full = lambda shape: pl.BlockSpec(shape, lambda n, i: (0,) * len(shape))
    blocked = pl.BlockSpec((None, per, GROUP_W, tq), lambda n, i: (n, i, 0, 0))
    prior = () if kv_t is None else tuple(kv_t)
    if prior:
        final_t = pl.BlockSpec((None, None, GROUP_W, tm), lambda n, i: (n, layer, 0, i))
    else:
        final_t = pl.BlockSpec((None, depth, GROUP_W, tm), lambda n, i: (n, 0, 0, i))
    kv_shape = jax.ShapeDtypeStruct((n_seq, depth, GROUP_W, t_len), F32)
    return pl.pallas_call(
        functools.partial(_inproj_prompt_kernel, tq),
        grid=(n_seq, nb),
        in_specs=[pl.BlockSpec((tm, d), lambda n, i: (n * nb + i, 0)),
                  full((1, d)), full(w_bf16.shape)]
        + [pl.BlockSpec(memory_space=pl.ANY)] * len(prior),
        out_specs=[pl.BlockSpec((tm, n_main), lambda n, i: (n * nb + i, 0)),
                   pl.BlockSpec((tm, GROUP_W), lambda n, i: (n * nb + i, 0)),
                   blocked, blocked, final_t, final_t],
        out_shape=[jax.ShapeDtypeStruct((rows, n_main), F32),
                   jax.ShapeDtypeStruct((rows, GROUP_W), BF16),
                   jax.ShapeDtypeStruct((n_seq, t_len // tq, GROUP_W, tq), BF16),
                   jax.ShapeDtypeStruct((n_seq, t_len // tq, GROUP_W, tq), BF16),
                   kv_shape, kv_shape],
        input_output_aliases={3: 4, 4: 5} if prior else {},
        compiler_params=_cparams("parallel", "parallel"),
        name="in_proj_prompt",
    )(x2d, g.reshape(1, d), w_bf16, *prior)


def _memkv_kernel(x_ref, wt_ref, o_ref):
    o_ref[...] = _dot_nt(wt_ref[...], x_ref[...].astype(BF16))


def mem_kv_t(mem, w_mem):
    n_seq, n_mem, d = mem.shape
    w_t = w_mem.T.astype(BF16)
    return pl.pallas_call(
        _memkv_kernel,
        grid=(n_seq,),
        in_specs=[pl.BlockSpec((None, n_mem, d), lambda n: (n, 0, 0)),
                  pl.BlockSpec(w_t.shape, lambda n: (0, 0))],
        out_specs=pl.BlockSpec((None, w_t.shape[0], n_mem), lambda n: (n, 0, 0)),
        out_shape=jax.ShapeDtypeStruct((n_seq, w_t.shape[0], n_mem), F32),
        compiler_params=_cparams("parallel"),
        name="mem_kv",
    )(mem, w_t)


def _ssm_disc_kernel(lre_ref, lim_ref, dt_ref, bre_ref, bim_ref,
                     lbre_ref, lbim_ref, bbre_ref, bbim_ref):
    lre = lre_ref[...]
    lim = lim_ref[...]
    dt = jnp.exp(dt_ref[...])
    mag = jnp.exp(lre * dt)
    lbre = mag * jnp.cos(lim * dt)
    lbim = mag * jnp.sin(lim * dt)
    lbre_ref[...] = lbre
    lbim_ref[...] = lbim
    nre = lbre - 1.0
    nim = lbim
    den = lre * lre + lim * lim
    cre = (nre * lre + nim * lim) / den
    cim = (nim * lre - nre * lim) / den
    bre = bre_ref[...]
    bim = bim_ref[...]
    bbre_ref[...] = cre * bre - cim * bim
    bbim_ref[...] = cre * bim + cim * bre


def ssm_discretise(lam_re, lam_im, log_dt, b_re, b_im):
    col = lambda a: a.reshape(SSM_W, 1)
    dt_col = jnp.broadcast_to(log_dt[:, None], (SSM_GROUPS, SSM_STATE)).reshape(SSM_W, 1)
    outs = pl.pallas_call(
        _ssm_disc_kernel,
        out_shape=[jax.ShapeDtypeStruct((SSM_W, 1), F32)] * 2
        + [jax.ShapeDtypeStruct((SSM_W, SSM_CH), F32)] * 2,
        name="ssm_discretise",
    )(col(lam_re), col(lam_im), dt_col, b_re.reshape(SSM_W, SSM_CH), b_im.reshape(SSM_W, SSM_CH))
    lbre, lbim, bbre, bbim = outs
    shp = (SSM_GROUPS, SSM_STATE, SSM_CH)
    return lbre.reshape(1, SSM_W), lbim.reshape(1, SSM_W), bbre.reshape(shp), bbim.reshape(shp)


def _block_diag_in(b_gpc):
    eye = jnp.eye(SSM_GROUPS, dtype=F32)
    m = b_gpc.transpose(0, 2, 1)[:, :, None, :] * eye[:, None, :, None]
    return m.reshape(SSM_GROUPS * SSM_CH, SSM_W)


def _block_diag_out(c_gcp):
    eye = jnp.eye(SSM_GROUPS, dtype=F32)
    m = c_gcp.transpose(0, 2, 1)[:, :, None, :] * eye[:, None, :, None]
    return m.reshape(SSM_W, SSM_GROUPS * SSM_CH)


def _ssm_pitch(t_chunk):
    p = -(-t_chunk // 8)
    return 8 * (p if p % 2 else p + 1)


def _ssm_kernel(n_seq, t_chunk, seq_major,
                u_ref, h0re_ref, h0im_ref, lre_ref, lim_ref,
                bre_ref, bim_ref,
                cre_ref, cimn_ref, d_ref, wglu_ref,
                y_ref, hre_ref, him_ref,
                bu_re, bu_im, hs_re, hs_im):
    step = pl.program_id(0)
    rows = n_seq * t_chunk

    @pl.when(step == 0)
    def _():
        hre_ref[...] = h0re_ref[...]
        him_ref[...] = h0im_ref[...]

    n_slab = SSM_W // LANES
    slab = lambda j: slice(j * LANES, (j + 1) * LANES)
    pitch = _ssm_pitch(t_chunk) if seq_major else None

    def put(ref, j, val):
        if not seq_major:
            ref[j] = val
            return
        for n in range(n_seq):
            ref[j, n * pitch:n * pitch + t_chunk, :] = val[n * t_chunk:(n + 1) * t_chunk, :]

    def get(ref, j):
        if not seq_major:
            return ref[j]
        return jnp.concatenate(
            [ref[j, n * pitch:n * pitch + t_chunk, :] for n in range(n_seq)], axis=0)

    def rows_of(t):
        if seq_major:
            return pl.ds(t, n_seq, stride=pitch)
        return pl.ds(pl.multiple_of(t * n_seq, n_seq), n_seq)

    halves = [slice(0, rows // 2), slice(rows // 2, rows)] if rows % 16 == 0 else [slice(0, rows)]
    u = u_ref[...].reshape(rows, GROUP_W)
    u_bf = u.astype(BF16)
    b_re = jnp.concatenate([_dot(u_bf[r], bre_ref[...]) for r in halves], axis=0)
    b_im = jnp.concatenate([_dot(u_bf[r], bim_ref[...]) for r in halves], axis=0)
    for j in range(n_slab):
        put(bu_re, j, b_re[:, slab(j)])
        put(bu_im, j, b_im[:, slab(j)])

    lre = [jnp.broadcast_to(lre_ref[:, slab(j)], (n_seq, LANES)) for j in range(n_slab)]
    lim = [jnp.broadcast_to(lim_ref[:, slab(j)], (n_seq, LANES)) for j in range(n_slab)]

    def body(t, carry):
        r = rows_of(t)
        new = []
        for j in range(n_slab):
            hr, hi = carry[j]
            nr = lre[j] * hr - lim[j] * hi + bu_re[j, r, :]
            ni = lre[j] * hi + lim[j] * hr + bu_im[j, r, :]
            hs_re[j, r, :] = nr
            hs_im[j, r, :] = ni
            new.append((nr, ni))
        return tuple(new)

    h0 = tuple((hre_ref[:, slab(j)], him_ref[:, slab(j)]) for j in range(n_slab))
    h_last = lax.fori_loop(0, t_chunk, body, h0, unroll=4)
    for j in range(n_slab):
        hre_ref[:, slab(j)] = h_last[j][0]
        him_ref[:, slab(j)] = h_last[j][1]

    h_re = jnp.concatenate([get(hs_re, j) for j in range(n_slab)], axis=-1).astype(BF16)
    h_im = jnp.concatenate([get(hs_im, j) for j in range(n_slab)], axis=-1).astype(BF16)
    ch = jnp.concatenate([_dot(h_re[r], cre_ref[...]) + _dot(h_im[r], cimn_ref[...])
                          for r in halves], axis=0)
    y = ch + d_ref[...] * u
    y = jax.nn.gelu(y)
    y = y * jax.nn.sigmoid(_dot(y.astype(BF16), wglu_ref[...]))
    y_ref[...] = y.reshape(y_ref.shape)


def ssm_scan(u, u_col, h0_re, h0_im, sp, n_seq, t_len, seq_major):
    t_chunk = min(t_len, 128)
    tr = t_chunk * n_seq
    s_rows = n_seq * _ssm_pitch(t_chunk) if seq_major else tr
    full = lambda shape: pl.BlockSpec(shape, lambda i: (0,) * len(shape))
    if seq_major:
        u_spec = pl.BlockSpec((n_seq, t_chunk, GROUP_W), lambda i: (0, i, u_col))
        y_spec = pl.BlockSpec((n_seq, t_chunk, GROUP_W), lambda i: (0, i, 0))
        y_shape = jax.ShapeDtypeStruct((n_seq, t_len, GROUP_W), F32)
    else:
        u_spec = pl.BlockSpec((tr, GROUP_W), lambda i: (i, u_col))
        y_spec = pl.BlockSpec((tr, GROUP_W), lambda i: (i, 0))
        y_shape = jax.ShapeDtypeStruct((t_len * n_seq, GROUP_W), F32)
    return pl.pallas_call(
        functools.partial(_ssm_kernel, n_seq, t_chunk, seq_major),
        grid=(t_len // t_chunk,),
        in_specs=[u_spec,
                  full((n_seq, SSM_W)), full((n_seq, SSM_W)),
                  full((1, SSM_W)), full((1, SSM_W)),
                  full((GROUP_W, SSM_W)), full((GROUP_W, SSM_W)),
                  full((SSM_W, GROUP_W)), full((SSM_W, GROUP_W)),
                  full((1, GROUP_W)), full((GROUP_W, GROUP_W))],
        out_specs=[y_spec, full((n_seq, SSM_W)), full((n_seq, SSM_W))],
        out_shape=[y_shape,
                   jax.ShapeDtypeStruct((n_seq, SSM_W), F32),
                   jax.ShapeDtypeStruct((n_seq, SSM_W), F32)],
        scratch_shapes=[pltpu.VMEM((SSM_W // LANES, s_rows, LANES), F32)] * 4,
        compiler_params=_cparams("arbitrary"),
        name="ssm_scan",
    )(u, h0_re, h0_im, sp["lbre"], sp["lbim"],
      sp["bre"], sp["bim"],
      sp["cre"], sp["cimn"], sp["d"], sp["wglu"])


def _softplus(z):
    return jnp.maximum(z, 0.0) + jnp.log(1.0 + jnp.exp(-jnp.abs(z)))


def _softplus2(z2):
    return jnp.maximum(z2, 0.0) + jnp.log(1.0 + jnp.exp2(-jnp.abs(z2))) * LOG2E


def _sb_kernel(tq, t_new, page, n_pages, layer, cp, n_chunks,
               pt_ref, bias_ref,
               qt_ref, k_ref, vt_ref,
               qrep_ref, ck_hbm, cv_hbm, knt_hbm, vnt_hbm,
               o_ref, os_ref,
               z_ref, sp_ref, d_ref, w_ref, acc_ref,
               kbuf, vbuf, sem, qbd_ref, scarry_ref, ws_ref):
    n = pl.program_id(0)
    i = pl.program_id(1)
    nb = pl.num_programs(1)
    slots_per_seq = nb * (nb + 1) // 2
    c_diag = n * slots_per_seq + i * (i + 1) // 2
    chunks_per_seq = n_pages // cp
    n_s = n_chunks // chunks_per_seq
    rows_q = N_HEADS * t_new
    heads = range(N_HEADS)

    def chunk_copies(c, with_new):
        b = c % 3
        bv = c % 4
        seq = c // chunks_per_seq
        sub = c % chunks_per_seq
        if with_new:
            return [pltpu.make_async_copy(knt_hbm.at[seq], kbuf.at[b, cp], sem.at[0, b]),
                    pltpu.make_async_copy(vnt_hbm.at[seq], vbuf.at[bv, cp], sem.at[1, bv])]
        copies = []
        for g in range(cp):
            phys = pt_ref[seq, n_pages - 1 - (sub * cp + g)]
            copies.append(pltpu.make_async_copy(ck_hbm.at[phys, layer], kbuf.at[b, g],
                                                sem.at[0, b]))
            copies.append(pltpu.make_async_copy(cv_hbm.at[phys, layer], vbuf.at[bv, g],
                                                sem.at[1, bv]))
        return copies

    def start_chunk(c):
        @pl.when(c < n_chunks)
        def _():
            for cpy in chunk_copies(c, False):
                cpy.start()

            @pl.when(c % chunks_per_seq == 0)
            def _():
                for cpy in chunk_copies(c, True):
                    cpy.start()

    jk = lax.broadcasted_iota(jnp.int32, (page, page), 0)
    sk = lax.broadcasted_iota(jnp.int32, (page, page), 1)
    suffix_p = jnp.where(jk > sk, -1.0, 0.0).astype(BF16)
    row_head = lax.broadcasted_iota(jnp.int32, (rows_q, 1), 0) // t_new
    bias_col = jnp.zeros((rows_q, 1), F32)
    for h in heads:
        bias_col = jnp.where(row_head == h, bias_ref[h], bias_col)
    bias_rows = jnp.concatenate([bias_col] * cp, axis=0)

    def open_chunk(c):
        @pl.when(c < n_chunks)
        def _():
            for cpy in chunk_copies(c, False):
                cpy.wait()

            @pl.when(c % chunks_per_seq == 0)
            def _():
                for cpy in chunk_copies(c, True):
                    cpy.wait()
                b = c % 3
                seq = c // chunks_per_seq
                lane_head = lax.broadcasted_iota(jnp.int32, (rows_q, GROUP_W), 1) // HEAD_DIM
                qbd = jnp.where(lane_head == row_head, qrep_ref[seq] * ATT_SCALE, 0.0).astype(BF16)
                qbd_ref[...] = qbd
                t_of_row = lax.broadcasted_iota(jnp.int32, (rows_q, page), 0) % t_new
                key = lax.broadcasted_iota(jnp.int32, (rows_q, page), 1)
                z = _dot(qbd, kbuf[b, cp].astype(BF16)) + bias_col
                w, total = _sb_rows_block(z, key < t_of_row, jnp.zeros((rows_q, 1), F32), suffix_p)
                os_ref[seq] = _dot_nt(w.astype(BF16), vbuf[c % 4, cp].astype(BF16))
                scarry_ref[...] = total

    def sample_stages(c):
        b = c % 3
        valid = c < n_chunks
        st = {}

        def prev_values():
            p = jnp.maximum(c - 1, 0)
            ok = jnp.logical_and(c >= 1, p < n_chunks)
            seq = jnp.minimum(p // chunks_per_seq, n_s - 1)
            w_prev = ws_ref[p % 2]
            old = os_ref[seq]
            acc = old
            for g in range(cp):
                acc = acc + _dot_nt(w_prev[g * rows_q:(g + 1) * rows_q, :],
                                    vbuf[p % 4, g].astype(BF16))
            os_ref[seq] = jnp.where(ok, acc, old)

        def scores():
            qbd = qbd_ref[...]
            st["z"] = jnp.concatenate(
                [_dot(qbd, kbuf[b, g].astype(BF16)) for g in range(cp)], axis=0) + bias_rows

        def softplus_and_suffix():
            st["sp"] = _softplus(st["z"])
            st["sp_bf"] = st["sp"].astype(BF16)
            st["local"] = _dot(st["sp_bf"], suffix_p)

        def weights():
            local_s = st["local"]
            totals = local_s[:, 0:1] - st["sp_bf"][:, 0:1].astype(F32)
            old = scarry_ref[...]
            carry = old
            carries_s = []
            for g in range(cp):
                carries_s.append(carry)
                carry = carry + totals[g * rows_q:(g + 1) * rows_q, :]
            scarry_ref[...] = jnp.where(valid, carry, old)
            ws_ref[c % 2] = jnp.exp((st["z"] - st["sp"])
                                    + (local_s + jnp.concatenate(carries_s, axis=0))).astype(BF16)

        return (prev_values, scores, softplus_and_suffix, weights)

    s_idx = lax.broadcasted_iota(jnp.int32, (tq, tq), 0)
    t_idx = lax.broadcasted_iota(jnp.int32, (tq, tq), 1)
    neg_suffix = jnp.where(t_idx > s_idx, -1.0, 0.0).astype(BF16)
    diag_mask = s_idx < t_idx

    def block_of(t):
        return jnp.maximum(i - t, 0)

    def scores(t):
        rows = pl.ds(pl.multiple_of(block_of(t) * tq, tq), tq)
        for h in heads:
            z_ref[t % 2, h] = _dot(k_ref[rows, _head(h)], qt_ref[_head(h), :])

    def front(t, mask):
        for h in heads:
            z = z_ref[t % 2, h] + bias_ref[h] * LOG2E
            sp = _softplus2(z)
            d_ref[t % 2, h] = z - sp
            sp_m = sp if mask is None else jnp.where(mask, sp, 0.0)
            sp_ref[t % 2, h] = sp_m.astype(BF16)

    def suffix_sums(t):
        return [_dot(neg_suffix, sp_ref[t % 2, h]) for h in heads]

    def weights(t, local, carries, mask):
        new_carries = []
        for h in heads:
            w = jnp.exp2(d_ref[t % 2, h] + (local[h] + carries[h]))
            if mask is not None:
                w = jnp.where(mask, w, 0.0)
            w_ref[h] = w.astype(BF16)
            new_carries.append(
                carries[h] + (local[h][0:1, :] - sp_ref[t % 2, h, 0:1, :].astype(F32)))
        return tuple(new_carries)

    def last(t):
        kb = block_of(t)
        for h in heads:
            acc_ref[_head(h), :] += _dot(vt_ref[kb, _head(h), :], w_ref[h])

    @pl.when(c_diag == 0)
    def _():
        start_chunk(c_diag)
        start_chunk(c_diag + 1)
        ws_ref[...] = jnp.zeros(ws_ref.shape, BF16)

    acc_ref[...] = jnp.zeros(acc_ref.shape, F32)
    open_chunk(c_diag)
    scores(0)
    scores(1)
    front(0, diag_mask)

    def trip(t, carries, mask):
        side = sample_stages(c_diag + t)
        side[0]()
        side[1]()
        side[2]()
        local = suffix_sums(t)
        front(t + 1, None)
        scores(t + 2)
        carries = weights(t, local, carries, mask)
        side[3]()
        last(t)
        start_chunk(c_diag + t + 2)
        return carries

    zero = jnp.zeros((1, tq), F32)
    carries = trip(0, (zero,) * N_HEADS, diag_mask)

    def body(t, carries):
        open_chunk(c_diag + t)
        return trip(t, carries, None)

    carries = lax.fori_loop(1, i, body, carries)

    @pl.when(i >= 1)
    def _():
        open_chunk(c_diag + i)
        side = sample_stages(c_diag + i)
        side[0]()
        side[1]()
        side[2]()
        weights(i, suffix_sums(i), carries, None)
        side[3]()
        last(i)
        start_chunk(c_diag + i + 2)

    o_ref[...] = acc_ref[...].T


def _chunk_pages(n_pages, n_chunks_max):
    for cp in range(1, n_pages + 1):
        if n_pages % cp == 0 and n_pages // cp <= n_chunks_max:
            return cp
    raise ValueError("prompt too short to carry the sample group's pages")


def sb_attention(q_t, k_rm, v_t, q_s, k_new, v_new, bias, cache_kt, cache_vt, page_table, layer,
                 n_seq, t_len, tq):
    nb = t_len // tq
    n_s, t_new, _ = q_s.shape
    page = cache_kt.shape[3]
    n_pages = page_table.shape[1]
    n_slots = n_seq * nb * (nb + 1) // 2
    cp = _chunk_pages(n_pages, (n_slots - 1) // n_s)
    n_chunks = n_s * (n_pages // cp)
    rows_q = N_HEADS * t_new
    q_rep = jnp.tile(q_s, (1, N_HEADS, 1))
    pad = ((0, 0), (0, 0), (0, page - t_new))
    k_new_t = jnp.pad(k_new.transpose(0, 2, 1), pad)
    v_new_t = jnp.pad(v_new.transpose(0, 2, 1), pad)
    full3 = lambda shape: pl.BlockSpec(shape, lambda n, i, pt: (0, 0, 0))
    hbm = pl.BlockSpec(memory_space=pl.ANY)
    y_p, acc = pl.pallas_call(
        functools.partial(_sb_kernel, tq, t_new, page, n_pages, layer, cp, n_chunks),
        grid_spec=pltpu.PrefetchScalarGridSpec(
            num_scalar_prefetch=1,
            grid=(n_seq, nb),
            in_specs=[pl.BlockSpec(memory_space=pltpu.SMEM),
                      pl.BlockSpec((None, None, GROUP_W, tq), lambda n, i, pt: (n, i, 0, 0)),
                      pl.BlockSpec((t_len, GROUP_W), lambda n, i, pt: (n, 0)),
                      pl.BlockSpec((None, nb, GROUP_W, tq), lambda n, i, pt: (n, 0, 0, 0)),
                      full3((n_s, rows_q, GROUP_W)),
                      hbm, hbm, hbm, hbm],
            out_specs=[pl.BlockSpec((tq, GROUP_W), lambda n, i, pt: (n * nb + i, 0)),
                       full3((n_s, rows_q, GROUP_W))],
            scratch_shapes=[pltpu.VMEM((2, N_HEADS, tq, tq), F32),
                            pltpu.VMEM((2, N_HEADS, tq, tq), BF16),
                            pltpu.VMEM((2, N_HEADS, tq, tq), F32),
                            pltpu.VMEM((N_HEADS, tq, tq), BF16),
                            pltpu.VMEM((GROUP_W, tq), F32),
                            pltpu.VMEM((3, cp + 1, GROUP_W, page), F32),
                            pltpu.VMEM((4, cp + 1, GROUP_W, page), F32),
                            pltpu.SemaphoreType.DMA((2, 4)),
                            pltpu.VMEM((rows_q, GROUP_W), BF16),
                            pltpu.VMEM((rows_q, 1), F32),
                            pltpu.VMEM((2, cp * rows_q, page), BF16)]),
        out_shape=[jax.ShapeDtypeStruct((n_seq * t_len, GROUP_W), F32),
                   jax.ShapeDtypeStruct((n_s, rows_q, GROUP_W), F32)],
        compiler_params=_cparams("arbitrary", "arbitrary"),
        name="sb_attention",
    )(page_table, bias, q_t, k_rm, v_t, q_rep, cache_kt, cache_vt, k_new_t, v_new_t)
    acc = acc.reshape(n_s, N_HEADS, t_new, N_HEADS, HEAD_DIM)
    picked = [acc[:, h, :, h, :] for h in range(N_HEADS)]
    return y_p, jnp.stack(picked, axis=2).reshape(n_s, t_new, GROUP_W)


def _sb_rows_block(z, mask, carry, neg_suffix):
    sp = _softplus(z)
    sp_m = sp if mask is None else jnp.where(mask, sp, 0.0)
    sp_bf = sp_m.astype(BF16)
    local = _dot(sp_bf, neg_suffix)
    w = jnp.exp((z - sp) + (local + carry))
    if mask is not None:
        w = jnp.where(mask, w, 0.0)
    return w, local[:, 0:1] - sp_bf[:, 0:1].astype(F32)


def _mem_scores(mq_ref, mkt_of_head):
    return [_dot((mq_ref[:, _head(h)] * ATT_SCALE).astype(BF16), mkt_of_head(h).astype(BF16))
            for h in range(N_HEADS)]


def _mem_values(s, mvt_h):
    e = jnp.exp(s - jnp.max(s, axis=-1, keepdims=True))
    return _dot_nt(e.astype(BF16), mvt_h.astype(BF16)) / jnp.sum(e, axis=-1, keepdims=True)


def _out_part(y, gate, group, gn_ref, wout_ref):
    m = _rms(y, gn_ref[group:group + 1, :]) * _silu(gate)
    return _dot(m.astype(BF16), wout_ref[group * GROUP_W:(group + 1) * GROUP_W, :])


def _conv_taps(v, vm1, vm2, b_gate, cw_ref):
    return b_gate * (vm2 * cw_ref[0:1, :] + vm1 * cw_ref[1:2, :] + v * cw_ref[2:3, :])


def _merge_prompt_kernel(tm, final,
                         x_ref, ag_ref, bb_ref, bc_ref, bx_ref, bg_ref, cg_ref, mq_ref, mg_ref,
                         hbc_ref, hbx_ref, buf_ref, ya_ref, yc_ref, mkt_ref, mvt_ref,
                         gn_ref, cw_ref, wout_ref, fg_ref,
                         o_ref, tail_ref):
    i = pl.program_id(1)
    scores = _mem_scores(mq_ref, lambda h: mkt_ref[_head(h), :])
    y_m = jnp.concatenate([_mem_values(scores[h], mvt_ref[_head(h), :]) for h in range(N_HEADS)],
                          axis=-1)
    part = _out_part(ya_ref[...], ag_ref[...], 0, gn_ref, wout_ref)
    part = part + _out_part(yc_ref[...], cg_ref[...], 2, gn_ref, wout_ref)

    v = bc_ref[...] * bx_ref[...]
    halo = hbc_ref[...] * hbx_ref[...]
    first = i == 0
    prev1 = jnp.where(first, buf_ref[1:2, :], halo[7:8, :])
    prev2 = jnp.where(first, buf_ref[0:1, :], halo[6:7, :])
    row = lax.broadcasted_iota(jnp.int32, (tm, 1), 0)
    vm1 = jnp.where(row == 0, prev1, pltpu.roll(v, 1, 0))
    vm2 = jnp.where(row == 0, prev2, jnp.where(row == 1, prev1, pltpu.roll(v, 2, 0)))
    y_b = _conv_taps(v, vm1, vm2, bb_ref[...], cw_ref)
    tail_ref[...] = v[tm - 8:tm, :]
    part = part + _out_part(y_b, bg_ref[...], 1, gn_ref, wout_ref)
    part = part + _out_part(y_m, mg_ref[...], 3, gn_ref, wout_ref)
    x_new = x_ref[...] + part
    if final:
        x_new = _rms(x_new, fg_ref[...])
    o_ref[...] = x_new


def merge_prompt(x2d, h2d, y_a, y_c, conv_buf, mkv_t, gn, conv_w, wout_bf16, final_g,
                 n_seq, t_len, final):
    d = x2d.shape[1]
    n_mem = mkv_t.shape[2]
    tm = min(t_len, 512)
    nb = t_len // tm
    col = PROMPT_COL
    hblk = lambda c: pl.BlockSpec((tm, GROUP_W), lambda n, i, c=c: (n * nb + i, col[c]))
    halo = lambda c: pl.BlockSpec(
        (8, GROUP_W), lambda n, i, c=c: (jnp.maximum((n * nb + i) * (tm // 8) - 1, 0), col[c]))
    rows = pl.BlockSpec((tm, GROUP_W), lambda n, i: (n * nb + i, 0))
    full = lambda shape: pl.BlockSpec(shape, lambda n, i: (0,) * len(shape))
    out, tail = pl.pallas_call(
        functools.partial(_merge_prompt_kernel, tm, final),
        grid=(n_seq, nb),
        in_specs=[pl.BlockSpec((tm, d), lambda n, i: (n * nb + i, 0)),
                  hblk(A_G), hblk(B_B), hblk(B_C), hblk(B_X), hblk(B_G), hblk(C_G),
                  hblk(M_Q), hblk(M_G), halo(B_C), halo(B_X),
                  pl.BlockSpec((None, CONV_K - 1, GROUP_W), lambda n, i: (n, 0, 0)),
                  rows, rows,
                  pl.BlockSpec((None, GROUP_W, n_mem), lambda n, i: (n, 0, 0)),
                  pl.BlockSpec((None, GROUP_W, n_mem), lambda n, i: (n, 1, 0)),
                  full((4, GROUP_W)), full((CONV_K, GROUP_W)), full(wout_bf16.shape),
                  full((1, d))],
        out_specs=[pl.BlockSpec((tm, d), lambda n, i: (n * nb + i, 0)),
                   pl.BlockSpec((None, 8, GROUP_W), lambda n, i: (n, 0, 0))],
        out_shape=[jax.ShapeDtypeStruct(x2d.shape, F32),
                   jax.ShapeDtypeStruct((n_seq, 8, GROUP_W), F32)],
        compiler_params=_cparams("parallel", "arbitrary"),
        name="merge_prompt",
    )(x2d, h2d, h2d, h2d, h2d, h2d, h2d, h2d, h2d, h2d, h2d, conv_buf, y_a, y_c, mkv_t, mkv_t,
      gn, conv_w, wout_bf16, final_g.reshape(1, d))
    return out, tail[:, 8 - (CONV_K - 1):, :]


def _merge_sample_kernel(t_len, seqs, final,
                         x_ref, ag_ref, bb_ref, bc_ref, bx_ref, bg_ref, cg_ref, mq_ref, mg_ref,
                         buf_ref, ya_ref, yc_ref, mkt_ref, mvt_ref,
                         gn_ref, cw_ref, wout_ref, fg_ref,
                         o_ref, v_ref):
    tm = seqs * t_len
    row = lax.broadcasted_iota(jnp.int32, (tm, 1), 0)
    seq_of_row = row // t_len
    t_of_row = row % t_len

    v = bc_ref[...] * bx_ref[...]
    v_ref[...] = v
    prev1 = jnp.zeros((tm, GROUP_W), F32)
    prev2 = jnp.zeros((tm, GROUP_W), F32)
    for s in range(seqs):
        prev1 = jnp.where(seq_of_row == s, buf_ref[s, 1:2, :], prev1)
        prev2 = jnp.where(seq_of_row == s, buf_ref[s, 0:1, :], prev2)
    vm1 = jnp.where(t_of_row == 0, prev1, pltpu.roll(v, 1, 0))
    vm2 = jnp.where(t_of_row == 0, prev2, jnp.where(t_of_row == 1, prev1, pltpu.roll(v, 2, 0)))
    y_b = _conv_taps(v, vm1, vm2, bb_ref[...], cw_ref)

    heads = [jnp.zeros((tm, HEAD_DIM), F32)] * N_HEADS
    for s in range(seqs):
        scores = _mem_scores(mq_ref, lambda h, s=s: mkt_ref[s, _head(h), :])
        for h in range(N_HEADS):
            o = _mem_values(scores[h], mvt_ref[s, _head(h), :])
            heads[h] = jnp.where(seq_of_row == s, o, heads[h])
    y_m = jnp.concatenate(heads, axis=-1)

    part = _out_part(ya_ref[...], ag_ref[...], 0, gn_ref, wout_ref)
    part = part + _out_part(y_b, bg_ref[...], 1, gn_ref, wout_ref)
    part = part + _out_part(yc_ref[...], cg_ref[...], 2, gn_ref, wout_ref)
    part = part + _out_part(y_m, mg_ref[...], 3, gn_ref, wout_ref)
    x_new = x_ref[...] + part
    o_ref[...] = _rms(x_new, fg_ref[...]) if final else x_new


def merge_sample(x2d, h2d, y_a, y_c, state_conv, cache_mkt, cache_mvt, gn, conv_w, wout_bf16,
                 final_g, n_seq, t_len, layer, final):
    d = x2d.shape[1]
    n_mem = cache_mkt.shape[3]
    seqs = 16 // t_len
    tm = seqs * t_len
    hblk = lambda c: pl.BlockSpec((tm, GROUP_W), lambda i, c=c: (i, c))
    rows = pl.BlockSpec((tm, GROUP_W), lambda i: (i, 0))
    full = lambda shape: pl.BlockSpec(shape, lambda i: (0,) * len(shape))
    mem = pl.BlockSpec((seqs, None, GROUP_W, n_mem), lambda i: (i, layer, 0, 0))
    return pl.pallas_call(
        functools.partial(_merge_sample_kernel, t_len, seqs, final),
        grid=(n_seq // seqs,),
        in_specs=[pl.BlockSpec((tm, d), lambda i: (i, 0)),
                  hblk(A_G), hblk(B_B), hblk(B_C), hblk(B_X), hblk(B_G), hblk(C_G),
                  hblk(M_Q), hblk(M_G),
                  pl.BlockSpec((seqs, None, CONV_K - 1, GROUP_W), lambda i: (i, layer, 0, 0)),
                  rows, rows, mem, mem,
                  full((4, GROUP_W)), full((CONV_K, GROUP_W)), full(wout_bf16.shape),
                  full((1, d))],
        out_specs=[pl.BlockSpec((tm, d), lambda i: (i, 0)), rows],
        out_shape=[jax.ShapeDtypeStruct(x2d.shape, F32),
                   jax.ShapeDtypeStruct((x2d.shape[0], GROUP_W), F32)],
        compiler_params=_cparams("parallel"),
        name="merge_sample",
    )(x2d, h2d, h2d, h2d, h2d, h2d, h2d, h2d, h2d, state_conv, y_a, y_c, cache_mkt, cache_mvt,
      gn, conv_w, wout_bf16, final_g.reshape(1, d))


def _time_major(a, n_seq, t_len):
    return a.reshape(n_seq, t_len, -1).transpose(1, 0, 2).reshape(n_seq * t_len, -1)


def _seq_major(a, n_seq, t_len):
    return a.reshape(t_len, n_seq, -1).transpose(1, 0, 2).reshape(n_seq * t_len, -1)


def _col(h2d, c):
    return h2d[:, c * GROUP_W:(c + 1) * GROUP_W]


def _pos_minor(a):
    lead = a.shape[:-3]
    pos, heads, dim = a.shape[-3:]
    nd = len(lead)
    return a.transpose(*range(nd), nd + 1, nd + 2, nd).reshape(*lead, heads * dim, pos)


def _from_pos_minor(a_t):
    lead = a_t.shape[:-2]
    pos = a_t.shape[-1]
    nd = len(lead)
    a = a_t.reshape(*lead, N_HEADS, HEAD_DIM, pos)
    return a.transpose(*range(nd), nd + 2, nd, nd + 1)


def kernel(x_prompt, x_sample, cache_sb_k, cache_sb_v, state_ssm_re, state_ssm_im, state_conv,
           cache_mem_k, cache_mem_v, page_table, mem_prompt, norm_g, w_in, w_out, group_norm_g,
           ssm_lambda_re, ssm_lambda_im, ssm_b_re, ssm_b_im, ssm_c_re, ssm_c_im, ssm_log_dt, ssm_d,
           ssm_w_glu, conv_w, sb_bias, w_mem_kv, final_norm_g):
    n_p, l_p, d = x_prompt.shape
    n_s, l_s, _ = x_sample.shape
    depth = w_in.shape[0]
    assert l_s >= CONV_K - 1 and 16 % l_s == 0 and n_p % 8 == 0 and n_s % 8 == 0
    tq = min(l_p, 256)

    xp = x_prompt.reshape(n_p * l_p, d)
    xs = x_sample.reshape(n_s * l_s, d)
    cache_kt = _pos_minor(cache_sb_k)
    cache_vt = _pos_minor(cache_sb_v)
    cache_mkt = _pos_minor(cache_mem_k)
    cache_mvt = _pos_minor(cache_mem_v)
    zeros_h = jnp.zeros((n_p, SSM_W), F32)
    zeros_buf = jnp.zeros((n_p, CONV_K - 1, GROUP_W), F32)

    kv_t = None
    p_re, p_im, p_conv, p_mkvt = [], [], [], []
    s_k, s_v, s_re, s_im, s_conv = [], [], [], [], []
    for i in range(depth):
        final = i == depth - 1
        w_in_bf = w_in[i].astype(BF16)
        w_out_bf = w_out[i].astype(BF16)

        lbre, lbim, bbre, bbim = ssm_discretise(ssm_lambda_re[i], ssm_lambda_im[i], ssm_log_dt[i],
                                                ssm_b_re[i], ssm_b_im[i])
        sp = {"lbre": lbre, "lbim": lbim,
              "bre": _block_diag_in(bbre).astype(BF16), "bim": _block_diag_in(bbim).astype(BF16),
              "cre": _block_diag_out(ssm_c_re[i]).astype(BF16),
              "cimn": _block_diag_out(-ssm_c_im[i]).astype(BF16),
              "d": ssm_d[i].reshape(1, GROUP_W), "wglu": ssm_w_glu[i].astype(BF16)}

        mkv_t = mem_kv_t(mem_prompt, w_mem_kv[i])
        h_p, k_rm, q_t, v_tb, *kv_t = in_proj_prompt(xp, norm_g[i], w_in_bf, n_p, l_p, tq, i,
                                                     depth, kv_t)
        h_s = in_proj(xs, norm_g[i], w_in_bf)
        seq3 = lambda c: _col(h_s, c).reshape(n_s, l_s, GROUP_W)

        y_c, y_cs = sb_attention(q_t, k_rm, v_tb, seq3(C_Q), seq3(C_K), seq3(C_V), sb_bias[i],
                                 cache_kt, cache_vt, page_table, i, n_p, l_p, tq)

        y_a, h_re, h_im = ssm_scan(h_p.reshape(n_p, l_p, -1), PROMPT_COL[A_U], zeros_h, zeros_h,
                                   sp, n_p, l_p, True)
        y_a = y_a.reshape(n_p * l_p, GROUP_W)
        xp, conv_p = merge_prompt(xp, h_p, y_a, y_c, zeros_buf, mkv_t, group_norm_g[i], conv_w[i],
                                  w_out_bf, final_norm_g, n_p, l_p, final)
        p_re.append(h_re.reshape(n_p, SSM_GROUPS, SSM_STATE))
        p_im.append(h_im.reshape(n_p, SSM_GROUPS, SSM_STATE))
        p_conv.append(conv_p)
        p_mkvt.append(mkv_t)

        ya_tm, h_re, h_im = ssm_scan(_time_major(_col(h_s, A_U), n_s, l_s), 0,
                                     state_ssm_re[:, i].reshape(n_s, SSM_W),
                                     state_ssm_im[:, i].reshape(n_s, SSM_W), sp, n_s, l_s, False)
        y_a = _seq_major(ya_tm, n_s, l_s)
        xs, v_conv = merge_sample(xs, h_s, y_a, y_cs.reshape(n_s * l_s, GROUP_W), state_conv,
                                  cache_mkt, cache_mvt, group_norm_g[i], conv_w[i], w_out_bf,
                                  final_norm_g, n_s, l_s, i, final)
        s_k.append(seq3(C_K).reshape(n_s, l_s, N_HEADS, HEAD_DIM))
        s_v.append(seq3(C_V).reshape(n_s, l_s, N_HEADS, HEAD_DIM))
        s_re.append(h_re.reshape(n_s, SSM_GROUPS, SSM_STATE))
        s_im.append(h_im.reshape(n_s, SSM_GROUPS, SSM_STATE))
        s_conv.append(v_conv.reshape(n_s, l_s, GROUP_W)[:, l_s - (CONV_K - 1):, :])

    stack = lambda xs_: jnp.stack(xs_, axis=1)
    mkv = stack(p_mkvt)
    return (xp.reshape(n_p, l_p, d), xs.reshape(n_s, l_s, d),
            _from_pos_minor(kv_t[0]), _from_pos_minor(kv_t[1]),
            stack(p_re), stack(p_im), stack(p_conv),
            _from_pos_minor(mkv[:, :, :GROUP_W]), _from_pos_minor(mkv[:, :, GROUP_W:]),
            stack(s_k), stack(s_v), stack(s_re), stack(s_im), stack(s_conv))
```

```python
import functools

import jax
import jax.numpy as jnp
from jax import lax
from jax.experimental import pallas as pl
from jax.experimental.pallas import tpu as pltpu

F32 = jnp.float32
BF16 = jnp.bfloat16

EPS = 1e-6
GROUP_W = 256
N_IN_BLOCKS = 12
HEAD_DIM = 64
N_HEADS = 4
ATT_SCALE = HEAD_DIM ** -0.5
LOG2E = 1.4426950408889634
SSM_GROUPS = 16
SSM_CH = 16
SSM_STATE = 64
SSM_W = SSM_GROUPS * SSM_STATE
CONV_K = 3
LANES = 128

(A_U, A_G, B_B, B_C, B_X, B_G, C_Q, C_K, C_V, C_G, M_Q, M_G) = range(N_IN_BLOCKS)
PROMPT_BLOCKS = (A_U, A_G, B_B, B_C, B_X, B_G, C_G, M_Q, M_G)
PROMPT_COL = {b: j for j, b in enumerate(PROMPT_BLOCKS)}

VMEM_LIMIT = 48 * 1024 * 1024
Q_TILE = 256
ROW_TILE = 256
MERGE_ROWS = 512
SSM_CHUNK = 128


def _cparams(*sem):
    return pltpu.CompilerParams(dimension_semantics=sem, vmem_limit_bytes=VMEM_LIMIT)


def _dot(a, b):
    return jnp.dot(a, b, preferred_element_type=F32)


def _dot_nt(a, b):
    return lax.dot_general(a, b, (((1,), (1,)), ((), ())), preferred_element_type=F32)


def _rms(x, g):
    return x * lax.rsqrt(jnp.mean(x * x, axis=-1, keepdims=True) + EPS) * g


def _silu(x):
    return x * jax.nn.sigmoid(x)


def _head(h):
    return slice(h * HEAD_DIM, (h + 1) * HEAD_DIM)


def _inproj_kernel(x_ref, g_ref, w_ref, o_ref):
    xn = _rms(x_ref[...], g_ref[...])
    o_ref[...] = _dot(xn.astype(BF16), w_ref[...])


def in_proj(x2d, g, w_bf16):
    rows, d = x2d.shape
    c = w_bf16.shape[1]
    tm = min(rows, ROW_TILE)
    return pl.pallas_call(
        _inproj_kernel,
        grid=(rows // tm,),
        in_specs=[pl.BlockSpec((tm, d), lambda i: (i, 0)),
                  pl.BlockSpec((1, d), lambda i: (0, 0)),
                  pl.BlockSpec((d, c), lambda i: (0, 0))],
        out_specs=pl.BlockSpec((tm, c), lambda i: (i, 0)),
        out_shape=jax.ShapeDtypeStruct((rows, c), F32),
        compiler_params=_cparams("parallel"),
        name="in_proj",
    )(x2d, g.reshape(1, d), w_bf16)


def _inproj_prompt_kernel(tq, x_ref, g_ref, w_ref, *refs):
    h_ref, krm_ref, qt_ref, vtb_ref, kt_ref, vt_ref = refs[-6:]
    blk = lambda b: slice(b * GROUP_W, (b + 1) * GROUP_W)
    xn = _rms(x_ref[...], g_ref[...]).astype(BF16)
    hm = _dot(xn, w_ref[...])
    h_ref[...] = jnp.concatenate([hm[:, blk(b)] for b in PROMPT_BLOCKS], axis=1)
    k = hm[:, blk(C_K)]
    krm_ref[...] = k.astype(BF16)
    q = hm[:, blk(C_Q)] * (ATT_SCALE * LOG2E)
    v = hm[:, blk(C_V)]
    slots = range(kt_ref.shape[0]) if len(kt_ref.shape) == 3 else (None,)
    for j in range(x_ref.shape[0] // tq):
        rows = slice(j * tq, (j + 1) * tq)
        k_t = k[rows].T
        v_t = v[rows].T
        for slot in slots:
            idx = (slice(None), rows) if slot is None else (slot, slice(None), rows)
            kt_ref[idx] = k_t
            vt_ref[idx] = v_t
        qt_ref[j] = q[rows].T.astype(BF16)
        vtb_ref[j] = v_t.astype(BF16)


def in_proj_prompt(x2d, g, w_bf16, n_seq, t_len, tq, layer, depth, kv_t):
    rows, d = x2d.shape
    per = 2 if t_len % (2 * tq) == 0 else 1
    tm = per * tq
    nb = t_len // tm
    n_main = len(PROMPT_BLOCKS) * GROUP_W
    full = lambda shape: pl.BlockSpec(shape, lambda n, i: (0,) * len(shape))
    blocked = pl.BlockSpec((None, per, GROUP_W, tq), lambda n, i: (n, i, 0, 0))
    prior = () if kv_t is None else tuple(kv_t)
    if prior:
        final_t = pl.BlockSpec((None, None, GROUP_W, tm), lambda n, i: (n, layer, 0, i))
    else:
        final_t = pl.BlockSpec((None, depth, GROUP_W, tm), lambda n, i: (n, 0, 0, i))
    kv_shape = jax.ShapeDtypeStruct((n_seq, depth, GROUP_W, t_len), F32)
    return pl.pallas_call(
        functools.partial(_inproj_prompt_kernel, tq),
        grid=(n_seq, nb),
        in_specs=[pl.BlockSpec((tm, d), lambda n, i: (n * nb + i, 0)),
                  full((1, d)), full(w_bf16.shape)]
        + [pl.BlockSpec(memory_space=pl.ANY)] * len(prior),
        out_specs=[pl.BlockSpec((tm, n_main), lambda n, i: (n * nb + i, 0)),
                   pl.BlockSpec((tm, GROUP_W), lambda n, i: (n * nb + i, 0)),
                   blocked, blocked, final_t, final_t],
        out_shape=[jax.ShapeDtypeStruct((rows, n_main), F32),
                   jax.ShapeDtypeStruct((rows, GROUP_W), BF16),
                   jax.ShapeDtypeStruct((n_seq, t_len // tq, GROUP_W, tq), BF16),
                   jax.ShapeDtypeStruct((n_seq, t_len // tq, GROUP_W, tq), BF16),
                   kv_shape, kv_shape],
        input_output_aliases={3: 4, 4: 5} if prior else {},
        compiler_params=_cparams("parallel", "parallel"),
        name="in_proj_prompt",
    )(x2d, g.reshape(1, d), w_bf16, *prior)


def _memkv_kernel(x_ref, wt_ref, o_ref):
    o_ref[...] = _dot_nt(wt_ref[...], x_ref[...].astype(BF16))


def mem_kv_t(mem, w_mem):
    n_seq, n_mem, d = mem.shape
    w_t = w_mem.T.astype(BF16)
    return pl.pallas_call(
        _memkv_kernel,
        grid=(n_seq,),
        in_specs=[pl.BlockSpec((None, n_mem, d), lambda n: (n, 0, 0)),
                  pl.BlockSpec(w_t.shape, lambda n: (0, 0))],
        out_specs=pl.BlockSpec((None, w_t.shape[0], n_mem), lambda n: (n, 0, 0)),
        out_shape=jax.ShapeDtypeStruct((n_seq, w_t.shape[0], n_mem), F32),
        compiler_params=_cparams("parallel"),
        name="mem_kv",
    )(mem, w_t)


def _ssm_disc_kernel(lre_ref, lim_ref, dt_ref, bre_ref, bim_ref,
                     lbre_ref, lbim_ref, bbre_ref, bbim_ref):
    lre = lre_ref[...]
    lim = lim_ref[...]
    dt = jnp.exp(dt_ref[...])
    mag = jnp.exp(lre * dt)
    lbre = mag * jnp.cos(lim * dt)
    lbim = mag * jnp.sin(lim * dt)
    lbre_ref[...] = lbre
    lbim_ref[...] = lbim
    nre = lbre - 1.0
    nim = lbim
    den = lre * lre + lim * lim
    cre = (nre * lre + nim * lim) / den
    cim = (nim * lre - nre * lim) / den
    bre = bre_ref[...]
    bim = bim_ref[...]
    bbre_ref[...] = cre * bre - cim * bim
    bbim_ref[...] = cre * bim + cim * bre


def ssm_discretise(lam_re, lam_im, log_dt, b_re, b_im):
    col = lambda a: a.reshape(SSM_W, 1)
    dt_col = jnp.broadcast_to(log_dt[:, None], (SSM_GROUPS, SSM_STATE)).reshape(SSM_W, 1)
    outs = pl.pallas_call(
        _ssm_disc_kernel,
        out_shape=[jax.ShapeDtypeStruct((SSM_W, 1), F32)] * 2
        + [jax.ShapeDtypeStruct((SSM_W, SSM_CH), F32)] * 2,
        name="ssm_discretise",
    )(col(lam_re), col(lam_im), dt_col, b_re.reshape(SSM_W, SSM_CH), b_im.reshape(SSM_W, SSM_CH))
    lbre, lbim, bbre, bbim = outs
    shp = (SSM_GROUPS, SSM_STATE, SSM_CH)
    return lbre.reshape(1, SSM_W), lbim.reshape(1, SSM_W), bbre.reshape(shp), bbim.reshape(shp)


def _block_diag_in(b_gpc):
    eye = jnp.eye(SSM_GROUPS, dtype=F32)
    m = b_gpc.transpose(0, 2, 1)[:, :, None, :] * eye[:, None, :, None]
    return m.reshape(SSM_GROUPS * SSM_CH, SSM_W)


def _block_diag_out(c_gcp):
    eye = jnp.eye(SSM_GROUPS, dtype=F32)
    m = c_gcp.transpose(0, 2, 1)[:, :, None, :] * eye[:, None, :, None]
    return m.reshape(SSM_W, SSM_GROUPS * SSM_CH)


def _ssm_pitch(t_chunk):
    p = -(-t_chunk // 8)
    return 8 * (p if p % 2 else p + 1)


def _ssm_kernel(n_seq, t_chunk, seq_major,
                u_ref, h0re_ref, h0im_ref, lre_ref, lim_ref,
                bre_ref, bim_ref,
                cre_ref, cimn_ref, d_ref, wglu_ref,
                y_ref, hre_ref, him_ref,
                bu_re, bu_im, hs_re, hs_im):
    step = pl.program_id(0)
    rows = n_seq * t_chunk

    @pl.when(step == 0)
    def _():
        hre_ref[...] = h0re_ref[...]
        him_ref[...] = h0im_ref[...]

    n_slab = SSM_W // LANES
    slab = lambda j: slice(j * LANES, (j + 1) * LANES)
    pitch = _ssm_pitch(t_chunk) if seq_major else None

    def put(ref, j, val):
        if not seq_major:
            ref[j] = val
            return
        for n in range(n_seq):
            ref[j, n * pitch:n * pitch + t_chunk, :] = val[n * t_chunk:(n + 1) * t_chunk, :]

    def get(ref, j):
        if not seq_major:
            return ref[j]
        return jnp.concatenate(
            [ref[j, n * pitch:n * pitch + t_chunk, :] for n in range(n_seq)], axis=0)

    def rows_of(t):
        if seq_major:
            return pl.ds(t, n_seq, stride=pitch)
        return pl.ds(pl.multiple_of(t * n_seq, n_seq), n_seq)

    halves = [slice(0, rows // 2), slice(rows // 2, rows)] if rows % 16 == 0 else [slice(0, rows)]
    u = u_ref[...].reshape(rows, GROUP_W)
    u_bf = u.astype(BF16)
    b_re = jnp.concatenate([_dot(u_bf[r], bre_ref[...]) for r in halves], axis=0)
    b_im = jnp.concatenate([_dot(u_bf[r], bim_ref[...]) for r in halves], axis=0)
    for j in range(n_slab):
        put(bu_re, j, b_re[:, slab(j)])
        put(bu_im, j, b_im[:, slab(j)])

    lre = [jnp.broadcast_to(lre_ref[:, slab(j)], (n_seq, LANES)) for j in range(n_slab)]
    lim = [jnp.broadcast_to(lim_ref[:, slab(j)], (n_seq, LANES)) for j in range(n_slab)]

    def body(t, carry):
        r = rows_of(t)
        new = []
        for j in range(n_slab):
            hr, hi = carry[j]
            nr = lre[j] * hr - lim[j] * hi + bu_re[j, r, :]
            ni = lre[j] * hi + lim[j] * hr + bu_im[j, r, :]
            hs_re[j, r, :] = nr
            hs_im[j, r, :] = ni
            new.append((nr, ni))
        return tuple(new)

    h0 = tuple((hre_ref[:, slab(j)], him_ref[:, slab(j)]) for j in range(n_slab))
    h_last = lax.fori_loop(0, t_chunk, body, h0, unroll=4)
    for j in range(n_slab):
        hre_ref[:, slab(j)] = h_last[j][0]
        him_ref[:, slab(j)] = h_last[j][1]

    h_re = jnp.concatenate([get(hs_re, j) for j in range(n_slab)], axis=-1).astype(BF16)
    h_im = jnp.concatenate([get(hs_im, j) for j in range(n_slab)], axis=-1).astype(BF16)
    ch = jnp.concatenate([_dot(h_re[r], cre_ref[...]) + _dot(h_im[r], cimn_ref[...])
                          for r in halves], axis=0)
    y = ch + d_ref[...] * u
    y = jax.nn.gelu(y)
    y = y * jax.nn.sigmoid(_dot(y.astype(BF16), wglu_ref[...]))
    y_ref[...] = y.reshape(y_ref.shape)


def ssm_scan(u, u_col, h0_re, h0_im, sp, n_seq, t_len, seq_major):
    t_chunk = min(t_len, SSM_CHUNK)
    tr = t_chunk * n_seq
    s_rows = n_seq * _ssm_pitch(t_chunk) if seq_major else tr
    full = lambda shape: pl.BlockSpec(shape, lambda i: (0,) * len(shape))
    if seq_major:
        u_spec = pl.BlockSpec((n_seq, t_chunk, GROUP_W), lambda i: (0, i, u_col))
        y_spec = pl.BlockSpec((n_seq, t_chunk, GROUP_W), lambda i: (0, i, 0))
        y_shape = jax.ShapeDtypeStruct((n_seq, t_len, GROUP_W), F32)
    else:
        u_spec = pl.BlockSpec((tr, GROUP_W), lambda i: (i, u_col))
        y_spec = pl.BlockSpec((tr, GROUP_W), lambda i: (i, 0))
        y_shape = jax.ShapeDtypeStruct((t_len * n_seq, GROUP_W), F32)
    return pl.pallas_call(
        functools.partial(_ssm_kernel, n_seq, t_chunk, seq_major),
        grid=(t_len // t_chunk,),
        in_specs=[u_spec,
                  full((n_seq, SSM_W)), full((n_seq, SSM_W)),
                  full((1, SSM_W)), full((1, SSM_W)),
                  full((GROUP_W, SSM_W)), full((GROUP_W, SSM_W)),
                  full((SSM_W, GROUP_W)), full((SSM_W, GROUP_W)),
                  full((1, GROUP_W)), full((GROUP_W, GROUP_W))],
        out_specs=[y_spec, full((n_seq, SSM_W)), full((n_seq, SSM_W))],
        out_shape=[y_shape,
                   jax.ShapeDtypeStruct((n_seq, SSM_W), F32),
                   jax.ShapeDtypeStruct((n_seq, SSM_W), F32)],
        scratch_shapes=[pltpu.VMEM((SSM_W // LANES, s_rows, LANES), F32)] * 4,
        compiler_params=_cparams("arbitrary"),
        name="ssm_scan",
    )(u, h0_re, h0_im, sp["lbre"], sp["lbim"],
      sp["bre"], sp["bim"],
      sp["cre"], sp["cimn"], sp["d"], sp["wglu"])


def _softplus(z):
    return jnp.maximum(z, 0.0) + jnp.log(1.0 + jnp.exp(-jnp.abs(z)))


def _softplus2(z2):
    return jnp.maximum(z2, 0.0) + jnp.log(1.0 + jnp.exp2(-jnp.abs(z2))) * LOG2E


def _sb_kernel(tq, t_new, page, n_pages, layer, cp, n_chunks,
               pt_ref, bias_ref,
               qt_ref, k_ref, vt_ref,
               qrep_ref, ck_hbm, cv_hbm, knt_hbm, vnt_hbm,
               o_ref, os_ref,
               z_ref, sp_ref, d_ref, w_ref, acc_ref,
               kbuf, vbuf, sem, qbd_ref, scarry_ref, ws_ref):
    n = pl.program_id(0)
    i = pl.program_id(1)
    nb = pl.num_programs(1)
    slots_per_seq = nb * (nb + 1) // 2
    c_diag = n * slots_per_seq + i * (i + 1) // 2
    chunks_per_seq = n_pages // cp
    n_s = n_chunks // chunks_per_seq
    rows_q = N_HEADS * t_new
    heads = range(N_HEADS)

    def chunk_copies(c, with_new):
        b = c % 3
        bv = c % 4
        seq = c // chunks_per_seq
        sub = c % chunks_per_seq
        if with_new:
            return [pltpu.make_async_copy(knt_hbm.at[seq], kbuf.at[b, cp], sem.at[0, b]),
                    pltpu.make_async_copy(vnt_hbm.at[seq], vbuf.at[bv, cp], sem.at[1, bv])]
        copies = []
        for g in range(cp):
            phys = pt_ref[seq, n_pages - 1 - (sub * cp + g)]
            copies.append(pltpu.make_async_copy(ck_hbm.at[phys, layer], kbuf.at[b, g],
                                                sem.at[0, b]))
            copies.append(pltpu.make_async_copy(cv_hbm.at[phys, layer], vbuf.at[bv, g],
                                                sem.at[1, bv]))
        return copies

    def start_chunk(c):
        @pl.when(c < n_chunks)
        def _():
            for cpy in chunk_copies(c, False):
                cpy.start()

            @pl.when(c % chunks_per_seq == 0)
            def _():
                for cpy in chunk_copies(c, True):
                    cpy.start()

    jk = lax.broadcasted_iota(jnp.int32, (page, page), 0)
    sk = lax.broadcasted_iota(jnp.int32, (page, page), 1)
    suffix_p = jnp.where(jk > sk, -1.0, 0.0).astype(BF16)
    row_head = lax.broadcasted_iota(jnp.int32, (rows_q, 1), 0) // t_new
    bias_col = jnp.zeros((rows_q, 1), F32)
    for h in heads:
        bias_col = jnp.where(row_head == h, bias_ref[h], bias_col)
    bias_rows = jnp.concatenate([bias_col] * cp, axis=0)

    def open_chunk(c):
        @pl.when(c < n_chunks)
        def _():
            for cpy in chunk_copies(c, False):
                cpy.wait()

            @pl.when(c % chunks_per_seq == 0)
            def _():
                for cpy in chunk_copies(c, True):
                    cpy.wait()
                b = c % 3
                seq = c // chunks_per_seq
                lane_head = lax.broadcasted_iota(jnp.int32, (rows_q, GROUP_W), 1) // HEAD_DIM
                qbd = jnp.where(lane_head == row_head, qrep_ref[seq] * ATT_SCALE, 0.0).astype(BF16)
                qbd_ref[...] = qbd
                t_of_row = lax.broadcasted_iota(jnp.int32, (rows_q, page), 0) % t_new
                key = lax.broadcasted_iota(jnp.int32, (rows_q, page), 1)
                z = _dot(qbd, kbuf[b, cp].astype(BF16)) + bias_col
                w, total = _sb_rows_block(z, key < t_of_row, jnp.zeros((rows_q, 1), F32), suffix_p)
                os_ref[seq] = _dot_nt(w.astype(BF16), vbuf[c % 4, cp].astype(BF16))
                scarry_ref[...] = total

    def sample_stages(c):
        b = c % 3
        valid = c < n_chunks
        st = {}

        def prev_values():
            p = jnp.maximum(c - 1, 0)
            ok = jnp.logical_and(c >= 1, p < n_chunks)
            seq = jnp.minimum(p // chunks_per_seq, n_s - 1)
            w_prev = ws_ref[p % 2]
            old = os_ref[seq]
            acc = old
            for g in range(cp):
                acc = acc + _dot_nt(w_prev[g * rows_q:(g + 1) * rows_q, :],
                                    vbuf[p % 4, g].astype(BF16))
            os_ref[seq] = jnp.where(ok, acc, old)

        def scores():
            qbd = qbd_ref[...]
            st["z"] = jnp.concatenate(
                [_dot(qbd, kbuf[b, g].astype(BF16)) for g in range(cp)], axis=0) + bias_rows

        def softplus_and_suffix():
            st["sp"] = _softplus(st["z"])
            st["sp_bf"] = st["sp"].astype(BF16)
            st["local"] = _dot(st["sp_bf"], suffix_p)

        def weights():
            local_s = st["local"]
            totals = local_s[:, 0:1] - st["sp_bf"][:, 0:1].astype(F32)
            old = scarry_ref[...]
            carry = old
            carries_s = []
            for g in range(cp):
                carries_s.append(carry)
                carry = carry + totals[g * rows_q:(g + 1) * rows_q, :]
            scarry_ref[...] = jnp.where(valid, carry, old)
            ws_ref[c % 2] = jnp.exp((st["z"] - st["sp"])
                                    + (local_s + jnp.concatenate(carries_s, axis=0))).astype(BF16)

        return (prev_values, scores, softplus_and_suffix, weights)

    s_idx = lax.broadcasted_iota(jnp.int32, (tq, tq), 0)
    t_idx = lax.broadcasted_iota(jnp.int32, (tq, tq), 1)
    neg_suffix = jnp.where(t_idx > s_idx, -1.0, 0.0).astype(BF16)
    diag_mask = s_idx < t_idx

    def block_of(t):
        return jnp.maximum(i - t, 0)

    def scores(t):
        rows = pl.ds(pl.multiple_of(block_of(t) * tq, tq), tq)
        for h in heads:
            z_ref[t % 2, h] = _dot(k_ref[rows, _head(h)], qt_ref[_head(h), :])

    def front(t, mask):
        for h in heads:
            z = z_ref[t % 2, h] + bias_ref[h] * LOG2E
            sp = _softplus2(z)
            d_ref[t % 2, h] = z - sp
            sp_m = sp if mask is None else jnp.where(mask, sp, 0.0)
            sp_ref[t % 2, h] = sp_m.astype(BF16)

    def suffix_sums(t):
        return [_dot(neg_suffix, sp_ref[t % 2, h]) for h in heads]

    def weights(t, local, carries, mask):
        new_carries = []
        for h in heads:
            w = jnp.exp2(d_ref[t % 2, h] + (local[h] + carries[h]))
            if mask is not None:
                w = jnp.where(mask, w, 0.0)
            w_ref[h] = w.astype(BF16)
            new_carries.append(
                carries[h] + (local[h][0:1, :] - sp_ref[t % 2, h, 0:1, :].astype(F32)))
        return tuple(new_carries)

    def last(t):
        kb = block_of(t)
        for h in heads:
            acc_ref[_head(h), :] += _dot(vt_ref[kb, _head(h), :], w_ref[h])

    @pl.when(c_diag == 0)
    def _():
        start_chunk(c_diag)
        start_chunk(c_diag + 1)
        ws_ref[...] = jnp.zeros(ws_ref.shape, BF16)

    acc_ref[...] = jnp.zeros(acc_ref.shape, F32)
    open_chunk(c_diag)
    scores(0)
    scores(1)
    front(0, diag_mask)

    def trip(t, carries, mask):
        prev_values, sample_scores, sample_suffix, sample_weights = sample_stages(c_diag + t)
        sample_scores()
        local = suffix_sums(t)
        front(t + 1, None)
        sample_suffix()
        prev_values()
        scores(t + 2)
        carries = weights(t, local, carries, mask)
        sample_weights()
        last(t)
        start_chunk(c_diag + t + 2)
        return carries

    zero = jnp.zeros((1, tq), F32)
    carries = trip(0, (zero,) * N_HEADS, diag_mask)

    def body(t, carries):
        open_chunk(c_diag + t)
        return trip(t, carries, None)

    carries = lax.fori_loop(1, i, body, carries)

    @pl.when(i >= 1)
    def _():
        open_chunk(c_diag + i)
        prev_values, sample_scores, sample_suffix, sample_weights = sample_stages(c_diag + i)
        sample_scores()
        local = suffix_sums(i)
        sample_suffix()
        prev_values()
        weights(i, local, carries, None)
        sample_weights()
        last(i)
        start_chunk(c_diag + i + 2)

    o_ref[...] = acc_ref[...].T


def _chunk_pages(n_pages, n_chunks_max):
    for cp in range(1, n_pages + 1):
        if n_pages % cp == 0 and n_pages // cp <= n_chunks_max:
            return cp
    raise ValueError("prompt too short to carry the sample group's pages")


def sb_attention(q_t, k_rm, v_t, q_s, k_new, v_new, bias, cache_kt, cache_vt, page_table, layer,
                 n_seq, t_len, tq):
    nb = t_len // tq
    n_s, t_new, _ = q_s.shape
    page = cache_kt.shape[3]
    n_pages = page_table.shape[1]
    n_slots = n_seq * nb * (nb + 1) // 2
    cp = _chunk_pages(n_pages, (n_slots - 1) // n_s)
    n_chunks = n_s * (n_pages // cp)
    rows_q = N_HEADS * t_new
    q_rep = jnp.tile(q_s, (1, N_HEADS, 1))
    pad = ((0, 0), (0, 0), (0, page - t_new))
    k_new_t = jnp.pad(k_new.transpose(0, 2, 1), pad)
    v_new_t = jnp.pad(v_new.transpose(0, 2, 1), pad)
    full3 = lambda shape: pl.BlockSpec(shape, lambda n, i, pt: (0, 0, 0))
    hbm = pl.BlockSpec(memory_space=pl.ANY)
    y_p, acc = pl.pallas_call(
        functools.partial(_sb_kernel, tq, t_new, page, n_pages, layer, cp, n_chunks),
        grid_spec=pltpu.PrefetchScalarGridSpec(
            num_scalar_prefetch=1,
            grid=(n_seq, nb),
            in_specs=[pl.BlockSpec(memory_space=pltpu.SMEM),
                      pl.BlockSpec((None, None, GROUP_W, tq), lambda n, i, pt: (n, i, 0, 0)),
                      pl.BlockSpec((t_len, GROUP_W), lambda n, i, pt: (n, 0)),
                      pl.BlockSpec((None, nb, GROUP_W, tq), lambda n, i, pt: (n, 0, 0, 0)),
                      full3((n_s, rows_q, GROUP_W)),
                      hbm, hbm, hbm, hbm],
            out_specs=[pl.BlockSpec((tq, GROUP_W), lambda n, i, pt: (n * nb + i, 0)),
                       full3((n_s, rows_q, GROUP_W))],
            scratch_shapes=[pltpu.VMEM((2, N_HEADS, tq, tq), F32),
                            pltpu.VMEM((2, N_HEADS, tq, tq), BF16),
                            pltpu.VMEM((2, N_HEADS, tq, tq), F32),
                            pltpu.VMEM((N_HEADS, tq, tq), BF16),
                            pltpu.VMEM((GROUP_W, tq), F32),
                            pltpu.VMEM((3, cp + 1, GROUP_W, page), F32),
                            pltpu.VMEM((4, cp + 1, GROUP_W, page), F32),
                            pltpu.SemaphoreType.DMA((2, 4)),
                            pltpu.VMEM((rows_q, GROUP_W), BF16),
                            pltpu.VMEM((rows_q, 1), F32),
                            pltpu.VMEM((2, cp * rows_q, page), BF16)]),
        out_shape=[jax.ShapeDtypeStruct((n_seq * t_len, GROUP_W), F32),
                   jax.ShapeDtypeStruct((n_s, rows_q, GROUP_W), F32)],
        compiler_params=_cparams("arbitrary", "arbitrary"),
        name="sb_attention",
    )(page_table, bias, q_t, k_rm, v_t, q_rep, cache_kt, cache_vt, k_new_t, v_new_t)
    acc = acc.reshape(n_s, N_HEADS, t_new, N_HEADS, HEAD_DIM)
    picked = [acc[:, h, :, h, :] for h in range(N_HEADS)]
    return y_p, jnp.stack(picked, axis=2).reshape(n_s, t_new, GROUP_W)


def _sb_rows_block(z, mask, carry, neg_suffix):
    sp = _softplus(z)
    sp_m = sp if mask is None else jnp.where(mask, sp, 0.0)
    sp_bf = sp_m.astype(BF16)
    local = _dot(sp_bf, neg_suffix)
    w = jnp.exp((z - sp) + (local + carry))
    if mask is not None:
        w = jnp.where(mask, w, 0.0)
    return w, local[:, 0:1] - sp_bf[:, 0:1].astype(F32)


def _mem_scores(mq_ref, mkt_of_head):
    return [_dot((mq_ref[:, _head(h)] * ATT_SCALE).astype(BF16), mkt_of_head(h).astype(BF16))
            for h in range(N_HEADS)]


def _mem_values(s, mvt_h):
    e = jnp.exp(s - jnp.max(s, axis=-1, keepdims=True))
    return _dot_nt(e.astype(BF16), mvt_h.astype(BF16)) / jnp.sum(e, axis=-1, keepdims=True)


def _out_part(y, gate, group, gn_ref, wout_ref):
    m = _rms(y, gn_ref[group:group + 1, :]) * _silu(gate)
    return _dot(m.astype(BF16), wout_ref[group * GROUP_W:(group + 1) * GROUP_W, :])


def _conv_taps(v, vm1, vm2, b_gate, cw_ref):
    return b_gate * (vm2 * cw_ref[0:1, :] + vm1 * cw_ref[1:2, :] + v * cw_ref[2:3, :])


def _merge_prompt_kernel(tm, final,
                         x_ref, ag_ref, bb_ref, bc_ref, bx_ref, bg_ref, cg_ref, mq_ref, mg_ref,
                         hbc_ref, hbx_ref, buf_ref, ya_ref, yc_ref, mkt_ref, mvt_ref,
                         gn_ref, cw_ref, wout_ref, fg_ref,
                         o_ref, tail_ref):
    i = pl.program_id(1)
    scores = _mem_scores(mq_ref, lambda h: mkt_ref[_head(h), :])
    y_m = jnp.concatenate([_mem_values(scores[h], mvt_ref[_head(h), :]) for h in range(N_HEADS)],
                          axis=-1)
    part = _out_part(ya_ref[...], ag_ref[...], 0, gn_ref, wout_ref)
    part = part + _out_part(yc_ref[...], cg_ref[...], 2, gn_ref, wout_ref)

    v = bc_ref[...] * bx_ref[...]
    halo = hbc_ref[...] * hbx_ref[...]
    first = i == 0
    prev1 = jnp.where(first, buf_ref[1:2, :], halo[7:8, :])
    prev2 = jnp.where(first, buf_ref[0:1, :], halo[6:7, :])
    row = lax.broadcasted_iota(jnp.int32, (tm, 1), 0)
    vm1 = jnp.where(row == 0, prev1, pltpu.roll(v, 1, 0))
    vm2 = jnp.where(row == 0, prev2, jnp.where(row == 1, prev1, pltpu.roll(v, 2, 0)))
    y_b = _conv_taps(v, vm1, vm2, bb_ref[...], cw_ref)
    tail_ref[...] = v[tm - 8:tm, :]
    part = part + _out_part(y_b, bg_ref[...], 1, gn_ref, wout_ref)
    part = part + _out_part(y_m, mg_ref[...], 3, gn_ref, wout_ref)
    x_new = x_ref[...] + part
    if final:
        x_new = _rms(x_new, fg_ref[...])
    o_ref[...] = x_new


def merge_prompt(x2d, h2d, y_a, y_c, conv_buf, mkv_t, gn, conv_w, wout_bf16, final_g,
                 n_seq, t_len, final):
    d = x2d.shape[1]
    n_mem = mkv_t.shape[2]
    tm = min(t_len, MERGE_ROWS)
    nb = t_len // tm
    col = PROMPT_COL
    hblk = lambda c: pl.BlockSpec((tm, GROUP_W), lambda n, i, c=c: (n * nb + i, col[c]))
    halo = lambda c: pl.BlockSpec(
        (8, GROUP_W), lambda n, i, c=c: (jnp.maximum((n * nb + i) * (tm // 8) - 1, 0), col[c]))
    rows = pl.BlockSpec((tm, GROUP_W), lambda n, i: (n * nb + i, 0))
    full = lambda shape: pl.BlockSpec(shape, lambda n, i: (0,) * len(shape))
    out, tail = pl.pallas_call(
        functools.partial(_merge_prompt_kernel, tm, final),
        grid=(n_seq, nb),
        in_specs=[pl.BlockSpec((tm, d), lambda n, i: (n * nb + i, 0)),
                  hblk(A_G), hblk(B_B), hblk(B_C), hblk(B_X), hblk(B_G), hblk(C_G),
                  hblk(M_Q), hblk(M_G), halo(B_C), halo(B_X),
                  pl.BlockSpec((None, CONV_K - 1, GROUP_W), lambda n, i: (n, 0, 0)),
                  rows, rows,
                  pl.BlockSpec((None, GROUP_W, n_mem), lambda n, i: (n, 0, 0)),
                  pl.BlockSpec((None, GROUP_W, n_mem), lambda n, i: (n, 1, 0)),
                  full((4, GROUP_W)), full((CONV_K, GROUP_W)), full(wout_bf16.shape),
                  full((1, d))],
        out_specs=[pl.BlockSpec((tm, d), lambda n, i: (n * nb + i, 0)),
                   pl.BlockSpec((None, 8, GROUP_W), lambda n, i: (n, 0, 0))],
        out_shape=[jax.ShapeDtypeStruct(x2d.shape, F32),
                   jax.ShapeDtypeStruct((n_seq, 8, GROUP_W), F32)],
        compiler_params=_cparams("parallel", "arbitrary"),
        name="merge_prompt",
    )(x2d, h2d, h2d, h2d, h2d, h2d, h2d, h2d, h2d, h2d, h2d, conv_buf, y_a, y_c, mkv_t, mkv_t,
      gn, conv_w, wout_bf16, final_g.reshape(1, d))
    return out, tail[:, 8 - (CONV_K - 1):, :]


def _merge_sample_kernel(t_len, seqs, final,
                         x_ref, ag_ref, bb_ref, bc_ref, bx_ref, bg_ref, cg_ref, mq_ref, mg_ref,
                         buf_ref, ya_ref, yc_ref, mkt_ref, mvt_ref,
                         gn_ref, cw_ref, wout_ref, fg_ref,
                         o_ref, v_ref):
    tm = seqs * t_len
    row = lax.broadcasted_iota(jnp.int32, (tm, 1), 0)
    seq_of_row = row // t_len
    t_of_row = row % t_len

    v = bc_ref[...] * bx_ref[...]
    v_ref[...] = v
    prev1 = jnp.zeros((tm, GROUP_W), F32)
    prev2 = jnp.zeros((tm, GROUP_W), F32)
    for s in range(seqs):
        prev1 = jnp.where(seq_of_row == s, buf_ref[s, 1:2, :], prev1)
        prev2 = jnp.where(seq_of_row == s, buf_ref[s, 0:1, :], prev2)
    vm1 = jnp.where(t_of_row == 0, prev1, pltpu.roll(v, 1, 0))
    vm2 = jnp.where(t_of_row == 0, prev2, jnp.where(t_of_row == 1, prev1, pltpu.roll(v, 2, 0)))
    y_b = _conv_taps(v, vm1, vm2, bb_ref[...], cw_ref)

    heads = [jnp.zeros((tm, HEAD_DIM), F32)] * N_HEADS
    for s in range(seqs):
        scores = _mem_scores(mq_ref, lambda h, s=s: mkt_ref[s, _head(h), :])
        for h in range(N_HEADS):
            o = _mem_values(scores[h], mvt_ref[s, _head(h), :])
            heads[h] = jnp.where(seq_of_row == s, o, heads[h])
    y_m = jnp.concatenate(heads, axis=-1)

    part = _out_part(ya_ref[...], ag_ref[...], 0, gn_ref, wout_ref)
    part = part + _out_part(y_b, bg_ref[...], 1, gn_ref, wout_ref)
    part = part + _out_part(yc_ref[...], cg_ref[...], 2, gn_ref, wout_ref)
    part = part + _out_part(y_m, mg_ref[...], 3, gn_ref, wout_ref)
    x_new = x_ref[...] + part
    o_ref[...] = _rms(x_new, fg_ref[...]) if final else x_new


def merge_sample(x2d, h2d, y_a, y_c, state_conv, cache_mkt, cache_mvt, gn, conv_w, wout_bf16,
                 final_g, n_seq, t_len, layer, final):
    d = x2d.shape[1]
    n_mem = cache_mkt.shape[3]
    seqs = 16 // t_len
    tm = seqs * t_len
    hblk = lambda c: pl.BlockSpec((tm, GROUP_W), lambda i, c=c: (i, c))
    rows = pl.BlockSpec((tm, GROUP_W), lambda i: (i, 0))
    full = lambda shape: pl.BlockSpec(shape, lambda i: (0,) * len(shape))
    mem = pl.BlockSpec((seqs, None, GROUP_W, n_mem), lambda i: (i, layer, 0, 0))
    return pl.pallas_call(
        functools.partial(_merge_sample_kernel, t_len, seqs, final),
        grid=(n_seq // seqs,),
        in_specs=[pl.BlockSpec((tm, d), lambda i: (i, 0)),
                  hblk(A_G), hblk(B_B), hblk(B_C), hblk(B_X), hblk(B_G), hblk(C_G),
                  hblk(M_Q), hblk(M_G),
                  pl.BlockSpec((seqs, None, CONV_K - 1, GROUP_W), lambda i: (i, layer, 0, 0)),
                  rows, rows, mem, mem,
                  full((4, GROUP_W)), full((CONV_K, GROUP_W)), full(wout_bf16.shape),
                  full((1, d))],
        out_specs=[pl.BlockSpec((tm, d), lambda i: (i, 0)), rows],
        out_shape=[jax.ShapeDtypeStruct(x2d.shape, F32),
                   jax.ShapeDtypeStruct((x2d.shape[0], GROUP_W), F32)],
        compiler_params=_cparams("parallel"),
        name="merge_sample",
    )(x2d, h2d, h2d, h2d, h2d, h2d, h2d, h2d, h2d, state_conv, y_a, y_c, cache_mkt, cache_mvt,
      gn, conv_w, wout_bf16, final_g.reshape(1, d))


def _time_major(a, n_seq, t_len):
    return a.reshape(n_seq, t_len, -1).transpose(1, 0, 2).reshape(n_seq * t_len, -1)


def _seq_major(a, n_seq, t_len):
    return a.reshape(t_len, n_seq, -1).transpose(1, 0, 2).reshape(n_seq * t_len, -1)


def _col(h2d, c):
    return h2d[:, c * GROUP_W:(c + 1) * GROUP_W]


def _pos_minor(a):
    lead = a.shape[:-3]
    pos, heads, dim = a.shape[-3:]
    nd = len(lead)
    return a.transpose(*range(nd), nd + 1, nd + 2, nd).reshape(*lead, heads * dim, pos)


def _from_pos_minor(a_t):
    lead = a_t.shape[:-2]
    pos = a_t.shape[-1]
    nd = len(lead)
    a = a_t.reshape(*lead, N_HEADS, HEAD_DIM, pos)
    return a.transpose(*range(nd), nd + 2, nd, nd + 1)


def kernel(x_prompt, x_sample, cache_sb_k, cache_sb_v, state_ssm_re, state_ssm_im, state_conv,
           cache_mem_k, cache_mem_v, page_table, mem_prompt, norm_g, w_in, w_out, group_norm_g,
           ssm_lambda_re, ssm_lambda_im, ssm_b_re, ssm_b_im, ssm_c_re, ssm_c_im, ssm_log_dt, ssm_d,
           ssm_w_glu, conv_w, sb_bias, w_mem_kv, final_norm_g):
    n_p, l_p, d = x_prompt.shape
    n_s, l_s, _ = x_sample.shape
    depth = w_in.shape[0]
    assert l_s >= CONV_K - 1 and 16 % l_s == 0 and n_p % 8 == 0 and n_s % 8 == 0
    tq = min(l_p, Q_TILE)

    xp = x_prompt.reshape(n_p * l_p, d)
    xs = x_sample.reshape(n_s * l_s, d)
    cache_kt = _pos_minor(cache_sb_k)
    cache_vt = _pos_minor(cache_sb_v)
    cache_mkt = _pos_minor(cache_mem_k)
    cache_mvt = _pos_minor(cache_mem_v)
    zeros_h = jnp.zeros((n_p, SSM_W), F32)
    zeros_buf = jnp.zeros((n_p, CONV_K - 1, GROUP_W), F32)

    kv_t = None
    p_re, p_im, p_conv, p_mkvt = [], [], [], []
    s_k, s_v, s_re, s_im, s_conv = [], [], [], [], []
    for i in range(depth):
        final = i == depth - 1
        w_in_bf = w_in[i].astype(BF16)
        w_out_bf = w_out[i].astype(BF16)

        lbre, lbim, bbre, bbim = ssm_discretise(ssm_lambda_re[i], ssm_lambda_im[i], ssm_log_dt[i],
                                                ssm_b_re[i], ssm_b_im[i])
        sp = {"lbre": lbre, "lbim": lbim,
              "bre": _block_diag_in(bbre).astype(BF16), "bim": _block_diag_in(bbim).astype(BF16),
              "cre": _block_diag_out(ssm_c_re[i]).astype(BF16),
              "cimn": _block_diag_out(-ssm_c_im[i]).astype(BF16),
              "d": ssm_d[i].reshape(1, GROUP_W), "wglu": ssm_w_glu[i].astype(BF16)}

        mkv_t = mem_kv_t(mem_prompt, w_mem_kv[i])
        h_p, k_rm, q_t, v_tb, *kv_t = in_proj_prompt(xp, norm_g[i], w_in_bf, n_p, l_p, tq, i,
                                                     depth, kv_t)
        h_s = in_proj(xs, norm_g[i], w_in_bf)
        seq3 = lambda c: _col(h_s, c).reshape(n_s, l_s, GROUP_W)

        y_c, y_cs = sb_attention(q_t, k_rm, v_tb, seq3(C_Q), seq3(C_K), seq3(C_V), sb_bias[i],
                                 cache_kt, cache_vt, page_table, i, n_p, l_p, tq)

        y_a, h_re, h_im = ssm_scan(h_p.reshape(n_p, l_p, -1), PROMPT_COL[A_U], zeros_h, zeros_h,
                                   sp, n_p, l_p, True)
        y_a = y_a.reshape(n_p * l_p, GROUP_W)
        xp, conv_p = merge_prompt(xp, h_p, y_a, y_c, zeros_buf, mkv_t, group_norm_g[i], conv_w[i],
                                  w_out_bf, final_norm_g, n_p, l_p, final)
        p_re.append(h_re.reshape(n_p, SSM_GROUPS, SSM_STATE))
        p_im.append(h_im.reshape(n_p, SSM_GROUPS, SSM_STATE))
        p_conv.append(conv_p)
        p_mkvt.append(mkv_t)

        ya_tm, h_re, h_im = ssm_scan(_time_major(_col(h_s, A_U), n_s, l_s), 0,
                                     state_ssm_re[:, i].reshape(n_s, SSM_W),
                                     state_ssm_im[:, i].reshape(n_s, SSM_W), sp, n_s, l_s, False)
        y_a = _seq_major(ya_tm, n_s, l_s)
        xs, v_conv = merge_sample(xs, h_s, y_a, y_cs.reshape(n_s * l_s, GROUP_W), state_conv,
                                  cache_mkt, cache_mvt, group_norm_g[i], conv_w[i], w_out_bf,
                                  final_norm_g, n_s, l_s, i, final)
        s_k.append(seq3(C_K).reshape(n_s, l_s, N_HEADS, HEAD_DIM))
        s_v.append(seq3(C_V).reshape(n_s, l_s, N_HEADS, HEAD_DIM))
        s_re.append(h_re.reshape(n_s, SSM_GROUPS, SSM_STATE))
        s_im.append(h_im.reshape(n_s, SSM_GROUPS, SSM_STATE))
        s_conv.append(v_conv.reshape(n_s, l_s, GROUP_W)[:, l_s - (CONV_K - 1):, :])

    stack = lambda xs_: jnp.stack(xs_, axis=1)
    mkv = stack(p_mkvt)
    return (xp.reshape(n_p, l_p, d), xs.reshape(n_s, l_s, d),
            _from_pos_minor(kv_t[0]), _from_pos_minor(kv_t[1]),
            stack(p_re), stack(p_im), stack(p_conv),
            _from_pos_minor(mkv[:, :, :GROUP_W]), _from_pos_minor(mkv[:, :, GROUP_W:]),
            stack(s_k), stack(s_v), stack(s_re), stack(s_im), stack(s_conv))
```

```python
import functools

import jax
import jax.numpy as jnp
from jax import lax
from jax.experimental import pallas as pl
from jax.experimental.pallas import tpu as pltpu

F32 = jnp.float32
BF16 = jnp.bfloat16

EPS = 1e-6
GROUP_W = 256
N_IN_BLOCKS = 12
HEAD_DIM = 64
N_HEADS = 4
ATT_SCALE = HEAD_DIM ** -0.5
LOG2E = 1.4426950408889634
SSM_GROUPS = 16
SSM_CH = 16
SSM_STATE = 64
SSM_W = SSM_GROUPS * SSM_STATE
CONV_K = 3
LANES = 128

(A_U, A_G, B_B, B_C, B_X, B_G, C_Q, C_K, C_V, C_G, M_Q, M_G) = range(N_IN_BLOCKS)
PROMPT_BLOCKS = (A_U, A_G, B_B, B_C, B_X, B_G, C_G, M_Q, M_G)
PROMPT_COL = {b: j for j, b in enumerate(PROMPT_BLOCKS)}

VMEM_LIMIT = 48 * 1024 * 1024
Q_TILE = 256
ROW_TILE = 256
MERGE_ROWS = 512
SSM_CHUNK = 128
CHUNK_LEAD = 4
K_RING = CHUNK_LEAD + 1
V_RING = CHUNK_LEAD + 2


def _cparams(*sem):
    return pltpu.CompilerParams(dimension_semantics=sem, vmem_limit_bytes=VMEM_LIMIT)


def _dot(a, b):
    return jnp.dot(a, b, preferred_element_type=F32)


def _dot_nt(a, b):
    return lax.dot_general(a, b, (((1,), (1,)), ((), ())), preferred_element_type=F32)


def _rms(x, g):
    return x * lax.rsqrt(jnp.mean(x * x, axis=-1, keepdims=True) + EPS) * g


def _silu(x):
    return x * jax.nn.sigmoid(x)


def _head(h):
    return slice(h * HEAD_DIM, (h + 1) * HEAD_DIM)


def _inproj_kernel(x_ref, g_ref, w_ref, o_ref):
    xn = _rms(x_ref[...], g_ref[...])
    o_ref[...] = _dot(xn.astype(BF16), w_ref[...])


def in_proj(x2d, g, w_bf16):
    rows, d = x2d.shape
    c = w_bf16.shape[1]
    tm = min(rows, ROW_TILE)
    return pl.pallas_call(
        _inproj_kernel,
        grid=(rows // tm,),
        in_specs=[pl.BlockSpec((tm, d), lambda i: (i, 0)),
                  pl.BlockSpec((1, d), lambda i: (0, 0)),
                  pl.BlockSpec((d, c), lambda i: (0, 0))],
        out_specs=pl.BlockSpec((tm, c), lambda i: (i, 0)),
        out_shape=jax.ShapeDtypeStruct((rows, c), F32),
        compiler_params=_cparams("parallel"),
        name="in_proj",
    )(x2d, g.reshape(1, d), w_bf16)


def _inproj_prompt_kernel(tq, x_ref, g_ref, w_ref, *refs):
    h_ref, krm_ref, qt_ref, vtb_ref, kt_ref, vt_ref = refs[-6:]
    blk = lambda b: slice(b * GROUP_W, (b + 1) * GROUP_W)
    xn = _rms(x_ref[...], g_ref[...]).astype(BF16)
    hm = _dot(xn, w_ref[...])
    h_ref[...] = jnp.concatenate([hm[:, blk(b)] for b in PROMPT_BLOCKS], axis=1)
    k = hm[:, blk(C_K)]
    krm_ref[...] = k.astype(BF16)
    q = hm[:, blk(C_Q)] * (ATT_SCALE * LOG2E)
    v = hm[:, blk(C_V)]
    slots = range(kt_ref.shape[0]) if len(kt_ref.shape) == 3 else (None,)
    for j in range(x_ref.shape[0] // tq):
        rows = slice(j * tq, (j + 1) * tq)
        k_t = k[rows].T
        v_t = v[rows].T
        for slot in slots:
            idx = (slice(None), rows) if slot is None else (slot, slice(None), rows)
            kt_ref[idx] = k_t
            vt_ref[idx] = v_t
        qt_ref[j] = q[rows].T.astype(BF16)
        vtb_ref[j] = v_t.astype(BF16)


def in_proj_prompt(x2d, g, w_bf16, n_seq, t_len, tq, layer, depth, kv_t):
    rows, d = x2d.shape
    per = 2 if t_len % (2 * tq) == 0 else 1
    tm = per * tq
    nb = t_len // tm
    n_main = len(PROMPT_BLOCKS) * GROUP_W
    full = lambda shape: pl.BlockSpec(shape, lambda n, i: (0,) * len(shape))
    blocked = pl.BlockSpec((None, per, GROUP_W, tq), lambda n, i: (n, i, 0, 0))
    prior = () if kv_t is None else tuple(kv_t)
    if prior:
        final_t = pl.BlockSpec((None, None, GROUP_W, tm), lambda n, i: (n, layer, 0, i))
    else:
        final_t = pl.BlockSpec((None, depth, GROUP_W, tm), lambda n, i: (n, 0, 0, i))
    kv_shape = jax.ShapeDtypeStruct((n_seq, depth, GROUP_W, t_len), F32)
    return pl.pallas_call(
        functools.partial(_inproj_prompt_kernel, tq),
        grid=(n_seq, nb),
        in_specs=[pl.BlockSpec((tm, d), lambda n, i: (n * nb + i, 0)),
                  full((1, d)), full(w_bf16.shape)]
        + [pl.BlockSpec(memory_space=pl.ANY)] * len(prior),
        out_specs=[pl.BlockSpec((tm, n_main), lambda n, i: (n * nb + i, 0)),
                   pl.BlockSpec((tm, GROUP_W), lambda n, i: (n * nb + i, 0)),
                   blocked, blocked, final_t, final_t],
        out_shape=[jax.ShapeDtypeStruct((rows, n_main), F32),
                   jax.ShapeDtypeStruct((rows, GROUP_W), BF16),
                   jax.ShapeDtypeStruct((n_seq, t_len // tq, GROUP_W, tq), BF16),
                   jax.ShapeDtypeStruct((n_seq, t_len // tq, GROUP_W, tq), BF16),
                   kv_shape, kv_shape],
        input_output_aliases={3: 4, 4: 5} if prior else {},
        compiler_params=_cparams("parallel", "parallel"),
        name="in_proj_prompt",
    )(x2d, g.reshape(1, d), w_bf16, *prior)


def _memkv_kernel(x_ref, wt_ref, o_ref):
    o_ref[...] = _dot_nt(wt_ref[...], x_ref[...].astype(BF16))


def mem_kv_t(mem, w_mem):
    n_seq, n_mem, d = mem.shape
    w_t = w_mem.T.astype(BF16)
    return pl.pallas_call(
        _memkv_kernel,
        grid=(n_seq,),
        in_specs=[pl.BlockSpec((None, n_mem, d), lambda n: (n, 0, 0)),
                  pl.BlockSpec(w_t.shape, lambda n: (0, 0))],
        out_specs=pl.BlockSpec((None, w_t.shape[0], n_mem), lambda n: (n, 0, 0)),
        out_shape=jax.ShapeDtypeStruct((n_seq, w_t.shape[0], n_mem), F32),
        compiler_params=_cparams("parallel"),
        name="mem_kv",
    )(mem, w_t)


def _ssm_disc_kernel(lre_ref, lim_ref, dt_ref, bre_ref, bim_ref,
                     lbre_ref, lbim_ref, bbre_ref, bbim_ref):
    lre = lre_ref[...]
    lim = lim_ref[...]
    dt = jnp.exp(dt_ref[...])
    mag = jnp.exp(lre * dt)
    lbre = mag * jnp.cos(lim * dt)
    lbim = mag * jnp.sin(lim * dt)
    lbre_ref[...] = lbre
    lbim_ref[...] = lbim
    nre = lbre - 1.0
    nim = lbim
    den = lre * lre + lim * lim
    cre = (nre * lre + nim * lim) / den
    cim = (nim * lre - nre * lim) / den
    bre = bre_ref[...]
    bim = bim_ref[...]
    bbre_ref[...] = cre * bre - cim * bim
    bbim_ref[...] = cre * bim + cim * bre


def ssm_discretise(lam_re, lam_im, log_dt, b_re, b_im):
    col = lambda a: a.reshape(SSM_W, 1)
    dt_col = jnp.broadcast_to(log_dt[:, None], (SSM_GROUPS, SSM_STATE)).reshape(SSM_W, 1)
    outs = pl.pallas_call(
        _ssm_disc_kernel,
        out_shape=[jax.ShapeDtypeStruct((SSM_W, 1), F32)] * 2
        + [jax.ShapeDtypeStruct((SSM_W, SSM_CH), F32)] * 2,
        name="ssm_discretise",
    )(col(lam_re), col(lam_im), dt_col, b_re.reshape(SSM_W, SSM_CH), b_im.reshape(SSM_W, SSM_CH))
    lbre, lbim, bbre, bbim = outs
    shp = (SSM_GROUPS, SSM_STATE, SSM_CH)
    return lbre.reshape(1, SSM_W), lbim.reshape(1, SSM_W), bbre.reshape(shp), bbim.reshape(shp)


def _block_diag_in(b_gpc):
    eye = jnp.eye(SSM_GROUPS, dtype=F32)
    m = b_gpc.transpose(0, 2, 1)[:, :, None, :] * eye[:, None, :, None]
    return m.reshape(SSM_GROUPS * SSM_CH, SSM_W)


def _block_diag_out(c_gcp):
    eye = jnp.eye(SSM_GROUPS, dtype=F32)
    m = c_gcp.transpose(0, 2, 1)[:, :, None, :] * eye[:, None, :, None]
    return m.reshape(SSM_W, SSM_GROUPS * SSM_CH)


def _ssm_pitch(t_chunk):
    p = -(-t_chunk // 8)
    return 8 * (p if p % 2 else p + 1)


def _ssm_kernel(n_seq, t_chunk, seq_major,
                u_ref, h0re_ref, h0im_ref, lre_ref, lim_ref,
                bre_ref, bim_ref,
                cre_ref, cimn_ref, d_ref, wglu_ref,
                y_ref, hre_ref, him_ref,
                bu_re, bu_im, hs_re, hs_im):
    step = pl.program_id(0)
    rows = n_seq * t_chunk

    @pl.when(step == 0)
    def _():
        hre_ref[...] = h0re_ref[...]
        him_ref[...] = h0im_ref[...]

    n_slab = SSM_W // LANES
    slab = lambda j: slice(j * LANES, (j + 1) * LANES)
    pitch = _ssm_pitch(t_chunk) if seq_major else None

    def put(ref, j, val):
        if not seq_major:
            ref[j] = val
            return
        for n in range(n_seq):
            ref[j, n * pitch:n * pitch + t_chunk, :] = val[n * t_chunk:(n + 1) * t_chunk, :]

    def get(ref, j):
        if not seq_major:
            return ref[j]
        return jnp.concatenate(
            [ref[j, n * pitch:n * pitch + t_chunk, :] for n in range(n_seq)], axis=0)

    def rows_of(t):
        if seq_major:
            return pl.ds(t, n_seq, stride=pitch)
        return pl.ds(pl.multiple_of(t * n_seq, n_seq), n_seq)

    halves = [slice(0, rows // 2), slice(rows // 2, rows)] if rows % 16 == 0 else [slice(0, rows)]
    u = u_ref[...].reshape(rows, GROUP_W)
    u_bf = u.astype(BF16)
    b_re = jnp.concatenate([_dot(u_bf[r], bre_ref[...]) for r in halves], axis=0)
    b_im = jnp.concatenate([_dot(u_bf[r], bim_ref[...]) for r in halves], axis=0)
    for j in range(n_slab):
        put(bu_re, j, b_re[:, slab(j)])
        put(bu_im, j, b_im[:, slab(j)])

    lre = [jnp.broadcast_to(lre_ref[:, slab(j)], (n_seq, LANES)) for j in range(n_slab)]
    lim = [jnp.broadcast_to(lim_ref[:, slab(j)], (n_seq, LANES)) for j in range(n_slab)]

    def body(t, carry):
        r = rows_of(t)
        new = []
        for j in range(n_slab):
            hr, hi = carry[j]
            nr = lre[j] * hr - lim[j] * hi + bu_re[j, r, :]
            ni = lre[j] * hi + lim[j] * hr + bu_im[j, r, :]
            hs_re[j, r, :] = nr
            hs_im[j, r, :] = ni
            new.append((nr, ni))
        return tuple(new)

    h0 = tuple((hre_ref[:, slab(j)], him_ref[:, slab(j)]) for j in range(n_slab))
    h_last = lax.fori_loop(0, t_chunk, body, h0, unroll=4)
    for j in range(n_slab):
        hre_ref[:, slab(j)] = h_last[j][0]
        him_ref[:, slab(j)] = h_last[j][1]

    h_re = jnp.concatenate([get(hs_re, j) for j in range(n_slab)], axis=-1).astype(BF16)
    h_im = jnp.concatenate([get(hs_im, j) for j in range(n_slab)], axis=-1).astype(BF16)
    ch = jnp.concatenate([_dot(h_re[r], cre_ref[...]) + _dot(h_im[r], cimn_ref[...])
                          for r in halves], axis=0)
    y = ch + d_ref[...] * u
    y = jax.nn.gelu(y)
    y = y * jax.nn.sigmoid(_dot(y.astype(BF16), wglu_ref[...]))
    y_ref[...] = y.reshape(y_ref.shape)


def ssm_scan(u, u_col, h0_re, h0_im, sp, n_seq, t_len, seq_major):
    t_chunk = min(t_len, SSM_CHUNK)
    tr = t_chunk * n_seq
    s_rows = n_seq * _ssm_pitch(t_chunk) if seq_major else tr
    full = lambda shape: pl.BlockSpec(shape, lambda i: (0,) * len(shape))
    if seq_major:
        u_spec = pl.BlockSpec((n_seq, t_chunk, GROUP_W), lambda i: (0, i, u_col))
        y_spec = pl.BlockSpec((n_seq, t_chunk, GROUP_W), lambda i: (0, i, 0))
        y_shape = jax.ShapeDtypeStruct((n_seq, t_len, GROUP_W), F32)
    else:
        u_spec = pl.BlockSpec((tr, GROUP_W), lambda i: (i, u_col))
        y_spec = pl.BlockSpec((tr, GROUP_W), lambda i: (i, 0))
        y_shape = jax.ShapeDtypeStruct((t_len * n_seq, GROUP_W), F32)
    return pl.pallas_call(
        functools.partial(_ssm_kernel, n_seq, t_chunk, seq_major),
        grid=(t_len // t_chunk,),
        in_specs=[u_spec,
                  full((n_seq, SSM_W)), full((n_seq, SSM_W)),
                  full((1, SSM_W)), full((1, SSM_W)),
                  full((GROUP_W, SSM_W)), full((GROUP_W, SSM_W)),
                  full((SSM_W, GROUP_W)), full((SSM_W, GROUP_W)),
                  full((1, GROUP_W)), full((GROUP_W, GROUP_W))],
        out_specs=[y_spec, full((n_seq, SSM_W)), full((n_seq, SSM_W))],
        out_shape=[y_shape,
                   jax.ShapeDtypeStruct((n_seq, SSM_W), F32),
                   jax.ShapeDtypeStruct((n_seq, SSM_W), F32)],
        scratch_shapes=[pltpu.VMEM((SSM_W // LANES, s_rows, LANES), F32)] * 4,
        compiler_params=_cparams("arbitrary"),
        name="ssm_scan",
    )(u, h0_re, h0_im, sp["lbre"], sp["lbim"],
      sp["bre"], sp["bim"],
      sp["cre"], sp["cimn"], sp["d"], sp["wglu"])


def _softplus(z):
    return jnp.maximum(z, 0.0) + jnp.log(1.0 + jnp.exp(-jnp.abs(z)))


def _softplus2(z2):
    return jnp.maximum(z2, 0.0) + jnp.log(1.0 + jnp.exp2(-jnp.abs(z2))) * LOG2E


def _sb_kernel(tq, t_new, page, n_pages, layer, cp, n_chunks,
               pt_ref, bias_ref,
               qt_ref, k_ref, vt_ref,
               qrep_ref, ck_hbm, cv_hbm, knt_hbm, vnt_hbm,
               o_ref, os_ref,
               z_ref, sp_ref, d_ref, w_ref, acc_ref,
               kbuf, vbuf, sem, qbd_ref, scarry_ref, ws_ref):
    n = pl.program_id(0)
    i = pl.program_id(1)
    nb = pl.num_programs(1)
    slots_per_seq = nb * (nb + 1) // 2
    c_diag = n * slots_per_seq + i * (i + 1) // 2
    chunks_per_seq = n_pages // cp
    n_s = n_chunks // chunks_per_seq
    rows_q = N_HEADS * t_new
    heads = range(N_HEADS)

    def chunk_copies(c, with_new):
        b = c % K_RING
        bv = c % V_RING
        seq = c // chunks_per_seq
        sub = c % chunks_per_seq
        if with_new:
            return [pltpu.make_async_copy(knt_hbm.at[seq], kbuf.at[b, cp], sem.at[0, b]),
                    pltpu.make_async_copy(vnt_hbm.at[seq], vbuf.at[bv, cp], sem.at[1, bv])]
        copies = []
        for g in range(cp):
            phys = pt_ref[seq, n_pages - 1 - (sub * cp + g)]
            copies.append(pltpu.make_async_copy(ck_hbm.at[phys, layer], kbuf.at[b, g],
                                                sem.at[0, b]))
            copies.append(pltpu.make_async_copy(cv_hbm.at[phys, layer], vbuf.at[bv, g],
                                                sem.at[1, bv]))
        return copies

    def start_chunk(c):
        @pl.when(c < n_chunks)
        def _():
            for cpy in chunk_copies(c, False):
                cpy.start()

            @pl.when(c % chunks_per_seq == 0)
            def _():
                for cpy in chunk_copies(c, True):
                    cpy.start()

    jk = lax.broadcasted_iota(jnp.int32, (page, page), 0)
    sk = lax.broadcasted_iota(jnp.int32, (page, page), 1)
    suffix_p = jnp.where(jk > sk, -1.0, 0.0).astype(BF16)
    row_head = lax.broadcasted_iota(jnp.int32, (rows_q, 1), 0) // t_new
    bias_col = jnp.zeros((rows_q, 1), F32)
    for h in heads:
        bias_col = jnp.where(row_head == h, bias_ref[h], bias_col)
    bias_rows = jnp.concatenate([bias_col] * cp, axis=0)

    def open_chunk(c):
        @pl.when(c < n_chunks)
        def _():
            for cpy in chunk_copies(c, False):
                cpy.wait()

            @pl.when(c % chunks_per_seq == 0)
            def _():
                for cpy in chunk_copies(c, True):
                    cpy.wait()
                b = c % K_RING
                seq = c // chunks_per_seq
                lane_head = lax.broadcasted_iota(jnp.int32, (rows_q, GROUP_W), 1) // HEAD_DIM
                qbd = jnp.where(lane_head == row_head, qrep_ref[seq] * ATT_SCALE, 0.0).astype(BF16)
                qbd_ref[...] = qbd
                t_of_row = lax.broadcasted_iota(jnp.int32, (rows_q, page), 0) % t_new
                key = lax.broadcasted_iota(jnp.int32, (rows_q, page), 1)
                z = _dot(qbd, kbuf[b, cp].astype(BF16)) + bias_col
                w, total = _sb_rows_block(z, key < t_of_row, jnp.zeros((rows_q, 1), F32), suffix_p)
                os_ref[seq] = _dot_nt(w.astype(BF16), vbuf[c % V_RING, cp].astype(BF16))
                scarry_ref[...] = total

    def sample_stages(c):
        b = c % K_RING
        valid = c < n_chunks
        st = {}

        def prev_values():
            p = jnp.maximum(c - 1, 0)
            ok = jnp.logical_and(c >= 1, p < n_chunks)
            seq = jnp.minimum(p // chunks_per_seq, n_s - 1)
            w_prev = ws_ref[p % 2]
            old = os_ref[seq]
            acc = old
            for g in range(cp):
                acc = acc + _dot_nt(w_prev[g * rows_q:(g + 1) * rows_q, :],
                                    vbuf[p % V_RING, g].astype(BF16))
            os_ref[seq] = jnp.where(ok, acc, old)

        def scores():
            qbd = qbd_ref[...]
            st["z"] = jnp.concatenate(
                [_dot(qbd, kbuf[b, g].astype(BF16)) for g in range(cp)], axis=0) + bias_rows

        def softplus_and_suffix():
            st["sp"] = _softplus(st["z"])
            st["sp_bf"] = st["sp"].astype(BF16)
            st["local"] = _dot(st["sp_bf"], suffix_p)

        def weights():
            local_s = st["local"]
            totals = local_s[:, 0:1] - st["sp_bf"][:, 0:1].astype(F32)
            old = scarry_ref[...]
            carry = old
            carries_s = []
            for g in range(cp):
                carries_s.append(carry)
                carry = carry + totals[g * rows_q:(g + 1) * rows_q, :]
            scarry_ref[...] = jnp.where(valid, carry, old)
            ws_ref[c % 2] = jnp.exp((st["z"] - st["sp"])
                                    + (local_s + jnp.concatenate(carries_s, axis=0))).astype(BF16)

        return (prev_values, scores, softplus_and_suffix, weights)

    s_idx = lax.broadcasted_iota(jnp.int32, (tq, tq), 0)
    t_idx = lax.broadcasted_iota(jnp.int32, (tq, tq), 1)
    neg_suffix = jnp.where(t_idx > s_idx, -1.0, 0.0).astype(BF16)
    diag_mask = s_idx < t_idx

    def block_of(t):
        return jnp.maximum(i - t, 0)

    def scores(t):
        rows = pl.ds(pl.multiple_of(block_of(t) * tq, tq), tq)
        for h in heads:
            z_ref[t % 2, h] = _dot(k_ref[rows, _head(h)], qt_ref[_head(h), :])

    def front(t, mask):
        for h in heads:
            z = z_ref[t % 2, h] + bias_ref[h] * LOG2E
            sp = _softplus2(z)
            d_ref[t % 2, h] = z - sp
            sp_m = sp if mask is None else jnp.where(mask, sp, 0.0)
            sp_ref[t % 2, h] = sp_m.astype(BF16)

    def suffix_sums(t):
        return [_dot(neg_suffix, sp_ref[t % 2, h]) for h in heads]

    def weights(t, local, carries, mask):
        new_carries = []
        for h in heads:
            w = jnp.exp2(d_ref[t % 2, h] + (local[h] + carries[h]))
            if mask is not None:
                w = jnp.where(mask, w, 0.0)
            w_ref[h] = w.astype(BF16)
            new_carries.append(
                carries[h] + (local[h][0:1, :] - sp_ref[t % 2, h, 0:1, :].astype(F32)))
        return tuple(new_carries)

    def last(t):
        kb = block_of(t)
        for h in heads:
            acc_ref[_head(h), :] += _dot(vt_ref[kb, _head(h), :], w_ref[h])

    @pl.when(c_diag == 0)
    def _():
        for ahead in range(CHUNK_LEAD):
            start_chunk(c_diag + ahead)
        ws_ref[...] = jnp.zeros(ws_ref.shape, BF16)

    acc_ref[...] = jnp.zeros(acc_ref.shape, F32)
    open_chunk(c_diag)
    scores(0)
    scores(1)
    front(0, diag_mask)

    def trip(t, carries, mask):
        prev_values, sample_scores, sample_suffix, sample_weights = sample_stages(c_diag + t)
        sample_scores()
        local = suffix_sums(t)
        front(t + 1, None)
        sample_suffix()
        prev_values()
        scores(t + 2)
        carries = weights(t, local, carries, mask)
        sample_weights()
        last(t)
        start_chunk(c_diag + t + CHUNK_LEAD)
        return carries

    zero = jnp.zeros((1, tq), F32)
    carries = trip(0, (zero,) * N_HEADS, diag_mask)

    def body(t, carries):
        open_chunk(c_diag + t)
        return trip(t, carries, None)

    carries = lax.fori_loop(1, i, body, carries)

    @pl.when(i >= 1)
    def _():
        open_chunk(c_diag + i)
        prev_values, sample_scores, sample_suffix, sample_weights = sample_stages(c_diag + i)
        sample_scores()
        local = suffix_sums(i)
        sample_suffix()
        prev_values()
        weights(i, local, carries, None)
        sample_weights()
        last(i)
        start_chunk(c_diag + i + CHUNK_LEAD)

    o_ref[...] = acc_ref[...].T


def _chunk_pages(n_pages, n_chunks_max):
    for cp in range(1, n_pages + 1):
        if n_pages % cp == 0 and n_pages // cp <= n_chunks_max:
            return cp
    raise ValueError("prompt too short to carry the sample group's pages")


def sb_attention(q_t, k_rm, v_t, q_s, k_new, v_new, bias, cache_kt, cache_vt, page_table, layer,
                 n_seq, t_len, tq):
    nb = t_len // tq
    n_s, t_new, _ = q_s.shape
    page = cache_kt.shape[3]
    n_pages = page_table.shape[1]
    n_slots = n_seq * nb * (nb + 1) // 2
    cp = _chunk_pages(n_pages, (n_slots - 1) // n_s)
    n_chunks = n_s * (n_pages // cp)
    rows_q = N_HEADS * t_new
    q_rep = jnp.tile(q_s, (1, N_HEADS, 1))
    pad = ((0, 0), (0, 0), (0, page - t_new))
    k_new_t = jnp.pad(k_new.transpose(0, 2, 1), pad)
    v_new_t = jnp.pad(v_new.transpose(0, 2, 1), pad)
    full3 = lambda shape: pl.BlockSpec(shape, lambda n, i, pt: (0, 0, 0))
    hbm = pl.BlockSpec(memory_space=pl.ANY)
    y_p, acc = pl.pallas_call(
        functools.partial(_sb_kernel, tq, t_new, page, n_pages, layer, cp, n_chunks),
        grid_spec=pltpu.PrefetchScalarGridSpec(
            num_scalar_prefetch=1,
            grid=(n_seq, nb),
            in_specs=[pl.BlockSpec(memory_space=pltpu.SMEM),
                      pl.BlockSpec((None, None, GROUP_W, tq), lambda n, i, pt: (n, i, 0, 0)),
                      pl.BlockSpec((t_len, GROUP_W), lambda n, i, pt: (n, 0)),
                      pl.BlockSpec((None, nb, GROUP_W, tq), lambda n, i, pt: (n, 0, 0, 0)),
                      full3((n_s, rows_q, GROUP_W)),
                      hbm, hbm, hbm, hbm],
            out_specs=[pl.BlockSpec((tq, GROUP_W), lambda n, i, pt: (n * nb + i, 0)),
                       full3((n_s, rows_q, GROUP_W))],
            scratch_shapes=[pltpu.VMEM((2, N_HEADS, tq, tq), F32),
                            pltpu.VMEM((2, N_HEADS, tq, tq), BF16),
                            pltpu.VMEM((2, N_HEADS, tq, tq), F32),
                            pltpu.VMEM((N_HEADS, tq, tq), BF16),
                            pltpu.VMEM((GROUP_W, tq), F32),
                            pltpu.VMEM((K_RING, cp + 1, GROUP_W, page), F32),
                            pltpu.VMEM((V_RING, cp + 1, GROUP_W, page), F32),
                            pltpu.SemaphoreType.DMA((2, V_RING)),
                            pltpu.VMEM((rows_q, GROUP_W), BF16),
                            pltpu.VMEM((rows_q, 1), F32),
                            pltpu.VMEM((2, cp * rows_q, page), BF16)]),
        out_shape=[jax.ShapeDtypeStruct((n_seq * t_len, GROUP_W), F32),
                   jax.ShapeDtypeStruct((n_s, rows_q, GROUP_W), F32)],
        compiler_params=_cparams("arbitrary", "arbitrary"),
        name="sb_attention",
    )(page_table, bias, q_t, k_rm, v_t, q_rep, cache_kt, cache_vt, k_new_t, v_new_t)
    acc = acc.reshape(n_s, N_HEADS, t_new, N_HEADS, HEAD_DIM)
    picked = [acc[:, h, :, h, :] for h in range(N_HEADS)]
    return y_p, jnp.stack(picked, axis=2).reshape(n_s, t_new, GROUP_W)


def _sb_rows_block(z, mask, carry, neg_suffix):
    sp = _softplus(z)
    sp_m = sp if mask is None else jnp.where(mask, sp, 0.0)
    sp_bf = sp_m.astype(BF16)
    local = _dot(sp_bf, neg_suffix)
    w = jnp.exp((z - sp) + (local + carry))
    if mask is not None:
        w = jnp.where(mask, w, 0.0)
    return w, local[:, 0:1] - sp_bf[:, 0:1].astype(F32)


def _mem_scores(mq_ref, mkt_of_head):
    return [_dot((mq_ref[:, _head(h)] * ATT_SCALE).astype(BF16), mkt_of_head(h).astype(BF16))
            for h in range(N_HEADS)]


def _mem_values(s, mvt_h):
    e = jnp.exp(s - jnp.max(s, axis=-1, keepdims=True))
    return _dot_nt(e.astype(BF16), mvt_h.astype(BF16)) / jnp.sum(e, axis=-1, keepdims=True)


def _out_part(y, gate, group, gn_ref, wout_ref):
    m = _rms(y, gn_ref[group:group + 1, :]) * _silu(gate)
    return _dot(m.astype(BF16), wout_ref[group * GROUP_W:(group + 1) * GROUP_W, :])


def _conv_taps(v, vm1, vm2, b_gate, cw_ref):
    return b_gate * (vm2 * cw_ref[0:1, :] + vm1 * cw_ref[1:2, :] + v * cw_ref[2:3, :])


def _merge_prompt_kernel(tm, final,
                         x_ref, ag_ref, bb_ref, bc_ref, bx_ref, bg_ref, cg_ref, mq_ref, mg_ref,
                         hbc_ref, hbx_ref, buf_ref, ya_ref, yc_ref, mkt_ref, mvt_ref,
                         gn_ref, cw_ref, wout_ref, fg_ref,
                         o_ref, tail_ref):
    i = pl.program_id(1)
    scores = _mem_scores(mq_ref, lambda h: mkt_ref[_head(h), :])
    y_m = jnp.concatenate([_mem_values(scores[h], mvt_ref[_head(h), :]) for h in range(N_HEADS)],
                          axis=-1)
    part = _out_part(ya_ref[...], ag_ref[...], 0, gn_ref, wout_ref)
    part = part + _out_part(yc_ref[...], cg_ref[...], 2, gn_ref, wout_ref)

    v = bc_ref[...] * bx_ref[...]
    halo = hbc_ref[...] * hbx_ref[...]
    first = i == 0
    prev1 = jnp.where(first, buf_ref[1:2, :], halo[7:8, :])
    prev2 = jnp.where(first, buf_ref[0:1, :], halo[6:7, :])
    row = lax.broadcasted_iota(jnp.int32, (tm, 1), 0)
    vm1 = jnp.where(row == 0, prev1, pltpu.roll(v, 1, 0))
    vm2 = jnp.where(row == 0, prev2, jnp.where(row == 1, prev1, pltpu.roll(v, 2, 0)))
    y_b = _conv_taps(v, vm1, vm2, bb_ref[...], cw_ref)
    tail_ref[...] = v[tm - 8:tm, :]
    part = part + _out_part(y_b, bg_ref[...], 1, gn_ref, wout_ref)
    part = part + _out_part(y_m, mg_ref[...], 3, gn_ref, wout_ref)
    x_new = x_ref[...] + part
    if final:
        x_new = _rms(x_new, fg_ref[...])
    o_ref[...] = x_new


def merge_prompt(x2d, h2d, y_a, y_c, conv_buf, mkv_t, gn, conv_w, wout_bf16, final_g,
                 n_seq, t_len, final):
    d = x2d.shape[1]
    n_mem = mkv_t.shape[2]
    tm = min(t_len, MERGE_ROWS)
    nb = t_len // tm
    col = PROMPT_COL
    hblk = lambda c: pl.BlockSpec((tm, GROUP_W), lambda n, i, c=c: (n * nb + i, col[c]))
    halo = lambda c: pl.BlockSpec(
        (8, GROUP_W), lambda n, i, c=c: (jnp.maximum((n * nb + i) * (tm // 8) - 1, 0), col[c]))
    rows = pl.BlockSpec((tm, GROUP_W), lambda n, i: (n * nb + i, 0))
    full = lambda shape: pl.BlockSpec(shape, lambda n, i: (0,) * len(shape))
    out, tail = pl.pallas_call(
        functools.partial(_merge_prompt_kernel, tm, final),
        grid=(n_seq, nb),
        in_specs=[pl.BlockSpec((tm, d), lambda n, i: (n * nb + i, 0)),
                  hblk(A_G), hblk(B_B), hblk(B_C), hblk(B_X), hblk(B_G), hblk(C_G),
                  hblk(M_Q), hblk(M_G), halo(B_C), halo(B_X),
                  pl.BlockSpec((None, CONV_K - 1, GROUP_W), lambda n, i: (n, 0, 0)),
                  rows, rows,
                  pl.BlockSpec((None, GROUP_W, n_mem), lambda n, i: (n, 0, 0)),
                  pl.BlockSpec((None, GROUP_W, n_mem), lambda n, i: (n, 1, 0)),
                  full((4, GROUP_W)), full((CONV_K, GROUP_W)), full(wout_bf16.shape),
                  full((1, d))],
        out_specs=[pl.BlockSpec((tm, d), lambda n, i: (n * nb + i, 0)),
                   pl.BlockSpec((None, 8, GROUP_W), lambda n, i: (n, 0, 0))],
        out_shape=[jax.ShapeDtypeStruct(x2d.shape, F32),
                   jax.ShapeDtypeStruct((n_seq, 8, GROUP_W), F32)],
        compiler_params=_cparams("parallel", "arbitrary"),
        name="merge_prompt",
    )(x2d, h2d, h2d, h2d, h2d, h2d, h2d, h2d, h2d, h2d, h2d, conv_buf, y_a, y_c, mkv_t, mkv_t,
      gn, conv_w, wout_bf16, final_g.reshape(1, d))
    return out, tail[:, 8 - (CONV_K - 1):, :]


def _merge_sample_kernel(t_len, seqs, final,
                         x_ref, ag_ref, bb_ref, bc_ref, bx_ref, bg_ref, cg_ref, mq_ref, mg_ref,
                         buf_ref, ya_ref, yc_ref, mkt_ref, mvt_ref,
                         gn_ref, cw_ref, wout_ref, fg_ref,
                         o_ref, v_ref):
    tm = seqs * t_len
    row = lax.broadcasted_iota(jnp.int32, (tm, 1), 0)
    seq_of_row = row // t_len
    t_of_row = row % t_len

    v = bc_ref[...] * bx_ref[...]
    v_ref[...] = v
    prev1 = jnp.zeros((tm, GROUP_W), F32)
    prev2 = jnp.zeros((tm, GROUP_W), F32)
    for s in range(seqs):
        prev1 = jnp.where(seq_of_row == s, buf_ref[s, 1:2, :], prev1)
        prev2 = jnp.where(seq_of_row == s, buf_ref[s, 0:1, :], prev2)
    vm1 = jnp.where(t_of_row == 0, prev1, pltpu.roll(v, 1, 0))
    vm2 = jnp.where(t_of_row == 0, prev2, jnp.where(t_of_row == 1, prev1, pltpu.roll(v, 2, 0)))
    y_b = _conv_taps(v, vm1, vm2, bb_ref[...], cw_ref)

    heads = [jnp.zeros((tm, HEAD_DIM), F32)] * N_HEADS
    for s in range(seqs):
        scores = _mem_scores(mq_ref, lambda h, s=s: mkt_ref[s, _head(h), :])
        for h in range(N_HEADS):
            o = _mem_values(scores[h], mvt_ref[s, _head(h), :])
            heads[h] = jnp.where(seq_of_row == s, o, heads[h])
    y_m = jnp.concatenate(heads, axis=-1)

    part = _out_part(ya_ref[...], ag_ref[...], 0, gn_ref, wout_ref)
    part = part + _out_part(y_b, bg_ref[...], 1, gn_ref, wout_ref)
    part = part + _out_part(yc_ref[...], cg_ref[...], 2, gn_ref, wout_ref)
    part = part + _out_part(y_m, mg_ref[...], 3, gn_ref, wout_ref)
    x_new = x_ref[...] + part
    o_ref[...] = _rms(x_new, fg_ref[...]) if final else x_new


def merge_sample(x2d, h2d, y_a, y_c, state_conv, cache_mkt, cache_mvt, gn, conv_w, wout_bf16,
                 final_g, n_seq, t_len, layer, final):
    d = x2d.shape[1]
    n_mem = cache_mkt.shape[3]
    seqs = 16 // t_len
    tm = seqs * t_len
    hblk = lambda c: pl.BlockSpec((tm, GROUP_W), lambda i, c=c: (i, c))
    rows = pl.BlockSpec((tm, GROUP_W), lambda i: (i, 0))
    full = lambda shape: pl.BlockSpec(shape, lambda i: (0,) * len(shape))
    mem = pl.BlockSpec((seqs, None, GROUP_W, n_mem), lambda i: (i, layer, 0, 0))
    return pl.pallas_call(
        functools.partial(_merge_sample_kernel, t_len, seqs, final),
        grid=(n_seq // seqs,),
        in_specs=[pl.BlockSpec((tm, d), lambda i: (i, 0)),
                  hblk(A_G), hblk(B_B), hblk(B_C), hblk(B_X), hblk(B_G), hblk(C_G),
                  hblk(M_Q), hblk(M_G),
                  pl.BlockSpec((seqs, None, CONV_K - 1, GROUP_W), lambda i: (i, layer, 0, 0)),
                  rows, rows, mem, mem,
                  full((4, GROUP_W)), full((CONV_K, GROUP_W)), full(wout_bf16.shape),
                  full((1, d))],
        out_specs=[pl.BlockSpec((tm, d), lambda i: (i, 0)), rows],
        out_shape=[jax.ShapeDtypeStruct(x2d.shape, F32),
                   jax.ShapeDtypeStruct((x2d.shape[0], GROUP_W), F32)],
        compiler_params=_cparams("parallel"),
        name="merge_sample",
    )(x2d, h2d, h2d, h2d, h2d, h2d, h2d, h2d, h2d, state_conv, y_a, y_c, cache_mkt, cache_mvt,
      gn, conv_w, wout_bf16, final_g.reshape(1, d))


def _time_major(a, n_seq, t_len):
    return a.reshape(n_seq, t_len, -1).transpose(1, 0, 2).reshape(n_seq * t_len, -1)


def _seq_major(a, n_seq, t_len):
    return a.reshape(t_len, n_seq, -1).transpose(1, 0, 2).reshape(n_seq * t_len, -1)


def _col(h2d, c):
    return h2d[:, c * GROUP_W:(c + 1) * GROUP_W]


def _pos_minor(a):
    lead = a.shape[:-3]
    pos, heads, dim = a.shape[-3:]
    nd = len(lead)
    return a.transpose(*range(nd), nd + 1, nd + 2, nd).reshape(*lead, heads * dim, pos)


def _from_pos_minor(a_t):
    lead = a_t.shape[:-2]
    pos = a_t.shape[-1]
    nd = len(lead)
    a = a_t.reshape(*lead, N_HEADS, HEAD_DIM, pos)
    return a.transpose(*range(nd), nd + 2, nd, nd + 1)


def kernel(x_prompt, x_sample, cache_sb_k, cache_sb_v, state_ssm_re, state_ssm_im, state_conv,
           cache_mem_k, cache_mem_v, page_table, mem_prompt, norm_g, w_in, w_out, group_norm_g,
           ssm_lambda_re, ssm_lambda_im, ssm_b_re, ssm_b_im, ssm_c_re, ssm_c_im, ssm_log_dt, ssm_d,
           ssm_w_glu, conv_w, sb_bias, w_mem_kv, final_norm_g):
    n_p, l_p, d = x_prompt.shape
    n_s, l_s, _ = x_sample.shape
    depth = w_in.shape[0]
    assert l_s >= CONV_K - 1 and 16 % l_s == 0 and n_p % 8 == 0 and n_s % 8 == 0
    tq = min(l_p, Q_TILE)

    xp = x_prompt.reshape(n_p * l_p, d)
    xs = x_sample.reshape(n_s * l_s, d)
    cache_kt = _pos_minor(cache_sb_k)
    cache_vt = _pos_minor(cache_sb_v)
    cache_mkt = _pos_minor(cache_mem_k)
    cache_mvt = _pos_minor(cache_mem_v)
    zeros_h = jnp.zeros((n_p, SSM_W), F32)
    zeros_buf = jnp.zeros((n_p, CONV_K - 1, GROUP_W), F32)

    kv_t = None
    p_re, p_im, p_conv, p_mkvt = [], [], [], []
    s_k, s_v, s_re, s_im, s_conv = [], [], [], [], []
    for i in range(depth):
        final = i == depth - 1
        w_in_bf = w_in[i].astype(BF16)
        w_out_bf = w_out[i].astype(BF16)

        lbre, lbim, bbre, bbim = ssm_discretise(ssm_lambda_re[i], ssm_lambda_im[i], ssm_log_dt[i],
                                                ssm_b_re[i], ssm_b_im[i])
        sp = {"lbre": lbre, "lbim": lbim,
              "bre": _block_diag_in(bbre).astype(BF16), "bim": _block_diag_in(bbim).astype(BF16),
              "cre": _block_diag_out(ssm_c_re[i]).astype(BF16),
              "cimn": _block_diag_out(-ssm_c_im[i]).astype(BF16),
              "d": ssm_d[i].reshape(1, GROUP_W), "wglu": ssm_w_glu[i].astype(BF16)}

        mkv_t = mem_kv_t(mem_prompt, w_mem_kv[i])
        h_p, k_rm, q_t, v_tb, *kv_t = in_proj_prompt(xp, norm_g[i], w_in_bf, n_p, l_p, tq, i,
                                                     depth, kv_t)
        h_s = in_proj(xs, norm_g[i], w_in_bf)
        seq3 = lambda c: _col(h_s, c).reshape(n_s, l_s, GROUP_W)

        y_c, y_cs = sb_attention(q_t, k_rm, v_tb, seq3(C_Q), seq3(C_K), seq3(C_V), sb_bias[i],
                                 cache_kt, cache_vt, page_table, i, n_p, l_p, tq)

        y_a, h_re, h_im = ssm_scan(h_p.reshape(n_p, l_p, -1), PROMPT_COL[A_U], zeros_h, zeros_h,
                                   sp, n_p, l_p, True)
        y_a = y_a.reshape(n_p * l_p, GROUP_W)
        xp, conv_p = merge_prompt(xp, h_p, y_a, y_c, zeros_buf, mkv_t, group_norm_g[i], conv_w[i],
                                  w_out_bf, final_norm_g, n_p, l_p, final)
        p_re.append(h_re.reshape(n_p, SSM_GROUPS, SSM_STATE))
        p_im.append(h_im.reshape(n_p, SSM_GROUPS, SSM_STATE))
        p_conv.append(conv_p)
        p_mkvt.append(mkv_t)

        ya_tm, h_re, h_im = ssm_scan(_time_major(_col(h_s, A_U), n_s, l_s), 0,
                                     state_ssm_re[:, i].reshape(n_s, SSM_W),
                                     state_ssm_im[:, i].reshape(n_s, SSM_W), sp, n_s, l_s, False)
        y_a = _seq_major(ya_tm, n_s, l_s)
        xs, v_conv = merge_sample(xs, h_s, y_a, y_cs.reshape(n_s * l_s, GROUP_W), state_conv,
                                  cache_mkt, cache_mvt, group_norm_g[i], conv_w[i], w_out_bf,
                                  final_norm_g, n_s, l_s, i, final)
        s_k.append(seq3(C_K).reshape(n_s, l_s, N_HEADS, HEAD_DIM))
        s_v.append(seq3(C_V).reshape(n_s, l_s, N_HEADS, HEAD_DIM))
        s_re.append(h_re.reshape(n_s, SSM_GROUPS, SSM_STATE))
        s_im.append(h_im.reshape(n_s, SSM_GROUPS, SSM_STATE))
        s_conv.append(v_conv.reshape(n_s, l_s, GROUP_W)[:, l_s - (CONV_K - 1):, :])

    stack = lambda xs_: jnp.stack(xs_, axis=1)
    mkv = stack(p_mkvt)
    return (xp.reshape(n_p, l_p, d), xs.reshape(n_s, l_s, d),
            _from_pos_minor(kv_t[0]), _from_pos_minor(kv_t[1]),
            stack(p_re), stack(p_im), stack(p_conv),
            _from_pos_minor(mkv[:, :, :GROUP_W]), _from_pos_minor(mkv[:, :, GROUP_W:]),
            stack(s_k), stack(s_v), stack(s_re), stack(s_im), stack(s_conv))
```

```python
import functools

import jax
import jax.numpy as jnp
from jax import lax
from jax.experimental import pallas as pl
from jax.experimental.pallas import tpu as pltpu

F32 = jnp.float32
BF16 = jnp.bfloat16

EPS = 1e-6
GROUP_W = 256
N_IN_BLOCKS = 12
HEAD_DIM = 64
N_HEADS = 4
ATT_SCALE = HEAD_DIM ** -0.5
LOG2E = 1.4426950408889634
SSM_GROUPS = 16
SSM_CH = 16
SSM_STATE = 64
SSM_W = SSM_GROUPS * SSM_STATE
CONV_K = 3
LANES = 128

(A_U, A_G, B_B, B_C, B_X, B_G, C_Q, C_K, C_V, C_G, M_Q, M_G) = range(N_IN_BLOCKS)
PROMPT_BLOCKS = (A_U, A_G, B_B, B_C, B_X, B_G, C_G, M_Q, M_G)
PROMPT_COL = {b: j for j, b in enumerate(PROMPT_BLOCKS)}

VMEM_LIMIT = 48 * 1024 * 1024
Q_TILE = 256
ROW_TILE = 256
MERGE_ROWS = 512
SSM_CHUNK = 128
CHUNK_LEAD = 8
K_RING = CHUNK_LEAD + 1
V_RING = CHUNK_LEAD + 2


def _cparams(*sem):
    return pltpu.CompilerParams(dimension_semantics=sem, vmem_limit_bytes=VMEM_LIMIT)


def _dot(a, b):
    return jnp.dot(a, b, preferred_element_type=F32)


def _dot_nt(a, b):
    return lax.dot_general(a, b, (((1,), (1,)), ((), ())), preferred_element_type=F32)


def _rms(x, g):
    return x * lax.rsqrt(jnp.mean(x * x, axis=-1, keepdims=True) + EPS) * g


def _silu(x):
    return x * jax.nn.sigmoid(x)


def _head(h):
    return slice(h * HEAD_DIM, (h + 1) * HEAD_DIM)


def _inproj_kernel(x_ref, g_ref, w_ref, o_ref):
    xn = _rms(x_ref[...], g_ref[...])
    o_ref[...] = _dot(xn.astype(BF16), w_ref[...])


def in_proj(x2d, g, w_bf16):
    rows, d = x2d.shape
    c = w_bf16.shape[1]
    tm = min(rows, ROW_TILE)
    return pl.pallas_call(
        _inproj_kernel,
        grid=(rows // tm,),
        in_specs=[pl.BlockSpec((tm, d), lambda i: (i, 0)),
                  pl.BlockSpec((1, d), lambda i: (0, 0)),
                  pl.BlockSpec((d, c), lambda i: (0, 0))],
        out_specs=pl.BlockSpec((tm, c), lambda i: (i, 0)),
        out_shape=jax.ShapeDtypeStruct((rows, c), F32),
        compiler_params=_cparams("parallel"),
        name="in_proj",
    )(x2d, g.reshape(1, d), w_bf16)


def _inproj_prompt_kernel(tq, x_ref, g_ref, w_ref, *refs):
    h_ref, krm_ref, qt_ref, vtb_ref, kt_ref, vt_ref = refs[-6:]
    blk = lambda b: slice(b * GROUP_W, (b + 1) * GROUP_W)
    xn = _rms(x_ref[...], g_ref[...]).astype(BF16)
    hm = _dot(xn, w_ref[...])
    h_ref[...] = jnp.concatenate([hm[:, blk(b)] for b in PROMPT_BLOCKS], axis=1)
    k = hm[:, blk(C_K)]
    krm_ref[...] = k.astype(BF16)
    q = hm[:, blk(C_Q)] * (ATT_SCALE * LOG2E)
    v = hm[:, blk(C_V)]
    slots = range(kt_ref.shape[0]) if len(kt_ref.shape) == 3 else (None,)
    for j in range(x_ref.shape[0] // tq):
        rows = slice(j * tq, (j + 1) * tq)
        k_t = k[rows].T
        v_t = v[rows].T
        for slot in slots:
            idx = (slice(None), rows) if slot is None else (slot, slice(None), rows)
            kt_ref[idx] = k_t
            vt_ref[idx] = v_t
        qt_ref[j] = q[rows].T.astype(BF16)
        vtb_ref[j] = v_t.astype(BF16)


def in_proj_prompt(x2d, g, w_bf16, n_seq, t_len, tq, layer, depth, kv_t):
    rows, d = x2d.shape
    per = 2 if t_len % (2 * tq) == 0 else 1
    tm = per * tq
    nb = t_len // tm
    n_main = len(PROMPT_BLOCKS) * GROUP_W
    full = lambda shape: pl.BlockSpec(shape, lambda n, i: (0,) * len(shape))
    blocked = pl.BlockSpec((None, per, GROUP_W, tq), lambda n, i: (n, i, 0, 0))
    prior = () if kv_t is None else tuple(kv_t)
    if prior:
        final_t = pl.BlockSpec((None, None, GROUP_W, tm), lambda n, i: (n, layer, 0, i))
    else:
        final_t = pl.BlockSpec((None, depth, GROUP_W, tm), lambda n, i: (n, 0, 0, i))
    kv_shape = jax.ShapeDtypeStruct((n_seq, depth, GROUP_W, t_len), F32)
    return pl.pallas_call(
        functools.partial(_inproj_prompt_kernel, tq),
        grid=(n_seq, nb),
        in_specs=[pl.BlockSpec((tm, d), lambda n, i: (n * nb + i, 0)),
                  full((1, d)), full(w_bf16.shape)]
        + [pl.BlockSpec(memory_space=pl.ANY)] * len(prior),
        out_specs=[pl.BlockSpec((tm, n_main), lambda n, i: (n * nb + i, 0)),
                   pl.BlockSpec((tm, GROUP_W), lambda n, i: (n * nb + i, 0)),
                   blocked, blocked, final_t, final_t],
        out_shape=[jax.ShapeDtypeStruct((rows, n_main), F32),
                   jax.ShapeDtypeStruct((rows, GROUP_W), BF16),
                   jax.ShapeDtypeStruct((n_seq, t_len // tq, GROUP_W, tq), BF16),
                   jax.ShapeDtypeStruct((n_seq, t_len // tq, GROUP_W, tq), BF16),
                   kv_shape, kv_shape],
        input_output_aliases={3: 4, 4: 5} if prior else {},
        compiler_params=_cparams("parallel", "parallel"),
        name="in_proj_prompt",
    )(x2d, g.reshape(1, d), w_bf16, *prior)


def _memkv_kernel(x_ref, wt_ref, o_ref):
    o_ref[...] = _dot_nt(wt_ref[...], x_ref[...].astype(BF16))


def mem_kv_t(mem, w_mem):
    n_seq, n_mem, d = mem.shape
    w_t = w_mem.T.astype(BF16)
    return pl.pallas_call(
        _memkv_kernel,
        grid=(n_seq,),
        in_specs=[pl.BlockSpec((None, n_mem, d), lambda n: (n, 0, 0)),
                  pl.BlockSpec(w_t.shape, lambda n: (0, 0))],
        out_specs=pl.BlockSpec((None, w_t.shape[0], n_mem), lambda n: (n, 0, 0)),
        out_shape=jax.ShapeDtypeStruct((n_seq, w_t.shape[0], n_mem), F32),
        compiler_params=_cparams("parallel"),
        name="mem_kv",
    )(mem, w_t)


def _ssm_disc_kernel(lre_ref, lim_ref, dt_ref, bre_ref, bim_ref,
                     lbre_ref, lbim_ref, bbre_ref, bbim_ref):
    lre = lre_ref[...]
    lim = lim_ref[...]
    dt = jnp.exp(dt_ref[...])
    mag = jnp.exp(lre * dt)
    lbre = mag * jnp.cos(lim * dt)
    lbim = mag * jnp.sin(lim * dt)
    lbre_ref[...] = lbre
    lbim_ref[...] = lbim
    nre = lbre - 1.0
    nim = lbim
    den = lre * lre + lim * lim
    cre = (nre * lre + nim * lim) / den
    cim = (nim * lre - nre * lim) / den
    bre = bre_ref[...]
    bim = bim_ref[...]
    bbre_ref[...] = cre * bre - cim * bim
    bbim_ref[...] = cre * bim + cim * bre


def ssm_discretise(lam_re, lam_im, log_dt, b_re, b_im):
    col = lambda a: a.reshape(SSM_W, 1)
    dt_col = jnp.broadcast_to(log_dt[:, None], (SSM_GROUPS, SSM_STATE)).reshape(SSM_W, 1)
    outs = pl.pallas_call(
        _ssm_disc_kernel,
        out_shape=[jax.ShapeDtypeStruct((SSM_W, 1), F32)] * 2
        + [jax.ShapeDtypeStruct((SSM_W, SSM_CH), F32)] * 2,
        name="ssm_discretise",
    )(col(lam_re), col(lam_im), dt_col, b_re.reshape(SSM_W, SSM_CH), b_im.reshape(SSM_W, SSM_CH))
    lbre, lbim, bbre, bbim = outs
    shp = (SSM_GROUPS, SSM_STATE, SSM_CH)
    return lbre.reshape(1, SSM_W), lbim.reshape(1, SSM_W), bbre.reshape(shp), bbim.reshape(shp)


def _block_diag_in(b_gpc):
    eye = jnp.eye(SSM_GROUPS, dtype=F32)
    m = b_gpc.transpose(0, 2, 1)[:, :, None, :] * eye[:, None, :, None]
    return m.reshape(SSM_GROUPS * SSM_CH, SSM_W)


def _block_diag_out(c_gcp):
    eye = jnp.eye(SSM_GROUPS, dtype=F32)
    m = c_gcp.transpose(0, 2, 1)[:, :, None, :] * eye[:, None, :, None]
    return m.reshape(SSM_W, SSM_GROUPS * SSM_CH)


def _ssm_pitch(t_chunk):
    p = -(-t_chunk // 8)
    return 8 * (p if p % 2 else p + 1)


def _ssm_kernel(n_seq, t_chunk, seq_major,
                u_ref, h0re_ref, h0im_ref, lre_ref, lim_ref,
                bre_ref, bim_ref,
                cre_ref, cimn_ref, d_ref, wglu_ref,
                y_ref, hre_ref, him_ref,
                bu_re, bu_im, hs_re, hs_im):
    step = pl.program_id(0)
    rows = n_seq * t_chunk

    @pl.when(step == 0)
    def _():
        hre_ref[...] = h0re_ref[...]
        him_ref[...] = h0im_ref[...]

    n_slab = SSM_W // LANES
    slab = lambda j: slice(j * LANES, (j + 1) * LANES)
    pitch = _ssm_pitch(t_chunk) if seq_major else None

    def put(ref, j, val):
        if not seq_major:
            ref[j] = val
            return
        for n in range(n_seq):
            ref[j, n * pitch:n * pitch + t_chunk, :] = val[n * t_chunk:(n + 1) * t_chunk, :]

    def get(ref, j):
        if not seq_major:
            return ref[j]
        return jnp.concatenate(
            [ref[j, n * pitch:n * pitch + t_chunk, :] for n in range(n_seq)], axis=0)

    def rows_of(t):
        if seq_major:
            return pl.ds(t, n_seq, stride=pitch)
        return pl.ds(pl.multiple_of(t * n_seq, n_seq), n_seq)

    halves = [slice(0, rows // 2), slice(rows // 2, rows)] if rows % 16 == 0 else [slice(0, rows)]
    u = u_ref[...].reshape(rows, GROUP_W)
    u_bf = u.astype(BF16)
    b_re = jnp.concatenate([_dot(u_bf[r], bre_ref[...]) for r in halves], axis=0)
    b_im = jnp.concatenate([_dot(u_bf[r], bim_ref[...]) for r in halves], axis=0)
    for j in range(n_slab):
        put(bu_re, j, b_re[:, slab(j)])
        put(bu_im, j, b_im[:, slab(j)])

    lre = [jnp.broadcast_to(lre_ref[:, slab(j)], (n_seq, LANES)) for j in range(n_slab)]
    lim = [jnp.broadcast_to(lim_ref[:, slab(j)], (n_seq, LANES)) for j in range(n_slab)]

    def body(t, carry):
        r = rows_of(t)
        new = []
        for j in range(n_slab):
            hr, hi = carry[j]
            nr = lre[j] * hr - lim[j] * hi + bu_re[j, r, :]
            ni = lre[j] * hi + lim[j] * hr + bu_im[j, r, :]
            hs_re[j, r, :] = nr
            hs_im[j, r, :] = ni
            new.append((nr, ni))
        return tuple(new)

    h0 = tuple((hre_ref[:, slab(j)], him_ref[:, slab(j)]) for j in range(n_slab))
    h_last = lax.fori_loop(0, t_chunk, body, h0, unroll=4)
    for j in range(n_slab):
        hre_ref[:, slab(j)] = h_last[j][0]
        him_ref[:, slab(j)] = h_last[j][1]

    h_re = jnp.concatenate([get(hs_re, j) for j in range(n_slab)], axis=-1).astype(BF16)
    h_im = jnp.concatenate([get(hs_im, j) for j in range(n_slab)], axis=-1).astype(BF16)
    ch = jnp.concatenate([_dot(h_re[r], cre_ref[...]) + _dot(h_im[r], cimn_ref[...])
                          for r in halves], axis=0)
    y = ch + d_ref[...] * u
    y = jax.nn.gelu(y)
    y = y * jax.nn.sigmoid(_dot(y.astype(BF16), wglu_ref[...]))
    y_ref[...] = y.reshape(y_ref.shape)


def ssm_scan(u, u_col, h0_re, h0_im, sp, n_seq, t_len, seq_major):
    t_chunk = min(t_len, SSM_CHUNK)
    tr = t_chunk * n_seq
    s_rows = n_seq * _ssm_pitch(t_chunk) if seq_major else tr
    full = lambda shape: pl.BlockSpec(shape, lambda i: (0,) * len(shape))
    if seq_major:
        u_spec = pl.BlockSpec((n_seq, t_chunk, GROUP_W), lambda i: (0, i, u_col))
        y_spec = pl.BlockSpec((n_seq, t_chunk, GROUP_W), lambda i: (0, i, 0))
        y_shape = jax.ShapeDtypeStruct((n_seq, t_len, GROUP_W), F32)
    else:
        u_spec = pl.BlockSpec((tr, GROUP_W), lambda i: (i, u_col))
        y_spec = pl.BlockSpec((tr, GROUP_W), lambda i: (i, 0))
        y_shape = jax.ShapeDtypeStruct((t_len * n_seq, GROUP_W), F32)
    return pl.pallas_call(
        functools.partial(_ssm_kernel, n_seq, t_chunk, seq_major),
        grid=(t_len // t_chunk,),
        in_specs=[u_spec,
                  full((n_seq, SSM_W)), full((n_seq, SSM_W)),
                  full((1, SSM_W)), full((1, SSM_W)),
                  full((GROUP_W, SSM_W)), full((GROUP_W, SSM_W)),
                  full((SSM_W, GROUP_W)), full((SSM_W, GROUP_W)),
                  full((1, GROUP_W)), full((GROUP_W, GROUP_W))],
        out_specs=[y_spec, full((n_seq, SSM_W)), full((n_seq, SSM_W))],
        out_shape=[y_shape,
                   jax.ShapeDtypeStruct((n_seq, SSM_W), F32),
                   jax.ShapeDtypeStruct((n_seq, SSM_W), F32)],
        scratch_shapes=[pltpu.VMEM((SSM_W // LANES, s_rows, LANES), F32)] * 4,
        compiler_params=_cparams("arbitrary"),
        name="ssm_scan",
    )(u, h0_re, h0_im, sp["lbre"], sp["lbim"],
      sp["bre"], sp["bim"],
      sp["cre"], sp["cimn"], sp["d"], sp["wglu"])


def _softplus(z):
    return jnp.maximum(z, 0.0) + jnp.log(1.0 + jnp.exp(-jnp.abs(z)))


def _softplus2(z2):
    return jnp.maximum(z2, 0.0) + jnp.log(1.0 + jnp.exp2(-jnp.abs(z2))) * LOG2E


def _sb_kernel(tq, t_new, page, n_pages, layer, cp, n_chunks,
               pt_ref, bias_ref,
               qt_ref, k_ref, vt_ref,
               qrep_ref, ck_hbm, cv_hbm, knt_hbm, vnt_hbm,
               o_ref, os_ref,
               z_ref, sp_ref, d_ref, w_ref, acc_ref,
               kbuf, vbuf, sem, qbd_ref, scarry_ref, ws_ref):
    n = pl.program_id(0)
    i = pl.program_id(1)
    nb = pl.num_programs(1)
    slots_per_seq = nb * (nb + 1) // 2
    c_diag = n * slots_per_seq + i * (i + 1) // 2
    chunks_per_seq = n_pages // cp
    n_s = n_chunks // chunks_per_seq
    rows_q = N_HEADS * t_new
    heads = range(N_HEADS)

    def chunk_copies(c, with_new):
        b = c % K_RING
        bv = c % V_RING
        seq = c // chunks_per_seq
        sub = c % chunks_per_seq
        if with_new:
            return [pltpu.make_async_copy(knt_hbm.at[seq], kbuf.at[b, cp], sem.at[0, b]),
                    pltpu.make_async_copy(vnt_hbm.at[seq], vbuf.at[bv, cp], sem.at[1, bv])]
        copies = []
        for g in range(cp):
            phys = pt_ref[seq, n_pages - 1 - (sub * cp + g)]
            copies.append(pltpu.make_async_copy(ck_hbm.at[phys, layer], kbuf.at[b, g],
                                                sem.at[0, b]))
            copies.append(pltpu.make_async_copy(cv_hbm.at[phys, layer], vbuf.at[bv, g],
                                                sem.at[1, bv]))
        return copies

    def start_chunk(c):
        @pl.when(c < n_chunks)
        def _():
            for cpy in chunk_copies(c, False):
                cpy.start()

            @pl.when(c % chunks_per_seq == 0)
            def _():
                for cpy in chunk_copies(c, True):
                    cpy.start()

    jk = lax.broadcasted_iota(jnp.int32, (page, page), 0)
    sk = lax.broadcasted_iota(jnp.int32, (page, page), 1)
    suffix_p = jnp.where(jk > sk, -1.0, 0.0).astype(BF16)
    row_head = lax.broadcasted_iota(jnp.int32, (rows_q, 1), 0) // t_new
    bias_col = jnp.zeros((rows_q, 1), F32)
    for h in heads:
        bias_col = jnp.where(row_head == h, bias_ref[h], bias_col)
    bias_rows = jnp.concatenate([bias_col] * cp, axis=0)

    def open_chunk(c):
        @pl.when(c < n_chunks)
        def _():
            for cpy in chunk_copies(c, False):
                cpy.wait()

            @pl.when(c % chunks_per_seq == 0)
            def _():
                for cpy in chunk_copies(c, True):
                    cpy.wait()
                b = c % K_RING
                seq = c // chunks_per_seq
                lane_head = lax.broadcasted_iota(jnp.int32, (rows_q, GROUP_W), 1) // HEAD_DIM
                qbd = jnp.where(lane_head == row_head, qrep_ref[seq] * ATT_SCALE, 0.0).astype(BF16)
                qbd_ref[...] = qbd
                t_of_row = lax.broadcasted_iota(jnp.int32, (rows_q, page), 0) % t_new
                key = lax.broadcasted_iota(jnp.int32, (rows_q, page), 1)
                z = _dot(qbd, kbuf[b, cp].astype(BF16)) + bias_col
                w, total = _sb_rows_block(z, key < t_of_row, jnp.zeros((rows_q, 1), F32), suffix_p)
                os_ref[seq] = _dot_nt(w.astype(BF16), vbuf[c % V_RING, cp].astype(BF16))
                scarry_ref[...] = total

    def sample_stages(c):
        b = c % K_RING
        valid = c < n_chunks
        st = {}

        def prev_values():
            p = jnp.maximum(c - 1, 0)
            ok = jnp.logical_and(c >= 1, p < n_chunks)
            seq = jnp.minimum(p // chunks_per_seq, n_s - 1)
            w_prev = ws_ref[p % 2]
            old = os_ref[seq]
            acc = old
            for g in range(cp):
                acc = acc + _dot_nt(w_prev[g * rows_q:(g + 1) * rows_q, :],
                                    vbuf[p % V_RING, g].astype(BF16))
            os_ref[seq] = jnp.where(ok, acc, old)

        def scores():
            qbd = qbd_ref[...]
            st["z"] = jnp.concatenate(
                [_dot(qbd, kbuf[b, g].astype(BF16)) for g in range(cp)], axis=0) + bias_rows

        def softplus_and_suffix():
            st["sp"] = _softplus(st["z"])
            st["sp_bf"] = st["sp"].astype(BF16)
            st["local"] = _dot(st["sp_bf"], suffix_p)

        def weights():
            local_s = st["local"]
            totals = local_s[:, 0:1] - st["sp_bf"][:, 0:1].astype(F32)
            old = scarry_ref[...]
            carry = old
            carries_s = []
            for g in range(cp):
                carries_s.append(carry)
                carry = carry + totals[g * rows_q:(g + 1) * rows_q, :]
            scarry_ref[...] = jnp.where(valid, carry, old)
            ws_ref[c % 2] = jnp.exp((st["z"] - st["sp"])
                                    + (local_s + jnp.concatenate(carries_s, axis=0))).astype(BF16)

        return (prev_values, scores, softplus_and_suffix, weights)

    s_idx = lax.broadcasted_iota(jnp.int32, (tq, tq), 0)
    t_idx = lax.broadcasted_iota(jnp.int32, (tq, tq), 1)
    neg_suffix = jnp.where(t_idx > s_idx, -1.0, 0.0).astype(BF16)
    diag_mask = s_idx < t_idx

    def block_of(t):
        return jnp.maximum(i - t, 0)

    def scores(t):
        rows = pl.ds(pl.multiple_of(block_of(t) * tq, tq), tq)
        for h in heads:
            z_ref[t % 2, h] = _dot(k_ref[rows, _head(h)], qt_ref[_head(h), :])

    def front(t, mask):
        for h in heads:
            z = z_ref[t % 2, h] + bias_ref[h] * LOG2E
            sp = _softplus2(z)
            d_ref[t % 2, h] = z - sp
            sp_m = sp if mask is None else jnp.where(mask, sp, 0.0)
            sp_ref[t % 2, h] = sp_m.astype(BF16)

    def suffix_sums(t):
        return [_dot(neg_suffix, sp_ref[t % 2, h]) for h in heads]

    def weights(t, local, carries, mask):
        new_carries = []
        for h in heads:
            w = jnp.exp2(d_ref[t % 2, h] + (local[h] + carries[h]))
            if mask is not None:
                w = jnp.where(mask, w, 0.0)
            w_ref[h] = w.astype(BF16)
            new_carries.append(
                carries[h] + (local[h][0:1, :] - sp_ref[t % 2, h, 0:1, :].astype(F32)))
        return tuple(new_carries)

    def last(t):
        kb = block_of(t)
        for h in heads:
            acc_ref[_head(h), :] += _dot(vt_ref[kb, _head(h), :], w_ref[h])

    @pl.when(c_diag == 0)
    def _():
        for ahead in range(CHUNK_LEAD):
            start_chunk(c_diag + ahead)
        ws_ref[...] = jnp.zeros(ws_ref.shape, BF16)

    acc_ref[...] = jnp.zeros(acc_ref.shape, F32)
    open_chunk(c_diag)
    scores(0)
    scores(1)
    front(0, diag_mask)

    def trip(t, carries, mask):
        prev_values, sample_scores, sample_suffix, sample_weights = sample_stages(c_diag + t)
        sample_scores()
        local = suffix_sums(t)
        front(t + 1, None)
        sample_suffix()
        prev_values()
        scores(t + 2)
        carries = weights(t, local, carries, mask)
        sample_weights()
        last(t)
        start_chunk(c_diag + t + CHUNK_LEAD)
        return carries

    zero = jnp.zeros((1, tq), F32)
    carries = trip(0, (zero,) * N_HEADS, diag_mask)

    def body(t, carries):
        open_chunk(c_diag + t)
        return trip(t, carries, None)

    carries = lax.fori_loop(1, i, body, carries)

    @pl.when(i >= 1)
    def _():
        open_chunk(c_diag + i)
        prev_values, sample_scores, sample_suffix, sample_weights = sample_stages(c_diag + i)
        sample_scores()
        local = suffix_sums(i)
        sample_suffix()
        prev_values()
        weights(i, local, carries, None)
        sample_weights()
        last(i)
        start_chunk(c_diag + i + CHUNK_LEAD)

    o_ref[...] = acc_ref[...].T


def _chunk_pages(n_pages, n_chunks_max):
    for cp in range(1, n_pages + 1):
        if n_pages % cp == 0 and n_pages // cp <= n_chunks_max:
            return cp
    raise ValueError("prompt too short to carry the sample group's pages")


def sb_attention(q_t, k_rm, v_t, q_s, k_new, v_new, bias, cache_kt, cache_vt, page_table, layer,
                 n_seq, t_len, tq):
    nb = t_len // tq
    n_s, t_new, _ = q_s.shape
    page = cache_kt.shape[3]
    n_pages = page_table.shape[1]
    n_slots = n_seq * nb * (nb + 1) // 2
    cp = _chunk_pages(n_pages, (n_slots - 1) // n_s)
    n_chunks = n_s * (n_pages // cp)
    rows_q = N_HEADS * t_new
    q_rep = jnp.tile(q_s, (1, N_HEADS, 1))
    pad = ((0, 0), (0, 0), (0, page - t_new))
    k_new_t = jnp.pad(k_new.transpose(0, 2, 1), pad)
    v_new_t = jnp.pad(v_new.transpose(0, 2, 1), pad)
    full3 = lambda shape: pl.BlockSpec(shape, lambda n, i, pt: (0, 0, 0))
    hbm = pl.BlockSpec(memory_space=pl.ANY)
    y_p, acc = pl.pallas_call(
        functools.partial(_sb_kernel, tq, t_new, page, n_pages, layer, cp, n_chunks),
        grid_spec=pltpu.PrefetchScalarGridSpec(
            num_scalar_prefetch=1,
            grid=(n_seq, nb),
            in_specs=[pl.BlockSpec(memory_space=pltpu.SMEM),
                      pl.BlockSpec((None, None, GROUP_W, tq), lambda n, i, pt: (n, i, 0, 0)),
                      pl.BlockSpec((t_len, GROUP_W), lambda n, i, pt: (n, 0)),
                      pl.BlockSpec((None, nb, GROUP_W, tq), lambda n, i, pt: (n, 0, 0, 0)),
                      full3((n_s, rows_q, GROUP_W)),
                      hbm, hbm, hbm, hbm],
            out_specs=[pl.BlockSpec((tq, GROUP_W), lambda n, i, pt: (n * nb + i, 0)),
                       full3((n_s, rows_q, GROUP_W))],
            scratch_shapes=[pltpu.VMEM((2, N_HEADS, tq, tq), F32),
                            pltpu.VMEM((2, N_HEADS, tq, tq), BF16),
                            pltpu.VMEM((2, N_HEADS, tq, tq), F32),
                            pltpu.VMEM((N_HEADS, tq, tq), BF16),
                            pltpu.VMEM((GROUP_W, tq), F32),
                            pltpu.VMEM((K_RING, cp + 1, GROUP_W, page), F32),
                            pltpu.VMEM((V_RING, cp + 1, GROUP_W, page), F32),
                            pltpu.SemaphoreType.DMA((2, V_RING)),
                            pltpu.VMEM((rows_q, GROUP_W), BF16),
                            pltpu.VMEM((rows_q, 1), F32),
                            pltpu.VMEM((2, cp * rows_q, page), BF16)]),
        out_shape=[jax.ShapeDtypeStruct((n_seq * t_len, GROUP_W), F32),
                   jax.ShapeDtypeStruct((n_s, rows_q, GROUP_W), F32)],
        compiler_params=_cparams("arbitrary", "arbitrary"),
        name="sb_attention",
    )(page_table, bias, q_t, k_rm, v_t, q_rep, cache_kt, cache_vt, k_new_t, v_new_t)
    acc = acc.reshape(n_s, N_HEADS, t_new, N_HEADS, HEAD_DIM)
    picked = [acc[:, h, :, h, :] for h in range(N_HEADS)]
    return y_p, jnp.stack(picked, axis=2).reshape(n_s, t_new, GROUP_W)


def _sb_rows_block(z, mask, carry, neg_suffix):
    sp = _softplus(z)
    sp_m = sp if mask is None else jnp.where(mask, sp, 0.0)
    sp_bf = sp_m.astype(BF16)
    local = _dot(sp_bf, neg_suffix)
    w = jnp.exp((z - sp) + (local + carry))
    if mask is not None:
        w = jnp.where(mask, w, 0.0)
    return w, local[:, 0:1] - sp_bf[:, 0:1].astype(F32)


def _mem_scores(mq_ref, mkt_of_head):
    return [_dot((mq_ref[:, _head(h)] * ATT_SCALE).astype(BF16), mkt_of_head(h).astype(BF16))
            for h in range(N_HEADS)]


def _mem_values(s, mvt_h):
    e = jnp.exp(s - jnp.max(s, axis=-1, keepdims=True))
    return _dot_nt(e.astype(BF16), mvt_h.astype(BF16)) / jnp.sum(e, axis=-1, keepdims=True)


def _out_part(y, gate, group, gn_ref, wout_ref):
    m = _rms(y, gn_ref[group:group + 1, :]) * _silu(gate)
    return _dot(m.astype(BF16), wout_ref[group * GROUP_W:(group + 1) * GROUP_W, :])


def _conv_taps(v, vm1, vm2, b_gate, cw_ref):
    return b_gate * (vm2 * cw_ref[0:1, :] + vm1 * cw_ref[1:2, :] + v * cw_ref[2:3, :])


def _merge_prompt_kernel(tm, final,
                         x_ref, ag_ref, bb_ref, bc_ref, bx_ref, bg_ref, cg_ref, mq_ref, mg_ref,
                         hbc_ref, hbx_ref, buf_ref, ya_ref, yc_ref, mkt_ref, mvt_ref,
                         gn_ref, cw_ref, wout_ref, fg_ref,
                         o_ref, tail_ref):
    i = pl.program_id(1)
    scores = _mem_scores(mq_ref, lambda h: mkt_ref[_head(h), :])
    y_m = jnp.concatenate([_mem_values(scores[h], mvt_ref[_head(h), :]) for h in range(N_HEADS)],
                          axis=-1)
    part = _out_part(ya_ref[...], ag_ref[...], 0, gn_ref, wout_ref)
    part = part + _out_part(yc_ref[...], cg_ref[...], 2, gn_ref, wout_ref)

    v = bc_ref[...] * bx_ref[...]
    halo = hbc_ref[...] * hbx_ref[...]
    first = i == 0
    prev1 = jnp.where(first, buf_ref[1:2, :], halo[7:8, :])
    prev2 = jnp.where(first, buf_ref[0:1, :], halo[6:7, :])
    row = lax.broadcasted_iota(jnp.int32, (tm, 1), 0)
    vm1 = jnp.where(row == 0, prev1, pltpu.roll(v, 1, 0))
    vm2 = jnp.where(row == 0, prev2, jnp.where(row == 1, prev1, pltpu.roll(v, 2, 0)))
    y_b = _conv_taps(v, vm1, vm2, bb_ref[...], cw_ref)
    tail_ref[...] = v[tm - 8:tm, :]
    part = part + _out_part(y_b, bg_ref[...], 1, gn_ref, wout_ref)
    part = part + _out_part(y_m, mg_ref[...], 3, gn_ref, wout_ref)
    x_new = x_ref[...] + part
    if final:
        x_new = _rms(x_new, fg_ref[...])
    o_ref[...] = x_new


def merge_prompt(x2d, h2d, y_a, y_c, conv_buf, mkv_t, gn, conv_w, wout_bf16, final_g,
                 n_seq, t_len, final):
    d = x2d.shape[1]
    n_mem = mkv_t.shape[2]
    tm = min(t_len, MERGE_ROWS)
    nb = t_len // tm
    col = PROMPT_COL
    hblk = lambda c: pl.BlockSpec((tm, GROUP_W), lambda n, i, c=c: (n * nb + i, col[c]))
    halo = lambda c: pl.BlockSpec(
        (8, GROUP_W), lambda n, i, c=c: (jnp.maximum((n * nb + i) * (tm // 8) - 1, 0), col[c]))
    rows = pl.BlockSpec((tm, GROUP_W), lambda n, i: (n * nb + i, 0))
    full = lambda shape: pl.BlockSpec(shape, lambda n, i: (0,) * len(shape))
    out, tail = pl.pallas_call(
        functools.partial(_merge_prompt_kernel, tm, final),
        grid=(n_seq, nb),
        in_specs=[pl.BlockSpec((tm, d), lambda n, i: (n * nb + i, 0)),
                  hblk(A_G), hblk(B_B), hblk(B_C), hblk(B_X), hblk(B_G), hblk(C_G),
                  hblk(M_Q), hblk(M_G), halo(B_C), halo(B_X),
                  pl.BlockSpec((None, CONV_K - 1, GROUP_W), lambda n, i: (n, 0, 0)),
                  rows, rows,
                  pl.BlockSpec((None, GROUP_W, n_mem), lambda n, i: (n, 0, 0)),
                  pl.BlockSpec((None, GROUP_W, n_mem), lambda n, i: (n, 1, 0)),
                  full((4, GROUP_W)), full((CONV_K, GROUP_W)), full(wout_bf16.shape),
                  full((1, d))],
        out_specs=[pl.BlockSpec((tm, d), lambda n, i: (n * nb + i, 0)),
                   pl.BlockSpec((None, 8, GROUP_W), lambda n, i: (n, 0, 0))],
        out_shape=[jax.ShapeDtypeStruct(x2d.shape, F32),
                   jax.ShapeDtypeStruct((n_seq, 8, GROUP_W), F32)],
        compiler_params=_cparams("parallel", "arbitrary"),
        name="merge_prompt",
    )(x2d, h2d, h2d, h2d, h2d, h2d, h2d, h2d, h2d, h2d, h2d, conv_buf, y_a, y_c, mkv_t, mkv_t,
      gn, conv_w, wout_bf16, final_g.reshape(1, d))
    return out, tail[:, 8 - (CONV_K - 1):, :]


def _merge_sample_kernel(t_len, seqs, final,
                         x_ref, ag_ref, bb_ref, bc_ref, bx_ref, bg_ref, cg_ref, mq_ref, mg_ref,
                         buf_ref, ya_ref, yc_ref, mkt_ref, mvt_ref,
                         gn_ref, cw_ref, wout_ref, fg_ref,
                         o_ref, v_ref):
    tm = seqs * t_len
    row = lax.broadcasted_iota(jnp.int32, (tm, 1), 0)
    seq_of_row = row // t_len
    t_of_row = row % t_len

    v = bc_ref[...] * bx_ref[...]
    v_ref[...] = v
    prev1 = jnp.zeros((tm, GROUP_W), F32)
    prev2 = jnp.zeros((tm, GROUP_W), F32)
    for s in range(seqs):
        prev1 = jnp.where(seq_of_row == s, buf_ref[s, 1:2, :], prev1)
        prev2 = jnp.where(seq_of_row == s, buf_ref[s, 0:1, :], prev2)
    vm1 = jnp.where(t_of_row == 0, prev1, pltpu.roll(v, 1, 0))
    vm2 = jnp.where(t_of_row == 0, prev2, jnp.where(t_of_row == 1, prev1, pltpu.roll(v, 2, 0)))
    y_b = _conv_taps(v, vm1, vm2, bb_ref[...], cw_ref)

    heads = [jnp.zeros((tm, HEAD_DIM), F32)] * N_HEADS
    for s in range(seqs):
        scores = _mem_scores(mq_ref, lambda h, s=s: mkt_ref[s, _head(h), :])
        for h in range(N_HEADS):
            o = _mem_values(scores[h], mvt_ref[s, _head(h), :])
            heads[h] = jnp.where(seq_of_row == s, o, heads[h])
    y_m = jnp.concatenate(heads, axis=-1)

    part = _out_part(ya_ref[...], ag_ref[...], 0, gn_ref, wout_ref)
    part = part + _out_part(y_b, bg_ref[...], 1, gn_ref, wout_ref)
    part = part + _out_part(yc_ref[...], cg_ref[...], 2, gn_ref, wout_ref)
    part = part + _out_part(y_m, mg_ref[...], 3, gn_ref, wout_ref)
    x_new = x_ref[...] + part
    o_ref[...] = _rms(x_new, fg_ref[...]) if final else x_new


def merge_sample(x2d, h2d, y_a, y_c, state_conv, cache_mkt, cache_mvt, gn, conv_w, wout_bf16,
                 final_g, n_seq, t_len, layer, final):
    d = x2d.shape[1]
    n_mem = cache_mkt.shape[3]
    seqs = 16 // t_len
    tm = seqs * t_len
    hblk = lambda c: pl.BlockSpec((tm, GROUP_W), lambda i, c=c: (i, c))
    rows = pl.BlockSpec((tm, GROUP_W), lambda i: (i, 0))
    full = lambda shape: pl.BlockSpec(shape, lambda i: (0,) * len(shape))
    mem = pl.BlockSpec((seqs, None, GROUP_W, n_mem), lambda i: (i, layer, 0, 0))
    return pl.pallas_call(
        functools.partial(_merge_sample_kernel, t_len, seqs, final),
        grid=(n_seq // seqs,),
        in_specs=[pl.BlockSpec((tm, d), lambda i: (i, 0)),
                  hblk(A_G), hblk(B_B), hblk(B_C), hblk(B_X), hblk(B_G), hblk(C_G),
                  hblk(M_Q), hblk(M_G),
                  pl.BlockSpec((seqs, None, CONV_K - 1, GROUP_W), lambda i: (i, layer, 0, 0)),
                  rows, rows, mem, mem,
                  full((4, GROUP_W)), full((CONV_K, GROUP_W)), full(wout_bf16.shape),
                  full((1, d))],
        out_specs=[pl.BlockSpec((tm, d), lambda i: (i, 0)), rows],
        out_shape=[jax.ShapeDtypeStruct(x2d.shape, F32),
                   jax.ShapeDtypeStruct((x2d.shape[0], GROUP_W), F32)],
        compiler_params=_cparams("parallel"),
        name="merge_sample",
    )(x2d, h2d, h2d, h2d, h2d, h2d, h2d, h2d, h2d, state_conv, y_a, y_c, cache_mkt, cache_mvt,
      gn, conv_w, wout_bf16, final_g.reshape(1, d))


def _time_major(a, n_seq, t_len):
    return a.reshape(n_seq, t_len, -1).transpose(1, 0, 2).reshape(n_seq * t_len, -1)


def _seq_major(a, n_seq, t_len):
    return a.reshape(t_len, n_seq, -1).transpose(1, 0, 2).reshape(n_seq * t_len, -1)


def _col(h2d, c):
    return h2d[:, c * GROUP_W:(c + 1) * GROUP_W]


def _pos_minor(a):
    lead = a.shape[:-3]
    pos, heads, dim = a.shape[-3:]
    nd = len(lead)
    return a.transpose(*range(nd), nd + 1, nd + 2, nd).reshape(*lead, heads * dim, pos)


def _from_pos_minor(a_t):
    lead = a_t.shape[:-2]
    pos = a_t.shape[-1]
    nd = len(lead)
    a = a_t.reshape(*lead, N_HEADS, HEAD_DIM, pos)
    return a.transpose(*range(nd), nd + 2, nd, nd + 1)


def kernel(x_prompt, x_sample, cache_sb_k, cache_sb_v, state_ssm_re, state_ssm_im, state_conv,
           cache_mem_k, cache_mem_v, page_table, mem_prompt, norm_g, w_in, w_out, group_norm_g,
           ssm_lambda_re, ssm_lambda_im, ssm_b_re, ssm_b_im, ssm_c_re, ssm_c_im, ssm_log_dt, ssm_d,
           ssm_w_glu, conv_w, sb_bias, w_mem_kv, final_norm_g):
    n_p, l_p, d = x_prompt.shape
    n_s, l_s, _ = x_sample.shape
    depth = w_in.shape[0]
    assert l_s >= CONV_K - 1 and 16 % l_s == 0 and n_p % 8 == 0 and n_s % 8 == 0
    tq = min(l_p, Q_TILE)

    xp = x_prompt.reshape(n_p * l_p, d)
    xs = x_sample.reshape(n_s * l_s, d)
    cache_kt = _pos_minor(cache_sb_k)
    cache_vt = _pos_minor(cache_sb_v)
    cache_mkt = _pos_minor(cache_mem_k)
    cache_mvt = _pos_minor(cache_mem_v)
    zeros_h = jnp.zeros((n_p, SSM_W), F32)
    zeros_buf = jnp.zeros((n_p, CONV_K - 1, GROUP_W), F32)

    kv_t = None
    p_re, p_im, p_conv, p_mkvt = [], [], [], []
    s_k, s_v, s_re, s_im, s_conv = [], [], [], [], []
    for i in range(depth):
        final = i == depth - 1
        w_in_bf = w_in[i].astype(BF16)
        w_out_bf = w_out[i].astype(BF16)

        lbre, lbim, bbre, bbim = ssm_discretise(ssm_lambda_re[i], ssm_lambda_im[i], ssm_log_dt[i],
                                                ssm_b_re[i], ssm_b_im[i])
        sp = {"lbre": lbre, "lbim": lbim,
              "bre": _block_diag_in(bbre).astype(BF16), "bim": _block_diag_in(bbim).astype(BF16),
              "cre": _block_diag_out(ssm_c_re[i]).astype(BF16),
              "cimn": _block_diag_out(-ssm_c_im[i]).astype(BF16),
              "d": ssm_d[i].reshape(1, GROUP_W), "wglu": ssm_w_glu[i].astype(BF16)}

        mkv_t = mem_kv_t(mem_prompt, w_mem_kv[i])
        h_p, k_rm, q_t, v_tb, *kv_t = in_proj_prompt(xp, norm_g[i], w_in_bf, n_p, l_p, tq, i,
                                                     depth, kv_t)
        h_s = in_proj(xs, norm_g[i], w_in_bf)
        seq3 = lambda c: _col(h_s, c).reshape(n_s, l_s, GROUP_W)

        y_c, y_cs = sb_attention(q_t, k_rm, v_tb, seq3(C_Q), seq3(C_K), seq3(C_V), sb_bias[i],
                                 cache_kt, cache_vt, page_table, i, n_p, l_p, tq)

        y_a, h_re, h_im = ssm_scan(h_p.reshape(n_p, l_p, -1), PROMPT_COL[A_U], zeros_h, zeros_h,
                                   sp, n_p, l_p, True)
        y_a = y_a.reshape(n_p * l_p, GROUP_W)
        xp, conv_p = merge_prompt(xp, h_p, y_a, y_c, zeros_buf, mkv_t, group_norm_g[i], conv_w[i],
                                  w_out_bf, final_norm_g, n_p, l_p, final)
        p_re.append(h_re.reshape(n_p, SSM_GROUPS, SSM_STATE))
        p_im.append(h_im.reshape(n_p, SSM_GROUPS, SSM_STATE))
        p_conv.append(conv_p)
        p_mkvt.append(mkv_t)

        ya_tm, h_re, h_im = ssm_scan(_time_major(_col(h_s, A_U), n_s, l_s), 0,
                                     state_ssm_re[:, i].reshape(n_s, SSM_W),
                                     state_ssm_im[:, i].reshape(n_s, SSM_W), sp, n_s, l_s, False)
        y_a = _seq_major(ya_tm, n_s, l_s)
        xs, v_conv = merge_sample(xs, h_s, y_a, y_cs.reshape(n_s * l_s, GROUP_W), state_conv,
                                  cache_mkt, cache_mvt, group_norm_g[i], conv_w[i], w_out_bf,
                                  final_norm_g, n_s, l_s, i, final)
        s_k.append(seq3(C_K).reshape(n_s, l_s, N_HEADS, HEAD_DIM))
        s_v.append(seq3(C_V).reshape(n_s, l_s, N_HEADS, HEAD_DIM))
        s_re.append(h_re.reshape(n_s, SSM_GROUPS, SSM_STATE))
        s_im.append(h_im.reshape(n_s, SSM_GROUPS, SSM_STATE))
        s_conv.append(v_conv.reshape(n_s, l_s, GROUP_W)[:, l_s - (CONV_K - 1):, :])

    stack = lambda xs_: jnp.stack(xs_, axis=1)
    mkv = stack(p_mkvt)
    return (xp.reshape(n_p, l_p, d), xs.reshape(n_s, l_s, d),
            _from_pos_minor(kv_t[0]), _from_pos_minor(kv_t[1]),
            stack(p_re), stack(p_im), stack(p_conv),
            _from_pos_minor(mkv[:, :, :GROUP_W]), _from_pos_minor(mkv[:, :, GROUP_W:]),
            stack(s_k), stack(s_v), stack(s_re), stack(s_im), stack(s_conv))
```
